```python
import math
import jax, jax.numpy as jnp
from jax import lax
import numpy as np

D_MODEL = 1024
BATCH = 8
SEQ = 2048
DEPTH = 1
DEC_BATCH = 128
DEC_SEQ = 4
PAST_LEN = 16384
PAGE_SIZE = 128

M_HEADS = 16
M_HEAD_DIM = 64
M_INNER = M_HEADS * M_HEAD_DIM
M_GROUPS = 4
M_STATE = 128
M_CONV = 4
M_CONV_DIM = M_INNER + 2 * M_GROUPS * M_STATE
M_CHUNK = 64
M_DT_MIN = 0.001
M_DT_MAX = 0.1
H_HEADS = 8
H_KEY = 128
H_VAL = 128
H_QK = H_HEADS * H_KEY
H_INNER = H_HEADS * H_VAL
H_CHUNK = 32
D_FF = 2816
ALPHA = (2.0 * DEPTH) ** 0.25
BETA = (8.0 * DEPTH) ** -0.25
EPS = 1e-5
F32 = jnp.float32
_IN_SIZES = (M_INNER, M_CONV_DIM, M_HEADS, H_QK, H_QK, H_INNER, H_INNER, D_MODEL, D_MODEL)
IN_COLS = sum(_IN_SIZES)

kernel_name = 'hybrid_ssd_hgrn2_macaron_deepnorm_step'


def _split_points():
    pts, acc = [], 0
    for s in _IN_SIZES[:-1]:
        acc += s
        pts.append(acc)
    return pts


def _tm(t):
    return jnp.moveaxis(t, 1, 0)


def _layer_norm(x, g, b):
    xf = x.astype(F32)
    mu = jnp.mean(xf, -1, keepdims=True)
    var = jnp.mean(jnp.square(xf - mu), -1, keepdims=True)
    return ((xf - mu) * lax.rsqrt(var + EPS) * g.astype(F32) + b.astype(F32)).astype(x.dtype)


def _swiglu(x, wg, wu, wd):
    return (jax.nn.silu(x @ wg) * (x @ wu)) @ wd


def _causal_conv(u, buf, w, b):
    length = u.shape[1]
    up = jnp.concatenate([buf.astype(u.dtype), u], axis=1)
    out = b + sum(up[:, k:k + length] * w[k] for k in range(M_CONV))
    return jax.nn.silu(out), up[:, length:]


def _gated_rmsnorm(y, z, w):
    g = y * jax.nn.silu(z.astype(F32))
    gg = g.reshape(g.shape[:-1] + (M_GROUPS, M_INNER // M_GROUPS))
    gg = gg * lax.rsqrt(jnp.mean(gg * gg, -1, keepdims=True) + EPS)
    return gg.reshape(g.shape) * w.astype(F32)


def _ssd(xh, dt, a, b_in, c_in, h0):
    bsz, length = xh.shape[:2]
    csz = math.gcd(length, M_CHUNK)
    nc = length // csz
    rep = M_HEADS // M_GROUPS
    x = xh.reshape(bsz, nc, csz, M_GROUPS, rep, M_HEAD_DIM)
    dt = dt.reshape(bsz, nc, csz, M_GROUPS, rep)
    bm = b_in.reshape(bsz, nc, csz, M_GROUPS, M_STATE)
    cm = c_in.reshape(bsz, nc, csz, M_GROUPS, M_STATE)
    a_cum = jnp.cumsum(dt * a.reshape(M_GROUPS, rep), axis=2)
    causal = jnp.tril(jnp.ones((csz, csz), dtype=bool))[:, :, None, None]
    seg = a_cum[:, :, :, None] - a_cum[:, :, None, :]
    decay = jnp.exp(jnp.where(causal, seg, -jnp.inf))
    cb = jnp.einsum('bctgn,bcsgn->bctsg', cm, bm)
    w_ts = cb[..., None] * decay * dt[:, :, None]
    y_intra = jnp.einsum('bctsgr,bcsgrp->bctgrp', w_ts, x)
    a_last = a_cum[:, :, -1]
    to_end = jnp.exp(a_last[:, :, None] - a_cum) * dt
    chunk_states = jnp.einsum('bcsgn,bcsgrp->bcgrpn', bm, x * to_end[..., None])

    def step(h, inp):
        dec, st, c_c, acum_c = inp
        y_c = jnp.einsum('btgn,bgrpn->btgrp', c_c, h) * jnp.exp(acum_c)[..., None]
        return dec[..., None, None] * h + st, y_c

    h_init = h0.reshape(bsz, M_GROUPS, rep, M_HEAD_DIM, M_STATE)
    h_fin, y_inter = lax.scan(step, h_init, (_tm(jnp.exp(a_last)), _tm(chunk_states), _tm(cm), _tm(a_cum)))
    y = y_intra + _tm(y_inter)
    return y.reshape(bsz, length, M_HEADS, M_HEAD_DIM), h_fin.reshape(bsz, M_HEADS, M_HEAD_DIM, M_STATE)


def _hgrn2(q, log_f, k, v, s0):
    bsz, length = q.shape[:2]
    csz = math.gcd(length, H_CHUNK)
    nc = length // csz
    shp_k = (bsz, nc, csz, H_HEADS, H_KEY)
    q = q.reshape(shp_k)
    log_f = log_f.reshape(shp_k)
    k = k.reshape(shp_k)
    v = v.reshape(bsz, nc, csz, H_HEADS, H_VAL)
    b_cum = jnp.cumsum(log_f, axis=2)
    qd = q * jnp.exp(b_cum)
    kd = k * jnp.exp(-b_cum)
    causal = jnp.tril(jnp.ones((csz, csz), dtype=bool))
    scores = jnp.where(causal, jnp.einsum('bcthk,bcshk->bchts', qd, kd), 0.0)
    y_intra = jnp.einsum('bchts,bcshv->bcthv', scores, v)
    b_last = b_cum[:, :, -1]
    k_end = k * jnp.exp(b_last[:, :, None] - b_cum)

    def step(s, inp):
        dec, ke_c, v_c, qd_c = inp
        y_c = jnp.einsum('bthk,bhkv->bthv', qd_c, s)
        s_new = dec[..., None] * s + jnp.einsum('bshk,bshv->bhkv', ke_c, v_c)
        return s_new, y_c

    s_fin, y_inter = lax.scan(step, s0, (_tm(jnp.exp(b_last)), _tm(k_end), _tm(v), _tm(qd)))
    y = y_intra + _tm(y_inter)
    return y.reshape(bsz, length, H_HEADS, H_VAL), s_fin


def _layer(x, conv_buf, ssm_h, hgrn_s, lb, w):
    (f1_wg, f1_wu, f1_wd, ln1_g, ln1_b, w_in, conv_w, conv_b, dt_bias, a_log, d_skip, m_norm_w, w_m_out,
     h_norm_w, w_h_out, w_o, ln2_g, ln2_b, f2_wg, f2_wu, f2_wd, ln3_g, ln3_b) = w
    dtype = x.dtype
    bsz, length, _ = x.shape
    x = _layer_norm(ALPHA * x + 0.5 * _swiglu(x, f1_wg, f1_wu, f1_wd), ln1_g, ln1_b)
    proj = x @ w_in
    z, xbc, dt_raw, q, f_raw, i_in, og, gate_m, gate_h = jnp.split(proj, _split_points(), axis=-1)
    xbc, new_conv = _causal_conv(xbc, conv_buf, conv_w, conv_b)
    xm, b_in, c_in = jnp.split(xbc.astype(F32), [M_INNER, M_INNER + M_GROUPS * M_STATE], axis=-1)
    xm = xm.reshape(bsz, length, M_HEADS, M_HEAD_DIM)
    dt = jax.nn.softplus(dt_raw.astype(F32) + dt_bias.astype(F32))
    a = -jnp.exp(a_log.astype(F32))
    y_ssd, new_h = _ssd(xm, dt, a, b_in.reshape(bsz, length, M_GROUPS, M_STATE),
                        c_in.reshape(bsz, length, M_GROUPS, M_STATE), ssm_h.astype(F32))
    y_m = y_ssd + d_skip.astype(F32)[:, None] * xm
    y_m = _gated_rmsnorm(y_m.reshape(bsz, length, M_INNER), z, m_norm_w).astype(dtype) @ w_m_out
    fr = f_raw.astype(F32)
    log_f = jnp.log(lb + (1.0 - lb) * jax.nn.sigmoid(fr))
    k = (1.0 - lb) * jax.nn.sigmoid(-fr)
    o, new_s = _hgrn2(q.astype(F32).reshape(bsz, length, H_HEADS, H_KEY),
                      log_f.reshape(bsz, length, H_HEADS, H_KEY),
                      k.reshape(bsz, length, H_HEADS, H_KEY),
                      i_in.astype(F32).reshape(bsz, length, H_HEADS, H_VAL), hgrn_s.astype(F32))
    o = o * lax.rsqrt(jnp.mean(o * o, -1, keepdims=True) + EPS)
    o = o.reshape(bsz, length, H_INNER) * h_norm_w.astype(F32) * jax.nn.silu(og.astype(F32))
    y_h = o.astype(dtype) @ w_h_out
    mix = jax.nn.sigmoid(gate_m) * y_m + jax.nn.sigmoid(gate_h) * y_h
    x = _layer_norm(ALPHA * x + mix @ w_o, ln2_g, ln2_b)
    x = _layer_norm(ALPHA * x + 0.5 * _swiglu(x, f2_wg, f2_wu, f2_wd), ln3_g, ln3_b)
    return x, new_conv.astype(conv_buf.dtype), new_h.astype(ssm_h.dtype), new_s.astype(hgrn_s.dtype)


def setup_inputs(seed: int = 0) -> dict:
    key = jax.random.key(seed)
    ks = iter(jax.random.split(key, 48))

    def nrm(shape, scale):
        return scale * jax.random.normal(next(ks), shape, F32)

    L = DEPTH
    dt0 = jnp.exp(jax.random.uniform(next(ks), (L, M_HEADS), F32, math.log(M_DT_MIN), math.log(M_DT_MAX)))
    dt_bias = dt0 + jnp.log(-jnp.expm1(-dt0))
    a_log = jnp.log(jax.random.uniform(next(ks), (L, M_HEADS), F32, 1.0, 16.0))
    return {
        'x_prompt': nrm((BATCH, SEQ, D_MODEL), 1.0),
        'x_sample': nrm((DEC_BATCH, DEC_SEQ, D_MODEL), 1.0),
        'state_conv': nrm((L, DEC_BATCH, M_CONV - 1, M_CONV_DIM), 1.0),
        'state_ssm': nrm((L, DEC_BATCH, M_HEADS, M_HEAD_DIM, M_STATE), 0.3),
        'state_hgrn': nrm((L, DEC_BATCH, H_HEADS, H_KEY, H_VAL), 0.3),
        'ffn1_w_gate': nrm((L, D_MODEL, D_FF), D_MODEL ** -0.5),
        'ffn1_w_up': nrm((L, D_MODEL, D_FF), D_MODEL ** -0.5),
        'ffn1_w_down': nrm((L, D_FF, D_MODEL), BETA * D_FF ** -0.5),
        'ln1_g': 1.0 + nrm((L, D_MODEL), 0.02),
        'ln1_b': nrm((L, D_MODEL), 0.01),
        'w_in': nrm((L, D_MODEL, IN_COLS), D_MODEL ** -0.5),
        'conv_w': nrm((L, M_CONV, M_CONV_DIM), M_CONV ** -0.5),
        'conv_b': nrm((L, M_CONV_DIM), 0.01),
        'dt_bias': dt_bias,
        'a_log': a_log,
        'd_skip': 1.0 + nrm((L, M_HEADS), 0.1),
        'm_norm_w': 1.0 + nrm((L, M_INNER), 0.02),
        'w_m_out': nrm((L, M_INNER, D_MODEL), BETA * M_INNER ** -0.5),
        'hgrn_lb_param': nrm((L + 1, H_QK), 0.1),
        'h_norm_w': 1.0 + nrm((L, H_INNER), 0.02),
        'w_h_out': nrm((L, H_INNER, D_MODEL), BETA * H_INNER ** -0.5),
        'w_o': nrm((L, D_MODEL, D_MODEL), BETA * D_MODEL ** -0.5),
        'ln2_g': 1.0 + nrm((L, D_MODEL), 0.02),
        'ln2_b': nrm((L, D_MODEL), 0.01),
        'ffn2_w_gate': nrm((L, D_MODEL, D_FF), D_MODEL ** -0.5),
        'ffn2_w_up': nrm((L, D_MODEL, D_FF), D_MODEL ** -0.5),
        'ffn2_w_down': nrm((L, D_FF, D_MODEL), BETA * D_FF ** -0.5),
        'ln3_g': 1.0 + nrm((L, D_MODEL), 0.02),
        'ln3_b': nrm((L, D_MODEL), 0.01),
    }


def reference(x_prompt, x_sample, state_conv, state_ssm, state_hgrn,
              ffn1_w_gate, ffn1_w_up, ffn1_w_down, ln1_g, ln1_b, w_in, conv_w, conv_b, dt_bias, a_log,
              d_skip, m_norm_w, w_m_out, hgrn_lb_param, h_norm_w, w_h_out, w_o, ln2_g, ln2_b,
              ffn2_w_gate, ffn2_w_up, ffn2_w_down, ln3_g, ln3_b):
    lb_all = jnp.cumsum(jax.nn.softmax(hgrn_lb_param.astype(F32), axis=0), axis=0)
    bp = x_prompt.shape[0]
    xp, xs = x_prompt, x_sample
    conv_p, ssm_p, hg_p, conv_s, ssm_s, hg_s = [], [], [], [], [], []
    for l in range(DEPTH):
        w = (ffn1_w_gate[l], ffn1_w_up[l], ffn1_w_down[l], ln1_g[l], ln1_b[l], w_in[l], conv_w[l], conv_b[l],
             dt_bias[l], a_log[l], d_skip[l], m_norm_w[l], w_m_out[l], h_norm_w[l], w_h_out[l], w_o[l],
             ln2_g[l], ln2_b[l], ffn2_w_gate[l], ffn2_w_up[l], ffn2_w_down[l], ln3_g[l], ln3_b[l])
        lb = lb_all[l]
        zc = jnp.zeros((bp, M_CONV - 1, M_CONV_DIM), state_conv.dtype)
        zh = jnp.zeros((bp, M_HEADS, M_HEAD_DIM, M_STATE), state_ssm.dtype)
        zs = jnp.zeros((bp, H_HEADS, H_KEY, H_VAL), state_hgrn.dtype)
        xp, cp, sp, hp = _layer(xp, zc, zh, zs, lb, w)
        xs, cs, ss, hs = _layer(xs, state_conv[l], state_ssm[l], state_hgrn[l], lb, w)
        conv_p.append(cp); ssm_p.append(sp); hg_p.append(hp)
        conv_s.append(cs); ssm_s.append(ss); hg_s.append(hs)
    return (xp, xs, jnp.stack(conv_p), jnp.stack(ssm_p), jnp.stack(hg_p),
            jnp.stack(conv_s), jnp.stack(ssm_s), jnp.stack(hg_s))
```

```python
import functools

import jax
import jax.numpy as jnp
from jax import lax
from jax.experimental import pallas as pl
from jax.experimental.pallas import tpu as pltpu

F32 = jnp.float32
BF16 = jnp.bfloat16

D_MODEL = 1024
D_FF = 2816
M_HEADS = 16
M_HEAD_DIM = 64
M_GROUPS = 4
M_STATE = 128
M_INNER = M_HEADS * M_HEAD_DIM
M_CONV = 4
M_CONV_DIM = M_INNER + 2 * M_GROUPS * M_STATE
HEADS_PER_GROUP = M_HEADS // M_GROUPS
GROUP_COLS = HEADS_PER_GROUP * M_HEAD_DIM
H_HEADS = 8
H_KEY = 128
H_VAL = 128
H_CHUNK = 32
ALPHA = 2.0 ** 0.25
EPS = 1e-5

LANES = 128
SUBLANES = 8
VMEM_LIMIT = 56 * 1024 * 1024

FF_CHUNK = D_FF // 2
PROJ_MAIN = 9 * D_MODEL
PROJ_BLOCKS = 4
PROJ_BLOCK_COLS = PROJ_MAIN // PROJ_BLOCKS
COL_XBC, COL_Z, COL_Q, COL_F, COL_I, COL_OG, COL_GM, COL_GH = 0, 2, 3, 4, 5, 6, 7, 8

SSD_CHUNK = 128
SSD_TL = 256
HGRN_TL = 128
SAMPLE_ROWS = 8
SAMPLE_FIRST = M_CONV - 1


def _sigmoid(x):
    return 1.0 / (1.0 + jnp.exp(-x))


def _silu(x):
    return x * _sigmoid(x)


def _softplus(x):
    return jnp.maximum(x, 0.0) + jnp.log(1.0 + jnp.exp(-jnp.abs(x)))


def _dot(a, b):
    return jnp.dot(a, b, preferred_element_type=F32)


def _dot_nt(a, b):
    return lax.dot_general(a, b, (((1,), (1,)), ((), ())), preferred_element_type=F32)


def _exact_dot(sel_bf16, x):
    hi = x.astype(BF16)
    r1 = x - hi.astype(F32)
    mid = r1.astype(BF16)
    lo = (r1 - mid.astype(F32)).astype(BF16)
    return (_dot(sel_bf16, hi) + _dot(sel_bf16, mid)) + _dot(sel_bf16, lo)


def _pad_rows(a):
    q = a.shape[0]
    if q == LANES:
        return a
    return jnp.concatenate([a, jnp.zeros((LANES - q, a.shape[1]), a.dtype)], axis=0)


def _layer_norm(y, g, b):
    mu = jnp.mean(y, axis=-1, keepdims=True)
    yc = y - mu
    var = jnp.mean(yc * yc, axis=-1, keepdims=True)
    return yc * lax.rsqrt(var + EPS) * g + b


def _swiglu(x, wg_ref, wu_ref, wd_ref):
    xb = x.astype(BF16)
    acc = None
    for c in range(D_FF // FF_CHUNK):
        sl = slice(c * FF_CHUNK, (c + 1) * FF_CHUNK)
        hg = _dot(xb, wg_ref[:, sl])
        hu = _dot(xb, wu_ref[:, sl])
        act = (_silu(hg) * hu).astype(BF16)
        part = _dot(act, wd_ref[sl, :])
        acc = part if acc is None else acc + part
    return acc


def _ffn_ln_kernel(x_ref, wg_ref, wu_ref, wd_ref, g_ref, b_ref, o_ref):
    x = x_ref[...]
    y = ALPHA * x + 0.5 * _swiglu(x, wg_ref, wu_ref, wd_ref)
    o_ref[...] = _layer_norm(y, g_ref[...], b_ref[...])


def _in_proj_kernel(x_ref, w_ref, wdt_ref, o_ref, odt_ref):
    j = pl.program_id(1)
    xb = x_ref[...].astype(BF16)
    o_ref[...] = _dot(xb, w_ref[j])

    @pl.when(j == 0)
    def _():
        odt_ref[...] = _dot(xb, wdt_ref[...])


def _merge_ffn_kernel(x1_ref, ym_ref, yh_ref, gm_ref, gh_ref, wo_ref, g2_ref, b2_ref,
                      wg_ref, wu_ref, wd_ref, g3_ref, b3_ref, o_ref):
    mix = _sigmoid(gm_ref[...]) * ym_ref[...] + _sigmoid(gh_ref[...]) * yh_ref[...]
    x2 = _layer_norm(ALPHA * x1_ref[...] + _dot(mix.astype(BF16), wo_ref[...]),
                     g2_ref[...], b2_ref[...])
    y = ALPHA * x2 + 0.5 * _swiglu(x2, wg_ref, wu_ref, wd_ref)
    o_ref[...] = _layer_norm(y, g3_ref[...], b3_ref[...])


def _ssd_chunk(xm, bm, cm, dt, a_row, h_prev):
    q = xm.shape[0]
    row_i = lax.broadcasted_iota(jnp.int32, (q, LANES), 0)
    col_i = lax.broadcasted_iota(jnp.int32, (q, LANES), 1)
    causal = col_i <= row_i
    tri = jnp.where(causal, 1.0, 0.0).astype(BF16)
    a_cum = _exact_dot(tri, _pad_rows(dt * a_row))
    a_last = a_cum[q - 1:q, :]
    e_last = jnp.exp(a_last)
    e_cum = jnp.exp(a_cum)
    to_end = jnp.exp(a_last - a_cum) * dt
    a_cum_t = jnp.transpose(_pad_rows(a_cum))
    dt_t = jnp.transpose(_pad_rows(dt))
    to_end_t = jnp.transpose(_pad_rows(to_end))
    head_blk = jnp.right_shift(lax.broadcasted_iota(jnp.int32, (q, GROUP_COLS), 1), 6)
    ys, hs = [], []
    for g in range(M_GROUPS):
        bg = _pad_rows(bm[:, g * M_STATE:(g + 1) * M_STATE]).astype(BF16)
        cg = cm[:, g * M_STATE:(g + 1) * M_STATE].astype(BF16)
        xg = _pad_rows(xm[:, g * GROUP_COLS:(g + 1) * GROUP_COLS])
        xg_b = xg.astype(BF16)
        xg_t = jnp.transpose(xg)
        hg = h_prev[g * GROUP_COLS:(g + 1) * GROUP_COLS, :]
        cb = _dot_nt(cg, bg)
        y_inter = _dot_nt(cg, hg.astype(BF16))
        acc = jnp.zeros((q, GROUP_COLS), F32)
        for r in range(HEADS_PER_GROUP):
            h = g * HEADS_PER_GROUP + r
            seg = a_cum[:, h:h + 1] - a_cum_t[h:h + 1, :]
            decay = jnp.where(causal, jnp.exp(jnp.where(causal, seg, 0.0)), 0.0)
            w = (cb * decay * dt_t[h:h + 1, :]).astype(BF16)
            part = _dot(w, xg_b) + y_inter * e_cum[:, h:h + 1]
            acc = jnp.where(head_blk == r, part, acc)
            lhs = (xg_t[r * M_HEAD_DIM:(r + 1) * M_HEAD_DIM, :] * to_end_t[h:h + 1, :]).astype(BF16)
            st = _dot(lhs, bg)
            hs.append(e_last[:, h:h + 1] * hg[r * M_HEAD_DIM:(r + 1) * M_HEAD_DIM, :] + st)
        ys.append(acc)
    return jnp.concatenate(ys, axis=1), hs


def _ssd_epilogue(y_ssd, xm, z, dskip, mnw, wmo_ref):
    g = (y_ssd + dskip * xm) * _silu(z)
    outs = []
    for k in range(M_GROUPS):
        gk = g[:, k * GROUP_COLS:(k + 1) * GROUP_COLS]
        outs.append(gk * lax.rsqrt(jnp.mean(gk * gk, axis=-1, keepdims=True) + EPS))
    gn = jnp.concatenate(outs, axis=1) * mnw
    return _dot(gn.astype(BF16), wmo_ref[...])


def _ssd_prompt_kernel(xbc_ref, z_ref, dt_ref, cw_ref, cb_ref, dtb_ref, alog_ref, dskip_ref,
                       mnw_ref, wmo_ref, ym_ref, conv_out_ref, ssm_out_ref, prev_scr, h_scr):
    t = pl.program_id(1)
    tl = xbc_ref.shape[1]

    @pl.when(t == 0)
    def _():
        prev_scr[...] = jnp.zeros_like(prev_scr)
        h_scr[...] = jnp.zeros_like(h_scr)

    u = xbc_ref[0]
    ext = jnp.concatenate([prev_scr[...], u], axis=0)
    conv = cb_ref[...] + cw_ref[M_CONV - 1:M_CONV, :] * u
    for j in range(1, M_CONV):
        shifted = pltpu.roll(ext, j, 0)[SUBLANES:SUBLANES + tl, :]
        conv = conv + cw_ref[M_CONV - 1 - j:M_CONV - j, :] * shifted
    prev_scr[...] = u[tl - SUBLANES:tl, :]
    xbc = _silu(conv)
    xm = xbc[:, :M_INNER]
    bm = xbc[:, M_INNER:M_INNER + M_GROUPS * M_STATE]
    cm = xbc[:, M_INNER + M_GROUPS * M_STATE:]
    dt = _softplus(dt_ref[0] + dtb_ref[...])
    a_row = -jnp.exp(alog_ref[...])
    ys = []
    for c in range(tl // SSD_CHUNK):
        rs = slice(c * SSD_CHUNK, (c + 1) * SSD_CHUNK)
        y_c, hs = _ssd_chunk(xm[rs], bm[rs], cm[rs], dt[rs], a_row, h_scr[...])
        for h in range(M_HEADS):
            h_scr[h * M_HEAD_DIM:(h + 1) * M_HEAD_DIM, :] = hs[h]
        ys.append(y_c)
    y_ssd = jnp.concatenate(ys, axis=0)
    ym_ref[0] = _ssd_epilogue(y_ssd, xm, z_ref[0], dskip_ref[...], mnw_ref[...], wmo_ref)

    @pl.when(t == pl.num_programs(1) - 1)
    def _():
        conv_out_ref[0] = u[tl - SUBLANES:tl, :]
        ssm_out_ref[0] = h_scr[...]


def _sample_valid_rows(n_cols):
    row = lax.broadcasted_iota(jnp.int32, (SAMPLE_ROWS, n_cols), 0)
    return (row >= SAMPLE_FIRST) & (row < SAMPLE_ROWS - 1)


def _ssd_sample_kernel(xbc_ref, z_ref, dt_ref, cs_ref, h0_ref, cw_ref, cb_ref, dtb_ref, alog_ref,
                       dskip_ref, mnw_ref, wmo_ref, ym_ref, u_out_ref, ssm_out_ref):
    row = lax.broadcasted_iota(jnp.int32, (SAMPLE_ROWS, M_CONV_DIM), 0)
    u = jnp.where(row < SAMPLE_FIRST, cs_ref[0], xbc_ref[0])
    conv = cb_ref[...] + cw_ref[M_CONV - 1:M_CONV, :] * u
    for j in range(1, M_CONV):
        conv = conv + cw_ref[M_CONV - 1 - j:M_CONV - j, :] * pltpu.roll(u, j, 0)
    xbc = _silu(conv)
    xm = xbc[:, :M_INNER]
    bm = xbc[:, M_INNER:M_INNER + M_GROUPS * M_STATE]
    cm = xbc[:, M_INNER + M_GROUPS * M_STATE:]
    dt = jnp.where(_sample_valid_rows(LANES), _softplus(dt_ref[0] + dtb_ref[...]), 0.0)
    a_row = -jnp.exp(alog_ref[...])
    y_ssd, hs = _ssd_chunk(xm, bm, cm, dt, a_row, h0_ref[0])
    for h in range(M_HEADS):
        ssm_out_ref[0, h * M_HEAD_DIM:(h + 1) * M_HEAD_DIM, :] = hs[h]
    u_out_ref[0] = u
    ym_ref[0] = _ssd_epilogue(y_ssd, xm, z_ref[0], dskip_ref[...], mnw_ref[...], wmo_ref)


def _hgrn_gates(f_raw, lbp_ref):
    p0 = lbp_ref[0:1, :]
    p1 = lbp_ref[1:2, :]
    m = jnp.maximum(p0, p1)
    e0 = jnp.exp(p0 - m)
    e1 = jnp.exp(p1 - m)
    lb = e0 / (e0 + e1)
    log_f = jnp.log(lb + (1.0 - lb) * _sigmoid(f_raw))
    k = (1.0 - lb) * _sigmoid(-f_raw)
    return log_f, k


def _hgrn_block(q, log_f, k, v, s_prev, ch):
    rows = q.shape[0]
    shift = ch.bit_length() - 1
    row_i = lax.broadcasted_iota(jnp.int32, (rows, LANES), 0)
    col_i = lax.broadcasted_iota(jnp.int32, (rows, LANES), 1)
    causal = (jnp.right_shift(row_i, shift) == jnp.right_shift(col_i, shift)) & (col_i <= row_i)
    tri = jnp.where(causal, 1.0, 0.0).astype(BF16)
    b_cum = _exact_dot(tri, _pad_rows(log_f))
    qd = q * jnp.exp(b_cum)
    kd = k * jnp.exp(-b_cum)
    col_t = lax.broadcasted_iota(jnp.int32, (LANES, LANES), 1)
    chunk_t = jnp.right_shift(col_t, shift)
    n_chunks = rows // ch
    outs, s_new = [], []
    for h in range(H_HEADS):
        hs = slice(h * H_KEY, (h + 1) * H_KEY)
        qh = qd[:, hs].astype(BF16)
        vh = _pad_rows(v[:, hs]).astype(BF16)
        sc = jnp.where(causal, _dot_nt(qh, _pad_rows(kd[:, hs]).astype(BF16)), 0.0)
        y_intra = _dot(sc.astype(BF16), vh)
        b_t = jnp.transpose(_pad_rows(b_cum[:, hs]))
        k_t = jnp.transpose(_pad_rows(k[:, hs]))
        last_cols = [b_t[:, c * ch + ch - 1:c * ch + ch] for c in range(n_chunks)]
        b_last = last_cols[n_chunks - 1]
        for c in range(n_chunks - 2, -1, -1):
            b_last = jnp.where(chunk_t == c, last_cols[c], b_last)
        k_end_t = k_t * jnp.exp(b_last - b_t)
        s = s_prev[h]
        pieces = []
        for c in range(n_chunks):
            rs = slice(c * ch, (c + 1) * ch)
            pieces.append(y_intra[rs] + _dot(qh[rs], s.astype(BF16)))
            ke = jnp.where(chunk_t == c, k_end_t, 0.0).astype(BF16)
            s = jnp.exp(last_cols[c]) * s + _dot(ke, vh)
        s_new.append(s)
        o_h = pieces[0] if n_chunks == 1 else jnp.concatenate(pieces, axis=0)
        outs.append(o_h * lax.rsqrt(jnp.mean(o_h * o_h, axis=-1, keepdims=True) + EPS))
    return jnp.concatenate(outs, axis=1), s_new


def _hgrn_prompt_kernel(q_ref, f_ref, i_ref, og_ref, lbp_ref, hnw_ref, who_ref,
                        yh_ref, s_out_ref, s_scr):
    t = pl.program_id(1)

    @pl.when(t == 0)
    def _():
        s_scr[...] = jnp.zeros_like(s_scr)

    log_f, k = _hgrn_gates(f_ref[0], lbp_ref)
    o, s_new = _hgrn_block(q_ref[0], log_f, k, i_ref[0], [s_scr[h] for h in range(H_HEADS)], H_CHUNK)
    for h in range(H_HEADS):
        s_scr[h] = s_new[h]
    o = o * hnw_ref[...] * _silu(og_ref[0])
    yh_ref[0] = _dot(o.astype(BF16), who_ref[...])

    @pl.when(t == pl.num_programs(1) - 1)
    def _():
        s_out_ref[0] = s_scr[...]


def _hgrn_sample_kernel(q_ref, f_ref, i_ref, og_ref, s0_ref, lbp_ref, hnw_ref, who_ref,
                        yh_ref, s_out_ref):
    valid = _sample_valid_rows(H_HEADS * H_KEY)
    log_f, k = _hgrn_gates(f_ref[0], lbp_ref)
    log_f = jnp.where(valid, log_f, 0.0)
    k = jnp.where(valid, k, 0.0)
    o, s_new = _hgrn_block(q_ref[0], log_f, k, i_ref[0],
                           [s0_ref[0, h] for h in range(H_HEADS)], SAMPLE_ROWS)
    for h in range(H_HEADS):
        s_out_ref[0, h] = s_new[h]
    o = o * hnw_ref[...] * _silu(og_ref[0])
    yh_ref[0] = _dot(o.astype(BF16), who_ref[...])


def _resident(shape):
    nd = len(shape)
    return pl.BlockSpec(shape, lambda *_: (0,) * nd, pipeline_mode=pl.Buffered(1))


def _params(semantics):
    return pltpu.CompilerParams(dimension_semantics=semantics, vmem_limit_bytes=VMEM_LIMIT)


def _row_tile(n_rows, want):
    tm = min(want, n_rows)
    assert n_rows % tm == 0
    return tm


def _ffn_ln(x, wg, wu, wd, g, b):
    n = x.shape[0]
    tm = _row_tile(n, 512)
    row = pl.BlockSpec((tm, D_MODEL), lambda i: (i, 0))
    return pl.pallas_call(
        _ffn_ln_kernel,
        grid=(n // tm,),
        in_specs=[row, _resident(wg.shape), _resident(wu.shape), _resident(wd.shape),
                  _resident(g.shape), _resident(b.shape)],
        out_specs=row,
        out_shape=jax.ShapeDtypeStruct((n, D_MODEL), F32),
        compiler_params=_params(("parallel",)),
        name="ffn_ln",
    )(x, wg, wu, wd, g, b)


def _in_proj(x1, w_main, w_dt):
    n = x1.shape[0]
    tm = _row_tile(n, 512)
    return pl.pallas_call(
        _in_proj_kernel,
        grid=(n // tm, PROJ_BLOCKS),
        in_specs=[pl.BlockSpec((tm, D_MODEL), lambda i, j: (i, 0)),
                  _resident(w_main.shape), _resident(w_dt.shape)],
        out_specs=[pl.BlockSpec((tm, PROJ_BLOCK_COLS), lambda i, j: (i, j)),
                   pl.BlockSpec((tm, LANES), lambda i, j: (i, 0))],
        out_shape=[jax.ShapeDtypeStruct((n, PROJ_MAIN), F32),
                   jax.ShapeDtypeStruct((n, LANES), F32)],
        compiler_params=_params(("parallel", "arbitrary")),
        name="in_proj",
    )(x1, w_main, w_dt)


def _merge_ffn(x1, ym, yh, proj, wo, g2, b2, wg, wu, wd, g3, b3):
    n = x1.shape[0]
    tm = _row_tile(n, 256)
    row = pl.BlockSpec((tm, D_MODEL), lambda i: (i, 0))
    col = lambda c: pl.BlockSpec((tm, D_MODEL), lambda i: (i, c))
    return pl.pallas_call(
        _merge_ffn_kernel,
        grid=(n // tm,),
        in_specs=[row, row, row, col(COL_GM), col(COL_GH), _resident(wo.shape),
                  _resident(g2.shape), _resident(b2.shape), _resident(wg.shape),
                  _resident(wu.shape), _resident(wd.shape), _resident(g3.shape),
                  _resident(b3.shape)],
        out_specs=row,
        out_shape=jax.ShapeDtypeStruct((n, D_MODEL), F32),
        compiler_params=_params(("parallel",)),
        name="merge_ffn",
    )(x1, ym, yh, proj, proj, wo, g2, b2, wg, wu, wd, g3, b3)


def _ssd_prompt(proj3, dt3, ssd_w):
    bsz, length, _ = proj3.shape
    tl = SSD_TL
    col = lambda width, c: pl.BlockSpec((1, tl, width), lambda b, t: (b, t, c))
    return pl.pallas_call(
        _ssd_prompt_kernel,
        grid=(bsz, length // tl),
        in_specs=[col(M_CONV_DIM, COL_XBC), col(D_MODEL, COL_Z), col(LANES, 0)]
                 + [_resident(w.shape) for w in ssd_w],
        out_specs=[pl.BlockSpec((1, tl, D_MODEL), lambda b, t: (b, t, 0)),
                   pl.BlockSpec((1, SUBLANES, M_CONV_DIM), lambda b, t: (b, 0, 0)),
                   pl.BlockSpec((1, M_INNER, M_STATE), lambda b, t: (b, 0, 0))],
        out_shape=[jax.ShapeDtypeStruct((bsz, length, D_MODEL), F32),
                   jax.ShapeDtypeStruct((bsz, SUBLANES, M_CONV_DIM), F32),
                   jax.ShapeDtypeStruct((bsz, M_INNER, M_STATE), F32)],
        scratch_shapes=[pltpu.VMEM((SUBLANES, M_CONV_DIM), F32),
                        pltpu.VMEM((M_INNER, M_STATE), F32)],
        compiler_params=_params(("parallel", "arbitrary")),
        name="ssd_prompt",
    )(proj3, proj3, dt3, *ssd_w)


def _ssd_sample(proj3, dt3, conv8, ssm0, ssd_w):
    bsz = proj3.shape[0]
    col = lambda width, c: pl.BlockSpec((1, SAMPLE_ROWS, width), lambda b: (b, 0, c))
    state = pl.BlockSpec((1, M_INNER, M_STATE), lambda b: (b, 0, 0))
    return pl.pallas_call(
        _ssd_sample_kernel,
        grid=(bsz,),
        in_specs=[col(M_CONV_DIM, COL_XBC), col(D_MODEL, COL_Z), col(LANES, 0),
                  col(M_CONV_DIM, 0), state] + [_resident(w.shape) for w in ssd_w],
        out_specs=[col(D_MODEL, 0), col(M_CONV_DIM, 0), state],
        out_shape=[jax.ShapeDtypeStruct((bsz, SAMPLE_ROWS, D_MODEL), F32),
                   jax.ShapeDtypeStruct((bsz, SAMPLE_ROWS, M_CONV_DIM), F32),
                   jax.ShapeDtypeStruct((bsz, M_INNER, M_STATE), F32)],
        compiler_params=_params(("parallel",)),
        name="ssd_sample",
    )(proj3, proj3, dt3, conv8, ssm0, *ssd_w)


def _hgrn_prompt(proj3, hgrn_w):
    bsz, length, _ = proj3.shape
    tl = HGRN_TL
    col = lambda c: pl.BlockSpec((1, tl, D_MODEL), lambda b, t: (b, t, c))
    return pl.pallas_call(
        _hgrn_prompt_kernel,
        grid=(bsz, length // tl),
        in_specs=[col(COL_Q), col(COL_F), col(COL_I), col(COL_OG)]
                 + [_resident(w.shape) for w in hgrn_w],
        out_specs=[pl.BlockSpec((1, tl, D_MODEL), lambda b, t: (b, t, 0)),
                   pl.BlockSpec((1, H_HEADS, H_KEY, H_VAL), lambda b, t: (b, 0, 0, 0))],
        out_shape=[jax.ShapeDtypeStruct((bsz, length, D_MODEL), F32),
                   jax.ShapeDtypeStruct((bsz, H_HEADS, H_KEY, H_VAL), F32)],
        scratch_shapes=[pltpu.VMEM((H_HEADS, H_KEY, H_VAL), F32)],
        compiler_params=_params(("parallel", "arbitrary")),
        name="hgrn_prompt",
    )(proj3, proj3, proj3, proj3, *hgrn_w)


def _hgrn_sample(proj3, s0, hgrn_w):
    bsz = proj3.shape[0]
    col = lambda c: pl.BlockSpec((1, SAMPLE_ROWS, D_MODEL), lambda b: (b, 0, c))
    state = pl.BlockSpec((1, H_HEADS, H_KEY, H_VAL), lambda b: (b, 0, 0, 0))
    return pl.pallas_call(
        _hgrn_sample_kernel,
        grid=(bsz,),
        in_specs=[col(COL_Q), col(COL_F), col(COL_I), col(COL_OG), state]
                 + [_resident(w.shape) for w in hgrn_w],
        out_specs=[col(0), state],
        out_shape=[jax.ShapeDtypeStruct((bsz, SAMPLE_ROWS, D_MODEL), F32),
                   jax.ShapeDtypeStruct((bsz, H_HEADS, H_KEY, H_VAL), F32)],
        compiler_params=_params(("parallel",)),
        name="hgrn_sample",
    )(proj3, proj3, proj3, proj3, s0, *hgrn_w)


def _pad_lanes(v):
    return jnp.pad(v, (0, LANES - v.shape[0])).reshape(1, LANES)


def kernel(x_prompt, x_sample, state_conv, state_ssm, state_hgrn, ffn1_w_gate, ffn1_w_up, ffn1_w_down, ln1_g, ln1_b, w_in, conv_w, conv_b, dt_bias, a_log, d_skip, m_norm_w, w_m_out, hgrn_lb_param, h_norm_w, w_h_out, w_o, ln2_g, ln2_b, ffn2_w_gate, ffn2_w_up, ffn2_w_down, ln3_g, ln3_b):
    assert w_in.shape[0] == 1, "single trunk layer"
    bp, lp, _ = x_prompt.shape
    bs, ls, _ = x_sample.shape
    assert ls == SAMPLE_ROWS - M_CONV

    row = lambda v: v[0].reshape(1, -1)
    f1 = (ffn1_w_gate[0].astype(BF16), ffn1_w_up[0].astype(BF16), ffn1_w_down[0].astype(BF16))
    f2 = (ffn2_w_gate[0].astype(BF16), ffn2_w_up[0].astype(BF16), ffn2_w_down[0].astype(BF16))
    wi = w_in[0]
    c_z, c_xbc, c_dt = M_INNER, M_INNER + M_CONV_DIM, M_INNER + M_CONV_DIM + M_HEADS
    w_main = jnp.concatenate([wi[:, c_z:c_xbc], wi[:, :c_z], wi[:, c_dt:]], axis=1).astype(BF16)
    w_main = w_main.reshape(D_MODEL, PROJ_BLOCKS, PROJ_BLOCK_COLS).transpose(1, 0, 2)
    w_dt = jnp.pad(wi[:, c_xbc:c_dt], ((0, 0), (0, LANES - M_HEADS))).astype(BF16)
    ssd_w = (conv_w[0], row(conv_b), _pad_lanes(dt_bias[0]), _pad_lanes(a_log[0]),
             jnp.repeat(d_skip[0], M_HEAD_DIM).reshape(1, M_INNER), row(m_norm_w),
             w_m_out[0].astype(BF16))
    hgrn_w = (hgrn_lb_param, row(h_norm_w), w_h_out[0].astype(BF16))
    merge_w = (w_o[0].astype(BF16), row(ln2_g), row(ln2_b)) + f2 + (row(ln3_g), row(ln3_b))

    def trunk_rows(x2d):
        x1 = _ffn_ln(x2d, *f1, row(ln1_g), row(ln1_b))
        proj, dt = _in_proj(x1, w_main, w_dt)
        return x1, proj, dt

    x1, proj, dt = trunk_rows(x_prompt.reshape(bp * lp, D_MODEL))
    proj3 = proj.reshape(bp, lp, PROJ_MAIN)
    ym, conv_p, ssm_p = _ssd_prompt(proj3, dt.reshape(bp, lp, LANES), ssd_w)
    yh, hg_p = _hgrn_prompt(proj3, hgrn_w)
    y_prompt = _merge_ffn(x1, ym.reshape(-1, D_MODEL), yh.reshape(-1, D_MODEL), proj, *merge_w)
    y_prompt = y_prompt.reshape(bp, lp, D_MODEL)
    new_conv_p = conv_p[:, SUBLANES - (M_CONV - 1):, :][None]
    new_ssm_p = ssm_p.reshape(1, bp, M_HEADS, M_HEAD_DIM, M_STATE)
    new_hg_p = hg_p[None]

    xs = jnp.pad(x_sample, ((0, 0), (SAMPLE_FIRST, 1), (0, 0)))
    x1s, projs, dts = trunk_rows(xs.reshape(bs * SAMPLE_ROWS, D_MODEL))
    projs3 = projs.reshape(bs, SAMPLE_ROWS, PROJ_MAIN)
    conv8 = jnp.pad(state_conv[0], ((0, 0), (0, SAMPLE_ROWS - (M_CONV - 1)), (0, 0)))
    yms, u_s, ssm_s = _ssd_sample(projs3, dts.reshape(bs, SAMPLE_ROWS, LANES), conv8,
                                  state_ssm[0].reshape(bs, M_INNER, M_STATE), ssd_w)
    yhs, hg_s = _hgrn_sample(projs3, state_hgrn[0], hgrn_w)
    y_s = _merge_ffn(x1s, yms.reshape(-1, D_MODEL), yhs.reshape(-1, D_MODEL), projs, *merge_w)
    y_sample = y_s.reshape(bs, SAMPLE_ROWS, D_MODEL)[:, SAMPLE_FIRST:SAMPLE_FIRST + ls, :]
    new_conv_s = u_s[:, ls:ls + M_CONV - 1, :][None]
    new_ssm_s = ssm_s.reshape(1, bs, M_HEADS, M_HEAD_DIM, M_STATE)
    new_hg_s = hg_s[None]

    return (y_prompt, y_sample, new_conv_p, new_ssm_p, new_hg_p, new_conv_s, new_ssm_s, new_hg_s)
```

```python
import jax
import jax.numpy as jnp
from jax import lax
from jax.experimental import pallas as pl
from jax.experimental.pallas import tpu as pltpu

F32 = jnp.float32
BF16 = jnp.bfloat16

D_MODEL = 1024
D_FF = 2816
M_HEADS = 16
M_HEAD_DIM = 64
M_GROUPS = 4
M_STATE = 128
M_INNER = M_HEADS * M_HEAD_DIM
M_CONV = 4
M_CONV_DIM = M_INNER + 2 * M_GROUPS * M_STATE
HEADS_PER_GROUP = M_HEADS // M_GROUPS
GROUP_COLS = HEADS_PER_GROUP * M_HEAD_DIM
H_HEADS = 8
H_KEY = 128
H_VAL = 128
H_CHUNK = 32
ALPHA = 2.0 ** 0.25
EPS = 1e-5

LANES = 128
SUBLANES = 8
BF16_SUBLANES = 16
VMEM_LIMIT = 56 * 1024 * 1024

FF_CHUNK = D_FF // 2
PROJ_MAIN = 9 * D_MODEL
PROJ_BLOCKS = 4
PROJ_BLOCK_COLS = PROJ_MAIN // PROJ_BLOCKS
COL_XBC, COL_Z, COL_Q, COL_F, COL_I, COL_OG, COL_GM, COL_GH = 0, 2, 3, 4, 5, 6, 7, 8

SSD_CHUNK = 128
SSD_TL = 256
HGRN_BLOCK = 128
HGRN_TL = 256
SAMPLE_ROWS = 8
SAMPLE_FIRST = M_CONV - 1
SAMPLE_BATCH_BLOCK = 4


def _sigmoid(x):
    return 1.0 / (1.0 + jnp.exp(-x))


def _silu(x):
    return x * _sigmoid(x)


def _softplus(x):
    return jnp.maximum(x, 0.0) + jnp.log(1.0 + jnp.exp(-jnp.abs(x)))


def _dot(a, b):
    return jnp.dot(a, b, preferred_element_type=F32)


def _dot_nt(a, b):
    return lax.dot_general(a, b, (((1,), (1,)), ((), ())), preferred_element_type=F32)


def _split3(x):
    hi = x.astype(BF16)
    r1 = x - hi.astype(F32)
    mid = r1.astype(BF16)
    lo = (r1 - mid.astype(F32)).astype(BF16)
    return hi, mid, lo


def _pad_rows(a):
    q = a.shape[0]
    if q == LANES:
        return a
    return jnp.concatenate([a, jnp.zeros((LANES - q, a.shape[1]), a.dtype)], axis=0)


def _chunk_cumsum(x, ch):
    rows, cols = x.shape
    if rows > ch and ch % BF16_SUBLANES == 0 and 3 * ch <= LANES:
        hi, mid, lo = _split3(x)
        row_i = lax.broadcasted_iota(jnp.int32, (ch, LANES), 0)
        col_i = lax.broadcasted_iota(jnp.int32, (ch, LANES), 1)
        sel = jnp.where((jnp.bitwise_and(col_i, ch - 1) <= row_i) & (col_i < 3 * ch), 1.0, 0.0)
        sel = sel.astype(BF16)
        pad = jnp.zeros((LANES - 3 * ch, cols), BF16)
        outs = []
        for c in range(rows // ch):
            rs = slice(c * ch, (c + 1) * ch)
            outs.append(_dot(sel, jnp.concatenate([hi[rs], mid[rs], lo[rs], pad], axis=0)))
        return jnp.concatenate(outs, axis=0)
    shift = ch.bit_length() - 1
    row_i = lax.broadcasted_iota(jnp.int32, (rows, LANES), 0)
    col_i = lax.broadcasted_iota(jnp.int32, (rows, LANES), 1)
    same = jnp.right_shift(row_i, shift) == jnp.right_shift(col_i, shift)
    sel = jnp.where(same & (col_i <= row_i), 1.0, 0.0).astype(BF16)
    hi, mid, lo = _split3(_pad_rows(x))
    return (_dot(sel, hi) + _dot(sel, mid)) + _dot(sel, lo)


def _layer_norm(y, g, b):
    mu = jnp.mean(y, axis=-1, keepdims=True)
    yc = y - mu
    var = jnp.mean(yc * yc, axis=-1, keepdims=True)
    return yc * lax.rsqrt(var + EPS) * g + b


def _swiglu(x, wg_ref, wu_ref, wd_ref):
    xb = x.astype(BF16)
    acc = None
    for c in range(D_FF // FF_CHUNK):
        sl = slice(c * FF_CHUNK, (c + 1) * FF_CHUNK)
        hg = _dot(xb, wg_ref[:, sl])
        hu = _dot(xb, wu_ref[:, sl])
        act = (_silu(hg) * hu).astype(BF16)
        part = _dot(act, wd_ref[sl, :])
        acc = part if acc is None else acc + part
    return acc


def _ffn_ln_kernel(x_ref, wg_ref, wu_ref, wd_ref, g_ref, b_ref, o_ref):
    x = x_ref[...]
    y = ALPHA * x + 0.5 * _swiglu(x, wg_ref, wu_ref, wd_ref)
    o_ref[...] = _layer_norm(y, g_ref[...], b_ref[...])


def _in_proj_kernel(x_ref, w_ref, wdt_ref, o_ref, odt_ref):
    j = pl.program_id(1)
    xb = x_ref[...].astype(BF16)
    o_ref[...] = _dot(xb, w_ref[j])

    @pl.when(j == 0)
    def _():
        odt_ref[...] = _dot(xb, wdt_ref[...])


def _merge_ffn_kernel(x1_ref, ym_ref, yh_ref, gm_ref, gh_ref, wo_ref, g2_ref, b2_ref,
                      wg_ref, wu_ref, wd_ref, g3_ref, b3_ref, o_ref):
    mix = _sigmoid(gm_ref[...]) * ym_ref[...] + _sigmoid(gh_ref[...]) * yh_ref[...]
    x2 = _layer_norm(ALPHA * x1_ref[...] + _dot(mix.astype(BF16), wo_ref[...]),
                     g2_ref[...], b2_ref[...])
    y = ALPHA * x2 + 0.5 * _swiglu(x2, wg_ref, wu_ref, wd_ref)
    o_ref[...] = _layer_norm(y, g3_ref[...], b3_ref[...])


def _ssd_streams(chunks, h_prev, a_row, stage_major):
    q = chunks[0][1].shape[0]
    n = len(chunks)
    row_i = lax.broadcasted_iota(jnp.int32, (q, LANES), 0)
    col_i = lax.broadcasted_iota(jnp.int32, (q, LANES), 1)
    causal = col_i <= row_i
    head_blk = jnp.right_shift(lax.broadcasted_iota(jnp.int32, (q, GROUP_COLS), 1),
                               M_HEAD_DIM.bit_length() - 1)
    groups = [slice(g * M_STATE, (g + 1) * M_STATE) for g in range(M_GROUPS)]
    rows_r = [slice(r * M_HEAD_DIM, (r + 1) * M_HEAD_DIM) for r in range(HEADS_PER_GROUP)]

    pre, ops = {}, {}

    def prepare(i):
        _, xm, bm, cm, dt = chunks[i]
        a_cum = _chunk_cumsum(dt * a_row, q)
        a_last = a_cum[q - 1:q, :]
        to_end = jnp.exp(a_last - a_cum) * dt
        pre[i] = dict(
            a_cum=a_cum, e_last=jnp.exp(a_last), e_cum=jnp.exp(a_cum),
            a_cum_t=jnp.transpose(_pad_rows(a_cum)),
            dt_t=jnp.transpose(_pad_rows(dt)),
            to_end_t=jnp.transpose(_pad_rows(to_end)))
        for g in range(M_GROUPS):
            xg = _pad_rows(xm[:, g * GROUP_COLS:(g + 1) * GROUP_COLS])
            ops[i, g] = dict(bg=_pad_rows(bm[:, groups[g]]).astype(BF16),
                             cg=cm[:, groups[g]].astype(BF16),
                             xg_b=xg.astype(BF16), xg_t=jnp.transpose(xg))

    def mm_cb(i, g):
        return _dot_nt(ops[i, g]["cg"], ops[i, g]["bg"])

    def mm_state(i, g, r):
        h = g * HEADS_PER_GROUP + r
        lhs = (ops[i, g]["xg_t"][rows_r[r], :] * pre[i]["to_end_t"][h:h + 1, :]).astype(BF16)
        return _dot(lhs, ops[i, g]["bg"])

    def mm_intra(i, g, r, cb):
        h = g * HEADS_PER_GROUP + r
        seg = pre[i]["a_cum"][:, h:h + 1] - pre[i]["a_cum_t"][h:h + 1, :]
        decay = jnp.where(causal, jnp.exp(jnp.where(causal, seg, 0.0)), 0.0)
        w = (cb * decay * pre[i]["dt_t"][h:h + 1, :]).astype(BF16)
        return _dot(w, ops[i, g]["xg_b"])

    def mm_inter(i, g, blocks):
        hg = jnp.concatenate(blocks[g * HEADS_PER_GROUP:(g + 1) * HEADS_PER_GROUP], axis=0)
        return _dot_nt(ops[i, g]["cg"], hg.astype(BF16))

    def combine(i, g, parts, y_inter):
        acc = jnp.zeros((q, GROUP_COLS), F32)
        for r in range(HEADS_PER_GROUP):
            h = g * HEADS_PER_GROUP + r
            acc = jnp.where(head_blk == r, parts[r] + y_inter * pre[i]["e_cum"][:, h:h + 1], acc)
        return acc

    cur = {seq: list(blocks) for seq, blocks in h_prev.items()}
    ys = []
    if not stage_major:
        for i, chunk in enumerate(chunks):
            prepare(i)
            enter = cur[chunk[0]]
            cols, new = [], []
            for g in range(M_GROUPS):
                cb = mm_cb(i, g)
                y_inter = mm_inter(i, g, enter)
                acc = jnp.zeros((q, GROUP_COLS), F32)
                for r in range(HEADS_PER_GROUP):
                    h = g * HEADS_PER_GROUP + r
                    y_h = mm_intra(i, g, r, cb) + y_inter * pre[i]["e_cum"][:, h:h + 1]
                    acc = jnp.where(head_blk == r, y_h, acc)
                    new.append(pre[i]["e_last"][:, h:h + 1] * enter[h] + mm_state(i, g, r))
                cols.append(acc)
            cur[chunk[0]] = new
            ys.append(jnp.concatenate(cols, axis=1))
        return ys, cur
    for i in range(n):
        prepare(i)
    cb = {(i, g): mm_cb(i, g) for i in range(n) for g in range(M_GROUPS)}
    st = {(i, g, r): mm_state(i, g, r)
          for i in range(n) for g in range(M_GROUPS) for r in range(HEADS_PER_GROUP)}
    part = {(i, g, r): mm_intra(i, g, r, cb[i, g])
            for i in range(n) for g in range(M_GROUPS) for r in range(HEADS_PER_GROUP)}
    enter = []
    for i, chunk in enumerate(chunks):
        seq = chunk[0]
        enter.append(list(cur[seq]))
        cur[seq] = [pre[i]["e_last"][:, h:h + 1] * cur[seq][h]
                    + st[i, h // HEADS_PER_GROUP, h % HEADS_PER_GROUP] for h in range(M_HEADS)]
    for i in range(n):
        cols = []
        for g in range(M_GROUPS):
            y_inter = mm_inter(i, g, enter[i])
            cols.append(combine(i, g, [part[i, g, r] for r in range(HEADS_PER_GROUP)], y_inter))
        ys.append(jnp.concatenate(cols, axis=1))
    return ys, cur


def _ssd_gated_norm(y_ssd, xm, z, dskip, mnw):
    g = (y_ssd + dskip * xm) * _silu(z)
    outs = []
    for k in range(M_GROUPS):
        gk = g[:, k * GROUP_COLS:(k + 1) * GROUP_COLS]
        outs.append(gk * lax.rsqrt(jnp.mean(gk * gk, axis=-1, keepdims=True) + EPS))
    return jnp.concatenate(outs, axis=1) * mnw


def _split_xbc(xbc):
    return (xbc[:, :M_INNER], xbc[:, M_INNER:M_INNER + M_GROUPS * M_STATE],
            xbc[:, M_INNER + M_GROUPS * M_STATE:])


def _state_blocks(h):
    return [h[k * M_HEAD_DIM:(k + 1) * M_HEAD_DIM, :] for k in range(M_HEADS)]


def _ssd_prompt_kernel(xbc_ref, z_ref, dt_ref, cw_ref, cb_ref, dtb_ref, alog_ref, dskip_ref,
                       mnw_ref, wmo_ref, ym_ref, conv_out_ref, ssm_out_ref, prev_scr, h_scr):
    t = pl.program_id(1)
    tl = xbc_ref.shape[1]

    @pl.when(t == 0)
    def _():
        prev_scr[...] = jnp.zeros_like(prev_scr)
        h_scr[...] = jnp.zeros_like(h_scr)

    u = xbc_ref[0]
    ext = jnp.concatenate([prev_scr[...], u], axis=0)
    conv = cb_ref[...] + cw_ref[M_CONV - 1:M_CONV, :] * u
    for j in range(1, M_CONV):
        shifted = pltpu.roll(ext, j, 0)[SUBLANES:SUBLANES + tl, :]
        conv = conv + cw_ref[M_CONV - 1 - j:M_CONV - j, :] * shifted
    prev_scr[...] = u[tl - SUBLANES:tl, :]
    xm, bm, cm = _split_xbc(_silu(conv))
    dt = _softplus(dt_ref[0] + dtb_ref[...])
    a_row = -jnp.exp(alog_ref[...])
    chunks = []
    for c in range(tl // SSD_CHUNK):
        rs = slice(c * SSD_CHUNK, (c + 1) * SSD_CHUNK)
        chunks.append((0, xm[rs], bm[rs], cm[rs], dt[rs]))
    ys, h_new = _ssd_streams(chunks, {0: _state_blocks(h_scr[...])}, a_row, stage_major=False)
    for h in range(M_HEADS):
        h_scr[h * M_HEAD_DIM:(h + 1) * M_HEAD_DIM, :] = h_new[0][h]
    gn = _ssd_gated_norm(jnp.concatenate(ys, axis=0), xm, z_ref[0], dskip_ref[...], mnw_ref[...])
    ym_ref[0] = _dot(gn.astype(BF16), wmo_ref[...])

    @pl.when(t == pl.num_programs(1) - 1)
    def _():
        conv_out_ref[0] = u[tl - SUBLANES:tl, :]
        ssm_out_ref[0] = h_scr[...]


def _sample_valid_rows(n_cols):
    row = lax.broadcasted_iota(jnp.int32, (SAMPLE_ROWS, n_cols), 0)
    return (row >= SAMPLE_FIRST) & (row < SAMPLE_ROWS - 1)


def _ssd_sample_kernel(xbc_ref, z_ref, dt_ref, cs_ref, h0_ref, cw_ref, cb_ref, dtb_ref, alog_ref,
                       dskip_ref, mnw_ref, wmo_ref, ym_ref, u_out_ref, ssm_out_ref):
    bb = xbc_ref.shape[0]
    row = lax.broadcasted_iota(jnp.int32, (SAMPLE_ROWS, M_CONV_DIM), 0)
    valid = _sample_valid_rows(LANES)
    a_row = -jnp.exp(alog_ref[...])
    chunks, h_prev = [], {}
    for i in range(bb):
        u = jnp.where(row < SAMPLE_FIRST, cs_ref[i], xbc_ref[i])
        u_out_ref[i] = u
        conv = cb_ref[...] + cw_ref[M_CONV - 1:M_CONV, :] * u
        for j in range(1, M_CONV):
            conv = conv + cw_ref[M_CONV - 1 - j:M_CONV - j, :] * pltpu.roll(u, j, 0)
        xm, bm, cm = _split_xbc(_silu(conv))
        dt = jnp.where(valid, _softplus(dt_ref[i] + dtb_ref[...]), 0.0)
        chunks.append((i, xm, bm, cm, dt))
        h_prev[i] = _state_blocks(h0_ref[i])
    ys, h_new = _ssd_streams(chunks, h_prev, a_row, stage_major=True)
    gns = []
    for i in range(bb):
        for h in range(M_HEADS):
            ssm_out_ref[i, h * M_HEAD_DIM:(h + 1) * M_HEAD_DIM, :] = h_new[i][h]
        gns.append(_ssd_gated_norm(ys[i], chunks[i][1], z_ref[i], dskip_ref[...], mnw_ref[...]))
    ym = _dot(jnp.concatenate(gns, axis=0).astype(BF16), wmo_ref[...])
    for i in range(bb):
        ym_ref[i] = ym[i * SAMPLE_ROWS:(i + 1) * SAMPLE_ROWS, :]


def _hgrn_gates(f_raw, lbp_ref):
    p0 = lbp_ref[0:1, :]
    p1 = lbp_ref[1:2, :]
    m = jnp.maximum(p0, p1)
    e0 = jnp.exp(p0 - m)
    e1 = jnp.exp(p1 - m)
    lb = e0 / (e0 + e1)
    log_f = jnp.log(lb + (1.0 - lb) * _sigmoid(f_raw))
    k = (1.0 - lb) * _sigmoid(-f_raw)
    return log_f, k


def _hgrn_streams(blocks, s_prev, ch):
    rows = blocks[0][1].shape[0]
    n = len(blocks)
    shift = ch.bit_length() - 1
    n_chunks = rows // ch
    row_i = lax.broadcasted_iota(jnp.int32, (rows, LANES), 0)
    col_i = lax.broadcasted_iota(jnp.int32, (rows, LANES), 1)
    causal = (jnp.right_shift(row_i, shift) == jnp.right_shift(col_i, shift)) & (col_i <= row_i)
    chunk_t = jnp.right_shift(lax.broadcasted_iota(jnp.int32, (LANES, LANES), 1), shift)
    heads = [slice(h * H_KEY, (h + 1) * H_KEY) for h in range(H_HEADS)]
    chunks = [slice(c * ch, (c + 1) * ch) for c in range(n_chunks)]
    items = [(i, h) for i in range(n) for h in range(H_HEADS)]

    pre = []
    for (_, q, log_f, k, v) in blocks:
        b_cum = _chunk_cumsum(log_f, ch)
        lasts = [b_cum[c * ch + ch - 1:c * ch + ch, :] for c in range(n_chunks)]
        b_last = jnp.concatenate([jnp.broadcast_to(l, (ch, l.shape[1])) for l in lasts], axis=0)
        pre.append(dict(qd=q * jnp.exp(b_cum), kd=k * jnp.exp(-b_cum),
                        k_end=k * jnp.exp(b_last - b_cum), lasts=lasts, v=v))
    qh = {(i, h): pre[i]["qd"][:, heads[h]].astype(BF16) for (i, h) in items}
    vh = {(i, h): _pad_rows(pre[i]["v"][:, heads[h]]) for (i, h) in items}
    sc = {(i, h): _dot_nt(qh[i, h], _pad_rows(pre[i]["kd"][:, heads[h]]).astype(BF16))
          for (i, h) in items}
    ds = {}
    for (i, h) in items:
        v_t = jnp.transpose(vh[i, h])
        ke = _pad_rows(pre[i]["k_end"][:, heads[h]]).astype(BF16)
        ds[i, h] = [_dot((v_t if n_chunks == 1 else jnp.where(chunk_t == c, v_t, 0.0)).astype(BF16), ke)
                    for c in range(n_chunks)]
    y_intra = {(i, h): _dot(jnp.where(causal, sc[i, h], 0.0).astype(BF16), vh[i, h].astype(BF16))
               for (i, h) in items}
    cur = {seq: list(states) for seq, states in s_prev.items()}
    enter = {}
    for i, blk in enumerate(blocks):
        seq = blk[0]
        for h in range(H_HEADS):
            s = cur[seq][h]
            per_chunk = []
            for c in range(n_chunks):
                per_chunk.append(s)
                s = jnp.exp(pre[i]["lasts"][c][:, heads[h]]) * s + ds[i, h][c]
            enter[i, h] = per_chunk
            cur[seq][h] = s
    y_inter = {(i, h): [_dot_nt(qh[i, h][chunks[c]], enter[i, h][c].astype(BF16))
                        for c in range(n_chunks)] for (i, h) in items}
    outs = []
    for i in range(n):
        cols = []
        for h in range(H_HEADS):
            pieces = [y_intra[i, h][chunks[c]] + y_inter[i, h][c] for c in range(n_chunks)]
            o_h = pieces[0] if n_chunks == 1 else jnp.concatenate(pieces, axis=0)
            cols.append(o_h * lax.rsqrt(jnp.mean(o_h * o_h, axis=-1, keepdims=True) + EPS))
        outs.append(jnp.concatenate(cols, axis=1))
    return outs, cur


def _hgrn_prompt_kernel(q_ref, f_ref, i_ref, og_ref, lbp_ref, hnw_ref, who_ref,
                        yh_ref, s_out_ref, s_scr):
    t = pl.program_id(1)
    tl = q_ref.shape[1]

    @pl.when(t == 0)
    def _():
        s_scr[...] = jnp.zeros_like(s_scr)

    log_f, k = _hgrn_gates(f_ref[0], lbp_ref)
    q = q_ref[0]
    v = i_ref[0]
    blocks = []
    for b in range(tl // HGRN_BLOCK):
        rs = slice(b * HGRN_BLOCK, (b + 1) * HGRN_BLOCK)
        blocks.append((0, q[rs], log_f[rs], k[rs], v[rs]))
    outs, s_new = _hgrn_streams(blocks, {0: [s_scr[h] for h in range(H_HEADS)]}, H_CHUNK)
    for h in range(H_HEADS):
        s_scr[h] = s_new[0][h]
    o = jnp.concatenate(outs, axis=0) * hnw_ref[...] * _silu(og_ref[0])
    yh_ref[0] = _dot(o.astype(BF16), who_ref[...])

    @pl.when(t == pl.num_programs(1) - 1)
    def _():
        for h in range(H_HEADS):
            s_out_ref[0, h] = jnp.transpose(s_scr[h])


def _hgrn_sample_kernel(q_ref, f_ref, i_ref, og_ref, s0_ref, lbp_ref, hnw_ref, who_ref,
                        yh_ref, s_out_ref):
    bb = q_ref.shape[0]
    valid = _sample_valid_rows(H_HEADS * H_KEY)
    blocks, s_prev = [], {}
    for i in range(bb):
        log_f, k = _hgrn_gates(f_ref[i], lbp_ref)
        blocks.append((i, q_ref[i], jnp.where(valid, log_f, 0.0), jnp.where(valid, k, 0.0), i_ref[i]))
        s_prev[i] = [jnp.transpose(s0_ref[i, h]) for h in range(H_HEADS)]
    outs, s_new = _hgrn_streams(blocks, s_prev, SAMPLE_ROWS)
    os_ = []
    for i in range(bb):
        for h in range(H_HEADS):
            s_out_ref[i, h] = jnp.transpose(s_new[i][h])
        os_.append(outs[i] * hnw_ref[...] * _silu(og_ref[i]))
    yh = _dot(jnp.concatenate(os_, axis=0).astype(BF16), who_ref[...])
    for i in range(bb):
        yh_ref[i] = yh[i * SAMPLE_ROWS:(i + 1) * SAMPLE_ROWS, :]


def _resident(shape):
    nd = len(shape)
    return pl.BlockSpec(shape, lambda *_: (0,) * nd, pipeline_mode=pl.Buffered(1))


def _params(semantics):
    return pltpu.CompilerParams(dimension_semantics=semantics, vmem_limit_bytes=VMEM_LIMIT)


def _row_tile(n_rows, want):
    tm = min(want, n_rows)
    assert n_rows % tm == 0
    return tm


def _ffn_ln(x, wg, wu, wd, g, b):
    n = x.shape[0]
    tm = _row_tile(n, 512)
    row = pl.BlockSpec((tm, D_MODEL), lambda i: (i, 0))
    return pl.pallas_call(
        _ffn_ln_kernel,
        grid=(n // tm,),
        in_specs=[row, _resident(wg.shape), _resident(wu.shape), _resident(wd.shape),
                  _resident(g.shape), _resident(b.shape)],
        out_specs=row,
        out_shape=jax.ShapeDtypeStruct((n, D_MODEL), F32),
        compiler_params=_params(("parallel",)),
        name="ffn_ln",
    )(x, wg, wu, wd, g, b)


def _in_proj(x1, w_main, w_dt):
    n = x1.shape[0]
    tm = _row_tile(n, 512)
    return pl.pallas_call(
        _in_proj_kernel,
        grid=(n // tm, PROJ_BLOCKS),
        in_specs=[pl.BlockSpec((tm, D_MODEL), lambda i, j: (i, 0)),
                  _resident(w_main.shape), _resident(w_dt.shape)],
        out_specs=[pl.BlockSpec((tm, PROJ_BLOCK_COLS), lambda i, j: (i, j)),
                   pl.BlockSpec((tm, LANES), lambda i, j: (i, 0))],
        out_shape=[jax.ShapeDtypeStruct((n, PROJ_MAIN), F32),
                   jax.ShapeDtypeStruct((n, LANES), F32)],
        compiler_params=_params(("parallel", "arbitrary")),
        name="in_proj",
    )(x1, w_main, w_dt)


def _merge_ffn(x1, ym, yh, proj, wo, g2, b2, wg, wu, wd, g3, b3):
    n = x1.shape[0]
    tm = _row_tile(n, 256)
    row = pl.BlockSpec((tm, D_MODEL), lambda i: (i, 0))
    col = lambda c: pl.BlockSpec((tm, D_MODEL), lambda i: (i, c))
    return pl.pallas_call(
        _merge_ffn_kernel,
        grid=(n // tm,),
        in_specs=[row, row, row, col(COL_GM), col(COL_GH), _resident(wo.shape),
                  _resident(g2.shape), _resident(b2.shape), _resident(wg.shape),
                  _resident(wu.shape), _resident(wd.shape), _resident(g3.shape),
                  _resident(b3.shape)],
        out_specs=row,
        out_shape=jax.ShapeDtypeStruct((n, D_MODEL), F32),
        compiler_params=_params(("parallel",)),
        name="merge_ffn",
    )(x1, ym, yh, proj, proj, wo, g2, b2, wg, wu, wd, g3, b3)


def _ssd_prompt(proj3, dt3, ssd_w):
    bsz, length, _ = proj3.shape
    tl = SSD_TL
    col = lambda width, c: pl.BlockSpec((1, tl, width), lambda b, t: (b, t, c))
    return pl.pallas_call(
        _ssd_prompt_kernel,
        grid=(bsz, length // tl),
        in_specs=[col(M_CONV_DIM, COL_XBC), col(D_MODEL, COL_Z), col(LANES, 0)]
                 + [_resident(w.shape) for w in ssd_w],
        out_specs=[pl.BlockSpec((1, tl, D_MODEL), lambda b, t: (b, t, 0)),
                   pl.BlockSpec((1, SUBLANES, M_CONV_DIM), lambda b, t: (b, 0, 0)),
                   pl.BlockSpec((1, M_INNER, M_STATE), lambda b, t: (b, 0, 0))],
        out_shape=[jax.ShapeDtypeStruct((bsz, length, D_MODEL), F32),
                   jax.ShapeDtypeStruct((bsz, SUBLANES, M_CONV_DIM), F32),
                   jax.ShapeDtypeStruct((bsz, M_INNER, M_STATE), F32)],
        scratch_shapes=[pltpu.VMEM((SUBLANES, M_CONV_DIM), F32),
                        pltpu.VMEM((M_INNER, M_STATE), F32)],
        compiler_params=_params(("parallel", "arbitrary")),
        name="ssd_prompt",
    )(proj3, proj3, dt3, *ssd_w)


def _ssd_sample(proj3, dt3, conv8, ssm0, ssd_w):
    bsz = proj3.shape[0]
    bb = SAMPLE_BATCH_BLOCK
    col = lambda width, c: pl.BlockSpec((bb, SAMPLE_ROWS, width), lambda b: (b, 0, c))
    state = pl.BlockSpec((bb, M_INNER, M_STATE), lambda b: (b, 0, 0))
    return pl.pallas_call(
        _ssd_sample_kernel,
        grid=(bsz // bb,),
        in_specs=[col(M_CONV_DIM, COL_XBC), col(D_MODEL, COL_Z), col(LANES, 0),
                  col(M_CONV_DIM, 0), state] + [_resident(w.shape) for w in ssd_w],
        out_specs=[col(D_MODEL, 0), col(M_CONV_DIM, 0), state],
        out_shape=[jax.ShapeDtypeStruct((bsz, SAMPLE_ROWS, D_MODEL), F32),
                   jax.ShapeDtypeStruct((bsz, SAMPLE_ROWS, M_CONV_DIM), F32),
                   jax.ShapeDtypeStruct((bsz, M_INNER, M_STATE), F32)],
        compiler_params=_params(("parallel",)),
        name="ssd_sample",
    )(proj3, proj3, dt3, conv8, ssm0, *ssd_w)


def _hgrn_prompt(proj3, hgrn_w):
    bsz, length, _ = proj3.shape
    tl = HGRN_TL
    col = lambda c: pl.BlockSpec((1, tl, D_MODEL), lambda b, t: (b, t, c))
    return pl.pallas_call(
        _hgrn_prompt_kernel,
        grid=(bsz, length // tl),
        in_specs=[col(COL_Q), col(COL_F), col(COL_I), col(COL_OG)]
                 + [_resident(w.shape) for w in hgrn_w],
        out_specs=[pl.BlockSpec((1, tl, D_MODEL), lambda b, t: (b, t, 0)),
                   pl.BlockSpec((1, H_HEADS, H_KEY, H_VAL), lambda b, t: (b, 0, 0, 0))],
        out_shape=[jax.ShapeDtypeStruct((bsz, length, D_MODEL), F32),
                   jax.ShapeDtypeStruct((bsz, H_HEADS, H_KEY, H_VAL), F32)],
        scratch_shapes=[pltpu.VMEM((H_HEADS, H_VAL, H_KEY), F32)],
        compiler_params=_params(("parallel", "arbitrary")),
        name="hgrn_prompt",
    )(proj3, proj3, proj3, proj3, *hgrn_w)


def _hgrn_sample(proj3, s0, hgrn_w):
    bsz = proj3.shape[0]
    bb = SAMPLE_BATCH_BLOCK
    col = lambda c: pl.BlockSpec((bb, SAMPLE_ROWS, D_MODEL), lambda b: (b, 0, c))
    state = pl.BlockSpec((bb, H_HEADS, H_KEY, H_VAL), lambda b: (b, 0, 0, 0))
    return pl.pallas_call(
        _hgrn_sample_kernel,
        grid=(bsz // bb,),
        in_specs=[col(COL_Q), col(COL_F), col(COL_I), col(COL_OG), state]
                 + [_resident(w.shape) for w in hgrn_w],
        out_specs=[col(0), state],
        out_shape=[jax.ShapeDtypeStruct((bsz, SAMPLE_ROWS, D_MODEL), F32),
                   jax.ShapeDtypeStruct((bsz, H_HEADS, H_KEY, H_VAL), F32)],
        compiler_params=_params(("parallel",)),
        name="hgrn_sample",
    )(proj3, proj3, proj3, proj3, s0, *hgrn_w)


def _pad_lanes(v):
    return jnp.pad(v, (0, LANES - v.shape[0])).reshape(1, LANES)


def kernel(x_prompt, x_sample, state_conv, state_ssm, state_hgrn, ffn1_w_gate, ffn1_w_up, ffn1_w_down, ln1_g, ln1_b, w_in, conv_w, conv_b, dt_bias, a_log, d_skip, m_norm_w, w_m_out, hgrn_lb_param, h_norm_w, w_h_out, w_o, ln2_g, ln2_b, ffn2_w_gate, ffn2_w_up, ffn2_w_down, ln3_g, ln3_b):
    assert w_in.shape[0] == 1, "single trunk layer"
    bp, lp, _ = x_prompt.shape
    bs, ls, _ = x_sample.shape
    assert ls == SAMPLE_ROWS - M_CONV

    row = lambda v: v[0].reshape(1, -1)
    f1 = (ffn1_w_gate[0].astype(BF16), ffn1_w_up[0].astype(BF16), ffn1_w_down[0].astype(BF16))
    f2 = (ffn2_w_gate[0].astype(BF16), ffn2_w_up[0].astype(BF16), ffn2_w_down[0].astype(BF16))
    wi = w_in[0]
    c_z, c_xbc, c_dt = M_INNER, M_INNER + M_CONV_DIM, M_INNER + M_CONV_DIM + M_HEADS
    w_main = jnp.concatenate([wi[:, c_z:c_xbc], wi[:, :c_z], wi[:, c_dt:]], axis=1).astype(BF16)
    w_main = w_main.reshape(D_MODEL, PROJ_BLOCKS, PROJ_BLOCK_COLS).transpose(1, 0, 2)
    w_dt = jnp.pad(wi[:, c_xbc:c_dt], ((0, 0), (0, LANES - M_HEADS))).astype(BF16)
    ssd_w = (conv_w[0], row(conv_b), _pad_lanes(dt_bias[0]), _pad_lanes(a_log[0]),
             jnp.repeat(d_skip[0], M_HEAD_DIM).reshape(1, M_INNER), row(m_norm_w),
             w_m_out[0].astype(BF16))
    hgrn_w = (hgrn_lb_param, row(h_norm_w), w_h_out[0].astype(BF16))
    merge_w = (w_o[0].astype(BF16), row(ln2_g), row(ln2_b)) + f2 + (row(ln3_g), row(ln3_b))

    def trunk_rows(x2d):
        x1 = _ffn_ln(x2d, *f1, row(ln1_g), row(ln1_b))
        proj, dt = _in_proj(x1, w_main, w_dt)
        return x1, proj, dt

    x1, proj, dt = trunk_rows(x_prompt.reshape(bp * lp, D_MODEL))
    proj3 = proj.reshape(bp, lp, PROJ_MAIN)
    ym, conv_p, ssm_p = _ssd_prompt(proj3, dt.reshape(bp, lp, LANES), ssd_w)
    yh, hg_p = _hgrn_prompt(proj3, hgrn_w)
    y_prompt = _merge_ffn(x1, ym.reshape(-1, D_MODEL), yh.reshape(-1, D_MODEL), proj, *merge_w)
    y_prompt = y_prompt.reshape(bp, lp, D_MODEL)
    new_conv_p = conv_p[:, SUBLANES - (M_CONV - 1):, :][None]
    new_ssm_p = ssm_p.reshape(1, bp, M_HEADS, M_HEAD_DIM, M_STATE)
    new_hg_p = hg_p[None]

    xs = jnp.pad(x_sample, ((0, 0), (SAMPLE_FIRST, 1), (0, 0)))
    x1s, projs, dts = trunk_rows(xs.reshape(bs * SAMPLE_ROWS, D_MODEL))
    projs3 = projs.reshape(bs, SAMPLE_ROWS, PROJ_MAIN)
    conv8 = jnp.pad(state_conv[0], ((0, 0), (0, SAMPLE_ROWS - (M_CONV - 1)), (0, 0)))
    yms, u_s, ssm_s = _ssd_sample(projs3, dts.reshape(bs, SAMPLE_ROWS, LANES), conv8,
                                  state_ssm[0].reshape(bs, M_INNER, M_STATE), ssd_w)
    yhs, hg_s = _hgrn_sample(projs3, state_hgrn[0], hgrn_w)
    y_s = _merge_ffn(x1s, yms.reshape(-1, D_MODEL), yhs.reshape(-1, D_MODEL), projs, *merge_w)
    y_sample = y_s.reshape(bs, SAMPLE_ROWS, D_MODEL)[:, SAMPLE_FIRST:SAMPLE_FIRST + ls, :]
    new_conv_s = u_s[:, ls:ls + M_CONV - 1, :][None]
    new_ssm_s = ssm_s.reshape(1, bs, M_HEADS, M_HEAD_DIM, M_STATE)
    new_hg_s = hg_s[None]

    return (y_prompt, y_sample, new_conv_p, new_ssm_p, new_hg_p, new_conv_s, new_ssm_s, new_hg_s)
```

```python
import functools

import jax
import jax.numpy as jnp
from jax import lax
from jax.experimental import pallas as pl
from jax.experimental.pallas import tpu as pltpu

F32 = jnp.float32
BF16 = jnp.bfloat16

D_MODEL = 1024
D_FF = 2816
M_HEADS = 16
M_HEAD_DIM = 64
M_GROUPS = 4
M_STATE = 128
M_INNER = M_HEADS * M_HEAD_DIM
M_CONV = 4
M_CONV_DIM = M_INNER + 2 * M_GROUPS * M_STATE
HEADS_PER_GROUP = M_HEADS // M_GROUPS
GROUP_COLS = HEADS_PER_GROUP * M_HEAD_DIM
H_HEADS = 8
H_KEY = 128
H_VAL = 128
H_CHUNK = 32
ALPHA = 2.0 ** 0.25
EPS = 1e-5

LANES = 128
SUBLANES = 8
BF16_SUBLANES = 16
VMEM_LIMIT = 56 * 1024 * 1024

FF_CHUNK = D_FF // 2
PROJ_MAIN = 9 * D_MODEL
COL_X, COL_Q, COL_GM = 0, 1, 2
COL_BC, COL_I, COL_GH = 3, 4, 5
COL_Z, COL_LOGF, COL_OG = 6, 7, 8

SSD_CHUNK = 128
SSD_TL = 256
HGRN_BLOCK = 128
HGRN_TL = 256
SAMPLE_ROWS = 8
SAMPLE_FIRST = M_CONV - 1
SAMPLE_BATCH_BLOCK = 8


def _sigmoid(x):
    return 1.0 / (1.0 + jnp.exp(-x))


def _silu(x):
    return x * _sigmoid(x)


def _softplus(x):
    return jnp.maximum(x, 0.0) + jnp.log(1.0 + jnp.exp(-jnp.abs(x)))


def _dot(a, b):
    return jnp.dot(a, b, preferred_element_type=F32)


def _dot_nt(a, b):
    return lax.dot_general(a, b, (((1,), (1,)), ((), ())), preferred_element_type=F32)


def _split3(x):
    hi = x.astype(BF16)
    r1 = x - hi.astype(F32)
    mid = r1.astype(BF16)
    lo = (r1 - mid.astype(F32)).astype(BF16)
    return hi, mid, lo


def _pad_rows(a):
    q = a.shape[0]
    if q == LANES:
        return a
    return jnp.concatenate([a, jnp.zeros((LANES - q, a.shape[1]), a.dtype)], axis=0)


def _chunk_cumsum(x, ch):
    rows, cols = x.shape
    if rows > ch and ch % BF16_SUBLANES == 0 and 3 * ch <= LANES:
        hi, mid, lo = _split3(x)
        row_i = lax.broadcasted_iota(jnp.int32, (ch, LANES), 0)
        col_i = lax.broadcasted_iota(jnp.int32, (ch, LANES), 1)
        sel = jnp.where((jnp.bitwise_and(col_i, ch - 1) <= row_i) & (col_i < 3 * ch), 1.0, 0.0)
        sel = sel.astype(BF16)
        pad = jnp.zeros((LANES - 3 * ch, cols), BF16)
        outs = []
        for c in range(rows // ch):
            rs = slice(c * ch, (c + 1) * ch)
            outs.append(_dot(sel, jnp.concatenate([hi[rs], mid[rs], lo[rs], pad], axis=0)))
        return jnp.concatenate(outs, axis=0)
    shift = ch.bit_length() - 1
    row_i = lax.broadcasted_iota(jnp.int32, (rows, LANES), 0)
    col_i = lax.broadcasted_iota(jnp.int32, (rows, LANES), 1)
    same = jnp.right_shift(row_i, shift) == jnp.right_shift(col_i, shift)
    sel = jnp.where(same & (col_i <= row_i), 1.0, 0.0).astype(BF16)
    hi, mid, lo = _split3(_pad_rows(x))
    return (_dot(sel, hi) + _dot(sel, mid)) + _dot(sel, lo)


def _layer_norm(y, g, b):
    mu = jnp.mean(y, axis=-1, keepdims=True)
    yc = y - mu
    var = jnp.mean(yc * yc, axis=-1, keepdims=True)
    return yc * lax.rsqrt(var + EPS) * g + b


def _swiglu(x, wg_ref, wu_ref, wd_ref):
    xb = x.astype(BF16)
    acc = None
    for c in range(D_FF // FF_CHUNK):
        sl = slice(c * FF_CHUNK, (c + 1) * FF_CHUNK)
        hg = _dot(xb, wg_ref[:, sl])
        hu = _dot(xb, wu_ref[:, sl])
        act = (_silu(hg) * hu).astype(BF16)
        part = _dot(act, wd_ref[sl, :])
        acc = part if acc is None else acc + part
    return acc


def _ffn_ln_kernel(x_ref, wg_ref, wu_ref, wd_ref, g_ref, b_ref, o_ref):
    x = x_ref[...]
    y = ALPHA * x + 0.5 * _swiglu(x, wg_ref, wu_ref, wd_ref)
    o_ref[...] = _layer_norm(y, g_ref[...], b_ref[...])


def _in_proj_kernel(x_ref, w_ref, wdt_ref, dtb_ref, lbp_ref, o_ref, odt_ref, k_ref):
    xb = x_ref[...].astype(BF16)
    for c in range(PROJ_MAIN // D_MODEL):
        cols = slice(c * D_MODEL, (c + 1) * D_MODEL)
        p = _dot(xb, w_ref[:, cols])
        if c in (COL_GM, COL_GH):
            o_ref[:, cols] = _sigmoid(p)
        elif c in (COL_Z, COL_OG):
            o_ref[:, cols] = _silu(p)
        elif c == COL_LOGF:
            log_f, k = _hgrn_gates(p, lbp_ref)
            o_ref[:, cols] = log_f
            k_ref[...] = k
        else:
            o_ref[:, cols] = p
    odt_ref[...] = _softplus(_dot(xb, wdt_ref[...]) + dtb_ref[...])


def _merge_ffn_kernel(x1_ref, ym_ref, yh_ref, gm_ref, gh_ref, wo_ref, g2_ref, b2_ref,
                      wg_ref, wu_ref, wd_ref, g3_ref, b3_ref, o_ref):
    mix = gm_ref[...] * ym_ref[...] + gh_ref[...] * yh_ref[...]
    x2 = _layer_norm(ALPHA * x1_ref[...] + _dot(mix.astype(BF16), wo_ref[...]),
                     g2_ref[...], b2_ref[...])
    y = ALPHA * x2 + 0.5 * _swiglu(x2, wg_ref, wu_ref, wd_ref)
    o_ref[...] = _layer_norm(y, g3_ref[...], b3_ref[...])


def _ssd_streams(chunks, h_prev, a_row, stage_major):
    q = chunks[0][1].shape[0]
    n = len(chunks)
    row_i = lax.broadcasted_iota(jnp.int32, (q, LANES), 0)
    col_i = lax.broadcasted_iota(jnp.int32, (q, LANES), 1)
    causal = col_i <= row_i
    head_blk = jnp.right_shift(lax.broadcasted_iota(jnp.int32, (q, GROUP_COLS), 1),
                               M_HEAD_DIM.bit_length() - 1)
    groups = [slice(g * M_STATE, (g + 1) * M_STATE) for g in range(M_GROUPS)]
    rows_r = [slice(r * M_HEAD_DIM, (r + 1) * M_HEAD_DIM) for r in range(HEADS_PER_GROUP)]

    pre, ops = {}, {}

    def prepare(i):
        _, xm, bm, cm, dt = chunks[i]
        a_cum = _chunk_cumsum(dt * a_row, q)
        a_last = a_cum[q - 1:q, :]
        to_end = jnp.exp(a_last - a_cum) * dt
        pre[i] = dict(
            a_cum=a_cum, e_last=jnp.exp(a_last), e_cum=jnp.exp(a_cum),
            a_cum_t=jnp.transpose(_pad_rows(a_cum)),
            dt_t=jnp.transpose(_pad_rows(dt)),
            to_end_t=jnp.transpose(_pad_rows(to_end)))
        for g in range(M_GROUPS):
            xg = _pad_rows(xm[:, g * GROUP_COLS:(g + 1) * GROUP_COLS])
            ops[i, g] = dict(bg=_pad_rows(bm[:, groups[g]]).astype(BF16),
                             cg=cm[:, groups[g]].astype(BF16),
                             xg_b=xg.astype(BF16), xg_t=jnp.transpose(xg))

    def mm_cb(i, g):
        return _dot_nt(ops[i, g]["cg"], ops[i, g]["bg"])

    def mm_state(i, g, r):
        h = g * HEADS_PER_GROUP + r
        lhs = (ops[i, g]["xg_t"][rows_r[r], :] * pre[i]["to_end_t"][h:h + 1, :]).astype(BF16)
        return _dot(lhs, ops[i, g]["bg"])

    def mm_intra(i, g, r, cb):
        h = g * HEADS_PER_GROUP + r
        seg = pre[i]["a_cum"][:, h:h + 1] - pre[i]["a_cum_t"][h:h + 1, :]
        decay = jnp.where(causal, jnp.exp(jnp.where(causal, seg, 0.0)), 0.0)
        w = (cb * decay * pre[i]["dt_t"][h:h + 1, :]).astype(BF16)
        return _dot(w, ops[i, g]["xg_b"])

    def mm_inter(i, g, blocks):
        hg = jnp.concatenate(blocks[g * HEADS_PER_GROUP:(g + 1) * HEADS_PER_GROUP], axis=0)
        return _dot_nt(ops[i, g]["cg"], hg.astype(BF16))

    def combine(i, g, parts, y_inter):
        acc = jnp.zeros((q, GROUP_COLS), F32)
        for r in range(HEADS_PER_GROUP):
            h = g * HEADS_PER_GROUP + r
            acc = jnp.where(head_blk == r, parts[r] + y_inter * pre[i]["e_cum"][:, h:h + 1], acc)
        return acc

    cur = {seq: list(blocks) for seq, blocks in h_prev.items()}
    ys = []
    if not stage_major:
        for i, chunk in enumerate(chunks):
            prepare(i)
            enter = cur[chunk[0]]
            cols, new = [], []
            for g in range(M_GROUPS):
                cb = mm_cb(i, g)
                y_inter = mm_inter(i, g, enter)
                acc = jnp.zeros((q, GROUP_COLS), F32)
                for r in range(HEADS_PER_GROUP):
                    h = g * HEADS_PER_GROUP + r
                    y_h = mm_intra(i, g, r, cb) + y_inter * pre[i]["e_cum"][:, h:h + 1]
                    acc = jnp.where(head_blk == r, y_h, acc)
                    new.append(pre[i]["e_last"][:, h:h + 1] * enter[h] + mm_state(i, g, r))
                cols.append(acc)
            cur[chunk[0]] = new
            ys.append(jnp.concatenate(cols, axis=1))
        return ys, cur
    for i in range(n):
        prepare(i)
    cb = {(i, g): mm_cb(i, g) for i in range(n) for g in range(M_GROUPS)}
    st = {(i, g, r): mm_state(i, g, r)
          for i in range(n) for g in range(M_GROUPS) for r in range(HEADS_PER_GROUP)}
    part = {(i, g, r): mm_intra(i, g, r, cb[i, g])
            for i in range(n) for g in range(M_GROUPS) for r in range(HEADS_PER_GROUP)}
    enter = []
    for i, chunk in enumerate(chunks):
        seq = chunk[0]
        enter.append(list(cur[seq]))
        cur[seq] = [pre[i]["e_last"][:, h:h + 1] * cur[seq][h]
                    + st[i, h // HEADS_PER_GROUP, h % HEADS_PER_GROUP] for h in range(M_HEADS)]
    for i in range(n):
        cols = []
        for g in range(M_GROUPS):
            y_inter = mm_inter(i, g, enter[i])
            cols.append(combine(i, g, [part[i, g, r] for r in range(HEADS_PER_GROUP)], y_inter))
        ys.append(jnp.concatenate(cols, axis=1))
    return ys, cur


def _ssd_gated_norm(y_ssd, xm, z_act, dskip, mnw):
    g = (y_ssd + dskip * xm) * z_act
    outs = []
    for k in range(M_GROUPS):
        gk = g[:, k * GROUP_COLS:(k + 1) * GROUP_COLS]
        outs.append(gk * lax.rsqrt(jnp.mean(gk * gk, axis=-1, keepdims=True) + EPS))
    return jnp.concatenate(outs, axis=1) * mnw


def _state_blocks(h):
    return [h[k * M_HEAD_DIM:(k + 1) * M_HEAD_DIM, :] for k in range(M_HEADS)]


def _conv_silu(u, prev8, cw, cbias):
    tl = u.shape[0]
    ext = jnp.concatenate([prev8, u], axis=0)
    conv = cbias + cw[M_CONV - 1:M_CONV, :] * u
    for j in range(1, M_CONV):
        shifted = pltpu.roll(ext, j, 0)[SUBLANES:SUBLANES + tl, :]
        conv = conv + cw[M_CONV - 1 - j:M_CONV - j, :] * shifted
    return _silu(conv)


def _ssd_prompt_kernel(x_ref, bc_ref, z_ref, dt_ref, cw_ref, cb_ref, alog_ref, dskip_ref,
                       mnw_ref, wmo_ref, ym_ref, conv_out_ref, ssm_out_ref, prev_scr, h_scr):
    t = pl.program_id(1)
    tl = x_ref.shape[1]

    @pl.when(t == 0)
    def _():
        prev_scr[...] = jnp.zeros_like(prev_scr)
        h_scr[...] = jnp.zeros_like(h_scr)

    tail = jnp.concatenate([x_ref[0, tl - SUBLANES:tl, :], bc_ref[0, tl - SUBLANES:tl, :]], axis=1)
    xm = _conv_silu(x_ref[0], prev_scr[:, :M_INNER], cw_ref[:, :M_INNER], cb_ref[:, :M_INNER])
    bc = _conv_silu(bc_ref[0], prev_scr[:, M_INNER:], cw_ref[:, M_INNER:], cb_ref[:, M_INNER:])
    prev_scr[...] = tail
    bm = bc[:, :M_GROUPS * M_STATE]
    cm = bc[:, M_GROUPS * M_STATE:]
    dt = dt_ref[0]
    a_row = -jnp.exp(alog_ref[...])
    chunks = []
    for c in range(tl // SSD_CHUNK):
        rs = slice(c * SSD_CHUNK, (c + 1) * SSD_CHUNK)
        chunks.append((0, xm[rs], bm[rs], cm[rs], dt[rs]))
    ys, h_new = _ssd_streams(chunks, {0: _state_blocks(h_scr[...])}, a_row, stage_major=False)
    for h in range(M_HEADS):
        h_scr[h * M_HEAD_DIM:(h + 1) * M_HEAD_DIM, :] = h_new[0][h]
    gn = _ssd_gated_norm(jnp.concatenate(ys, axis=0), xm, z_ref[0], dskip_ref[...], mnw_ref[...])
    ym_ref[0] = _dot(gn.astype(BF16), wmo_ref[...])

    @pl.when(t == pl.num_programs(1) - 1)
    def _():
        conv_out_ref[0] = tail
        ssm_out_ref[0] = h_scr[...]


def _sample_valid_rows(n_cols):
    row = lax.broadcasted_iota(jnp.int32, (SAMPLE_ROWS, n_cols), 0)
    return (row >= SAMPLE_FIRST) & (row < SAMPLE_ROWS - 1)


def _ssd_sample_kernel(x_ref, bc_ref, z_ref, dt_ref, cs_ref, h0_ref, cw_ref, cb_ref, alog_ref,
                       dskip_ref, mnw_ref, wmo_ref, ym_ref, u_out_ref, ssm_out_ref):
    bb = x_ref.shape[0]
    row = lax.broadcasted_iota(jnp.int32, (SAMPLE_ROWS, M_CONV_DIM), 0)
    valid = _sample_valid_rows(LANES)
    a_row = -jnp.exp(alog_ref[...])
    chunks, h_prev = [], {}
    for i in range(bb):
        raw = jnp.concatenate([x_ref[i], bc_ref[i]], axis=1)
        u = jnp.where(row < SAMPLE_FIRST, cs_ref[i], raw)
        u_out_ref[i] = u
        conv = cb_ref[...] + cw_ref[M_CONV - 1:M_CONV, :] * u
        for j in range(1, M_CONV):
            conv = conv + cw_ref[M_CONV - 1 - j:M_CONV - j, :] * pltpu.roll(u, j, 0)
        xbc = _silu(conv)
        xm = xbc[:, :M_INNER]
        bm = xbc[:, M_INNER:M_INNER + M_GROUPS * M_STATE]
        cm = xbc[:, M_INNER + M_GROUPS * M_STATE:]
        dt = jnp.where(valid, dt_ref[i], 0.0)
        chunks.append((i, xm, bm, cm, dt))
        h_prev[i] = _state_blocks(h0_ref[i])
    ys, h_new = _ssd_streams(chunks, h_prev, a_row, stage_major=True)
    gns = []
    for i in range(bb):
        for h in range(M_HEADS):
            ssm_out_ref[i, h * M_HEAD_DIM:(h + 1) * M_HEAD_DIM, :] = h_new[i][h]
        gns.append(_ssd_gated_norm(ys[i], chunks[i][1], z_ref[i], dskip_ref[...], mnw_ref[...]))
    ym = _dot(jnp.concatenate(gns, axis=0).astype(BF16), wmo_ref[...])
    for i in range(bb):
        ym_ref[i] = ym[i * SAMPLE_ROWS:(i + 1) * SAMPLE_ROWS, :]


def _hgrn_gates(f_raw, lbp_ref):
    p0 = lbp_ref[0:1, :]
    p1 = lbp_ref[1:2, :]
    m = jnp.maximum(p0, p1)
    e0 = jnp.exp(p0 - m)
    e1 = jnp.exp(p1 - m)
    lb = e0 / (e0 + e1)
    log_f = jnp.log(lb + (1.0 - lb) * _sigmoid(f_raw))
    k = (1.0 - lb) * _sigmoid(-f_raw)
    return log_f, k


def _hgrn_streams(blocks, s_prev, ch):
    rows = blocks[0][1].shape[0]
    n = len(blocks)
    shift = ch.bit_length() - 1
    n_chunks = rows // ch
    row_i = lax.broadcasted_iota(jnp.int32, (rows, LANES), 0)
    col_i = lax.broadcasted_iota(jnp.int32, (rows, LANES), 1)
    causal = (jnp.right_shift(row_i, shift) == jnp.right_shift(col_i, shift)) & (col_i <= row_i)
    chunk_t = jnp.right_shift(lax.broadcasted_iota(jnp.int32, (LANES, LANES), 1), shift)
    heads = [slice(h * H_KEY, (h + 1) * H_KEY) for h in range(H_HEADS)]
    chunks = [slice(c * ch, (c + 1) * ch) for c in range(n_chunks)]
    items = [(i, h) for i in range(n) for h in range(H_HEADS)]

    pre = []
    for (_, q, log_f, k, v) in blocks:
        b_cum = _chunk_cumsum(log_f, ch)
        lasts = [b_cum[c * ch + ch - 1:c * ch + ch, :] for c in range(n_chunks)]
        b_last = jnp.concatenate([jnp.broadcast_to(l, (ch, l.shape[1])) for l in lasts], axis=0)
        pre.append(dict(qd=q * jnp.exp(b_cum), kd=k * jnp.exp(-b_cum),
                        k_end=k * jnp.exp(b_last - b_cum), lasts=lasts, v=v))
    qh = {(i, h): pre[i]["qd"][:, heads[h]].astype(BF16) for (i, h) in items}
    vh = {(i, h): _pad_rows(pre[i]["v"][:, heads[h]]) for (i, h) in items}
    sc = {(i, h): _dot_nt(qh[i, h], _pad_rows(pre[i]["kd"][:, heads[h]]).astype(BF16))
          for (i, h) in items}
    ds = {}
    for (i, h) in items:
        v_t = jnp.transpose(vh[i, h])
        ke = _pad_rows(pre[i]["k_end"][:, heads[h]]).astype(BF16)
        ds[i, h] = [_dot((v_t if n_chunks == 1 else jnp.where(chunk_t == c, v_t, 0.0)).astype(BF16), ke)
                    for c in range(n_chunks)]
    y_intra = {(i, h): _dot(jnp.where(causal, sc[i, h], 0.0).astype(BF16), vh[i, h].astype(BF16))
               for (i, h) in items}
    cur = {seq: list(states) for seq, states in s_prev.items()}
    enter = {}
    for i, blk in enumerate(blocks):
        seq = blk[0]
        for h in range(H_HEADS):
            s = cur[seq][h]
            per_chunk = []
            for c in range(n_chunks):
                per_chunk.append(s)
                s = jnp.exp(pre[i]["lasts"][c][:, heads[h]]) * s + ds[i, h][c]
            enter[i, h] = per_chunk
            cur[seq][h] = s
    y_inter = {(i, h): [_dot_nt(qh[i, h][chunks[c]], enter[i, h][c].astype(BF16))
                        for c in range(n_chunks)] for (i, h) in items}
    outs = []
    for i in range(n):
        cols = []
        for h in range(H_HEADS):
            pieces = [y_intra[i, h][chunks[c]] + y_inter[i, h][c] for c in range(n_chunks)]
            o_h = pieces[0] if n_chunks == 1 else jnp.concatenate(pieces, axis=0)
            cols.append(o_h * lax.rsqrt(jnp.mean(o_h * o_h, axis=-1, keepdims=True) + EPS))
        outs.append(jnp.concatenate(cols, axis=1))
    return outs, cur


def _hgrn_prompt_kernel(q_ref, logf_ref, k_ref, i_ref, og_ref, hnw_ref, who_ref,
                        yh_ref, s_out_ref, s_scr):
    t = pl.program_id(1)
    tl = q_ref.shape[1]

    @pl.when(t == 0)
    def _():
        s_scr[...] = jnp.zeros_like(s_scr)

    log_f = logf_ref[0]
    k = k_ref[0]
    q = q_ref[0]
    v = i_ref[0]
    blocks = []
    for b in range(tl // HGRN_BLOCK):
        rs = slice(b * HGRN_BLOCK, (b + 1) * HGRN_BLOCK)
        blocks.append((0, q[rs], log_f[rs], k[rs], v[rs]))
    outs, s_new = _hgrn_streams(blocks, {0: [s_scr[h] for h in range(H_HEADS)]}, H_CHUNK)
    for h in range(H_HEADS):
        s_scr[h] = s_new[0][h]
    o = jnp.concatenate(outs, axis=0) * hnw_ref[...] * og_ref[0]
    yh_ref[0] = _dot(o.astype(BF16), who_ref[...])

    @pl.when(t == pl.num_programs(1) - 1)
    def _():
        for h in range(H_HEADS):
            s_out_ref[0, h] = jnp.transpose(s_scr[h])


def _hgrn_sample_kernel(q_ref, logf_ref, k_ref, i_ref, og_ref, s0_ref, hnw_ref, who_ref,
                        yh_ref, s_out_ref):
    bb = q_ref.shape[0]
    valid = _sample_valid_rows(H_HEADS * H_KEY)
    blocks, s_prev = [], {}
    for i in range(bb):
        blocks.append((i, q_ref[i], jnp.where(valid, logf_ref[i], 0.0),
                       jnp.where(valid, k_ref[i], 0.0), i_ref[i]))
        s_prev[i] = [jnp.transpose(s0_ref[i, h]) for h in range(H_HEADS)]
    outs, s_new = _hgrn_streams(blocks, s_prev, SAMPLE_ROWS)
    os_ = []
    for i in range(bb):
        for h in range(H_HEADS):
            s_out_ref[i, h] = jnp.transpose(s_new[i][h])
        os_.append(outs[i] * hnw_ref[...] * og_ref[i])
    yh = _dot(jnp.concatenate(os_, axis=0).astype(BF16), who_ref[...])
    for i in range(bb):
        yh_ref[i] = yh[i * SAMPLE_ROWS:(i + 1) * SAMPLE_ROWS, :]


def _resident(shape):
    nd = len(shape)
    return pl.BlockSpec(shape, lambda *_: (0,) * nd, pipeline_mode=pl.Buffered(1))


def _params(semantics):
    return pltpu.CompilerParams(dimension_semantics=semantics, vmem_limit_bytes=VMEM_LIMIT)


def _row_tile(n_rows, want):
    tm = min(want, n_rows)
    assert n_rows % tm == 0
    return tm


def _ffn_ln(x, wg, wu, wd, g, b):
    n = x.shape[0]
    tm = _row_tile(n, 512)
    row = pl.BlockSpec((tm, D_MODEL), lambda i: (i, 0))
    return pl.pallas_call(
        _ffn_ln_kernel,
        grid=(n // tm,),
        in_specs=[row, _resident(wg.shape), _resident(wu.shape), _resident(wd.shape),
                  _resident(g.shape), _resident(b.shape)],
        out_specs=row,
        out_shape=jax.ShapeDtypeStruct((n, D_MODEL), F32),
        compiler_params=_params(("parallel",)),
        name="ffn_ln",
    )(x, wg, wu, wd, g, b)


def _in_proj(x1, proj_w):
    n = x1.shape[0]
    tm = _row_tile(n, 256)
    row = lambda width: pl.BlockSpec((tm, width), lambda i: (i, 0))
    return pl.pallas_call(
        _in_proj_kernel,
        grid=(n // tm,),
        in_specs=[row(D_MODEL)] + [_resident(w.shape) for w in proj_w],
        out_specs=[row(PROJ_MAIN), row(LANES), row(D_MODEL)],
        out_shape=[jax.ShapeDtypeStruct((n, PROJ_MAIN), F32),
                   jax.ShapeDtypeStruct((n, LANES), F32),
                   jax.ShapeDtypeStruct((n, D_MODEL), F32)],
        compiler_params=_params(("parallel",)),
        name="in_proj",
    )(x1, *proj_w)


def _merge_ffn(x1, ym, yh, proj, wo, g2, b2, wg, wu, wd, g3, b3):
    n = x1.shape[0]
    tm = _row_tile(n, 256)
    row = pl.BlockSpec((tm, D_MODEL), lambda i: (i, 0))
    col = lambda c: pl.BlockSpec((tm, D_MODEL), lambda i: (i, c))
    return pl.pallas_call(
        _merge_ffn_kernel,
        grid=(n // tm,),
        in_specs=[row, row, row, col(COL_GM), col(COL_GH), _resident(wo.shape),
                  _resident(g2.shape), _resident(b2.shape), _resident(wg.shape),
                  _resident(wu.shape), _resident(wd.shape), _resident(g3.shape),
                  _resident(b3.shape)],
        out_specs=row,
        out_shape=jax.ShapeDtypeStruct((n, D_MODEL), F32),
        compiler_params=_params(("parallel",)),
        name="merge_ffn",
    )(x1, ym, yh, proj, proj, wo, g2, b2, wg, wu, wd, g3, b3)


def _ssd_prompt(proj3, dt3, conv_wb, ssd_w):
    bsz, length, _ = proj3.shape
    tl = SSD_TL
    col = lambda width, c: pl.BlockSpec((1, tl, width), lambda b, t: (b, t, c))
    weights = tuple(conv_wb) + tuple(ssd_w)
    return pl.pallas_call(
        _ssd_prompt_kernel,
        grid=(bsz, length // tl),
        in_specs=[col(D_MODEL, COL_X), col(D_MODEL, COL_BC), col(D_MODEL, COL_Z), col(LANES, 0)]
                 + [_resident(w.shape) for w in weights],
        out_specs=[pl.BlockSpec((1, tl, D_MODEL), lambda b, t: (b, t, 0)),
                   pl.BlockSpec((1, SUBLANES, M_CONV_DIM), lambda b, t: (b, 0, 0)),
                   pl.BlockSpec((1, M_INNER, M_STATE), lambda b, t: (b, 0, 0))],
        out_shape=[jax.ShapeDtypeStruct((bsz, length, D_MODEL), F32),
                   jax.ShapeDtypeStruct((bsz, SUBLANES, M_CONV_DIM), F32),
                   jax.ShapeDtypeStruct((bsz, M_INNER, M_STATE), F32)],
        scratch_shapes=[pltpu.VMEM((SUBLANES, M_CONV_DIM), F32),
                        pltpu.VMEM((M_INNER, M_STATE), F32)],
        compiler_params=_params(("parallel", "arbitrary")),
        name="ssd_prompt",
    )(proj3, proj3, proj3, dt3, *weights)


def _ssd_sample(proj3, dt3, conv8, ssm0, conv_wb, ssd_w):
    bsz = proj3.shape[0]
    bb = SAMPLE_BATCH_BLOCK
    col = lambda width, c: pl.BlockSpec((bb, SAMPLE_ROWS, width), lambda b: (b, 0, c))
    state = pl.BlockSpec((bb, M_INNER, M_STATE), lambda b: (b, 0, 0))
    weights = tuple(conv_wb) + tuple(ssd_w)
    return pl.pallas_call(
        _ssd_sample_kernel,
        grid=(bsz // bb,),
        in_specs=[col(D_MODEL, COL_X), col(D_MODEL, COL_BC), col(D_MODEL, COL_Z), col(LANES, 0),
                  col(M_CONV_DIM, 0), state] + [_resident(w.shape) for w in weights],
        out_specs=[col(D_MODEL, 0), col(M_CONV_DIM, 0), state],
        out_shape=[jax.ShapeDtypeStruct((bsz, SAMPLE_ROWS, D_MODEL), F32),
                   jax.ShapeDtypeStruct((bsz, SAMPLE_ROWS, M_CONV_DIM), F32),
                   jax.ShapeDtypeStruct((bsz, M_INNER, M_STATE), F32)],
        compiler_params=_params(("parallel",)),
        name="ssd_sample",
    )(proj3, proj3, proj3, dt3, conv8, ssm0, *weights)


def _hgrn_prompt(proj3, k3, hgrn_w):
    bsz, length, _ = proj3.shape
    tl = HGRN_TL
    col = lambda c: pl.BlockSpec((1, tl, D_MODEL), lambda b, t: (b, t, c))
    return pl.pallas_call(
        _hgrn_prompt_kernel,
        grid=(bsz, length // tl),
        in_specs=[col(COL_Q), col(COL_LOGF), col(0), col(COL_I), col(COL_OG)]
                 + [_resident(w.shape) for w in hgrn_w],
        out_specs=[pl.BlockSpec((1, tl, D_MODEL), lambda b, t: (b, t, 0)),
                   pl.BlockSpec((1, H_HEADS, H_KEY, H_VAL), lambda b, t: (b, 0, 0, 0))],
        out_shape=[jax.ShapeDtypeStruct((bsz, length, D_MODEL), F32),
                   jax.ShapeDtypeStruct((bsz, H_HEADS, H_KEY, H_VAL), F32)],
        scratch_shapes=[pltpu.VMEM((H_HEADS, H_VAL, H_KEY), F32)],
        compiler_params=_params(("parallel", "arbitrary")),
        name="hgrn_prompt",
    )(proj3, proj3, k3, proj3, proj3, *hgrn_w)


def _hgrn_sample(proj3, k3, s0, hgrn_w):
    bsz = proj3.shape[0]
    bb = SAMPLE_BATCH_BLOCK
    col = lambda c: pl.BlockSpec((bb, SAMPLE_ROWS, D_MODEL), lambda b: (b, 0, c))
    state = pl.BlockSpec((bb, H_HEADS, H_KEY, H_VAL), lambda b: (b, 0, 0, 0))
    return pl.pallas_call(
        _hgrn_sample_kernel,
        grid=(bsz // bb,),
        in_specs=[col(COL_Q), col(COL_LOGF), col(0), col(COL_I), col(COL_OG), state]
                 + [_resident(w.shape) for w in hgrn_w],
        out_specs=[col(0), state],
        out_shape=[jax.ShapeDtypeStruct((bsz, SAMPLE_ROWS, D_MODEL), F32),
                   jax.ShapeDtypeStruct((bsz, H_HEADS, H_KEY, H_VAL), F32)],
        compiler_params=_params(("parallel",)),
        name="hgrn_sample",
    )(proj3, proj3, k3, proj3, proj3, s0, *hgrn_w)


def _pad_lanes(v):
    return jnp.pad(v, (0, LANES - v.shape[0])).reshape(1, LANES)


def kernel(x_prompt, x_sample, state_conv, state_ssm, state_hgrn, ffn1_w_gate, ffn1_w_up, ffn1_w_down, ln1_g, ln1_b, w_in, conv_w, conv_b, dt_bias, a_log, d_skip, m_norm_w, w_m_out, hgrn_lb_param, h_norm_w, w_h_out, w_o, ln2_g, ln2_b, ffn2_w_gate, ffn2_w_up, ffn2_w_down, ln3_g, ln3_b):
    assert w_in.shape[0] == 1, "single trunk layer"
    bp, lp, _ = x_prompt.shape
    bs, ls, _ = x_sample.shape
    assert ls == SAMPLE_ROWS - M_CONV

    row = lambda v: v[0].reshape(1, -1)
    f1 = (ffn1_w_gate[0].astype(BF16), ffn1_w_up[0].astype(BF16), ffn1_w_down[0].astype(BF16))
    f2 = (ffn2_w_gate[0].astype(BF16), ffn2_w_up[0].astype(BF16), ffn2_w_down[0].astype(BF16))
    wi = w_in[0]
    piece = {}
    at = 0
    for name, width in (("z", M_INNER), ("x", M_INNER), ("bc", M_CONV_DIM - M_INNER), ("dt", M_HEADS),
                        ("q", D_MODEL), ("f", D_MODEL), ("i", D_MODEL), ("og", D_MODEL),
                        ("gm", D_MODEL), ("gh", D_MODEL)):
        piece[name] = wi[:, at:at + width]
        at += width
    order = ("x", "q", "gm", "bc", "i", "gh", "z", "f", "og")
    w_main = jnp.concatenate([piece[p] for p in order], axis=1).astype(BF16)
    w_dt = jnp.pad(piece["dt"], ((0, 0), (0, LANES - M_HEADS))).astype(BF16)
    proj_w = (w_main, w_dt, _pad_lanes(dt_bias[0]), hgrn_lb_param)
    conv_wb = (conv_w[0], row(conv_b))
    ssd_w = (_pad_lanes(a_log[0]), jnp.repeat(d_skip[0], M_HEAD_DIM).reshape(1, M_INNER),
             row(m_norm_w), w_m_out[0].astype(BF16))
    hgrn_w = (row(h_norm_w), w_h_out[0].astype(BF16))
    merge_w = (w_o[0].astype(BF16), row(ln2_g), row(ln2_b)) + f2 + (row(ln3_g), row(ln3_b))

    x1 = _ffn_ln(x_prompt.reshape(bp * lp, D_MODEL), *f1, row(ln1_g), row(ln1_b))
    proj, dt, kk = _in_proj(x1, proj_w)
    proj3 = proj.reshape(bp, lp, PROJ_MAIN)
    ym, conv_p, ssm_p = _ssd_prompt(proj3, dt.reshape(bp, lp, LANES), conv_wb, ssd_w)
    yh, hg_p = _hgrn_prompt(proj3, kk.reshape(bp, lp, D_MODEL), hgrn_w)
    y_prompt = _merge_ffn(x1, ym.reshape(-1, D_MODEL), yh.reshape(-1, D_MODEL), proj, *merge_w)
    y_prompt = y_prompt.reshape(bp, lp, D_MODEL)
    new_conv_p = conv_p[:, SUBLANES - (M_CONV - 1):, :][None]
    new_ssm_p = ssm_p.reshape(1, bp, M_HEADS, M_HEAD_DIM, M_STATE)
    new_hg_p = hg_p[None]

    xs = jnp.pad(x_sample, ((0, 0), (SAMPLE_FIRST, 1), (0, 0)))
    x1s = _ffn_ln(xs.reshape(bs * SAMPLE_ROWS, D_MODEL), *f1, row(ln1_g), row(ln1_b))
    projs, dts, kks = _in_proj(x1s, proj_w)
    projs3 = projs.reshape(bs, SAMPLE_ROWS, PROJ_MAIN)
    conv8 = jnp.pad(state_conv[0], ((0, 0), (0, SAMPLE_ROWS - (M_CONV - 1)), (0, 0)))
    yms, u_s, ssm_s = _ssd_sample(projs3, dts.reshape(bs, SAMPLE_ROWS, LANES), conv8,
                                  state_ssm[0].reshape(bs, M_INNER, M_STATE), conv_wb, ssd_w)
    yhs, hg_s = _hgrn_sample(projs3, kks.reshape(bs, SAMPLE_ROWS, D_MODEL), state_hgrn[0], hgrn_w)
    y_s = _merge_ffn(x1s, yms.reshape(-1, D_MODEL), yhs.reshape(-1, D_MODEL), projs, *merge_w)
    y_sample = y_s.reshape(bs, SAMPLE_ROWS, D_MODEL)[:, SAMPLE_FIRST:SAMPLE_FIRST + ls, :]
    new_conv_s = u_s[:, ls:ls + M_CONV - 1, :][None]
    new_ssm_s = ssm_s.reshape(1, bs, M_HEADS, M_HEAD_DIM, M_STATE)
    new_hg_s = hg_s[None]

    return (y_prompt, y_sample, new_conv_p, new_ssm_p, new_hg_p, new_conv_s, new_ssm_s, new_hg_s)
```

```python
import jax
import jax.numpy as jnp
from jax import lax
from jax.experimental import pallas as pl
from jax.experimental.pallas import tpu as pltpu

F32 = jnp.float32
BF16 = jnp.bfloat16

D_MODEL = 1024
D_FF = 2816
M_HEADS = 16
M_HEAD_DIM = 64
M_GROUPS = 4
M_STATE = 128
M_INNER = M_HEADS * M_HEAD_DIM
M_CONV = 4
M_CONV_DIM = M_INNER + 2 * M_GROUPS * M_STATE
HEADS_PER_GROUP = M_HEADS // M_GROUPS
GROUP_COLS = HEADS_PER_GROUP * M_HEAD_DIM
H_HEADS = 8
H_KEY = 128
H_VAL = 128
H_CHUNK = 32
ALPHA = 2.0 ** 0.25
EPS = 1e-5

LANES = 128
SUBLANES = 8
BF16_SUBLANES = 16
VMEM_LIMIT = 56 * 1024 * 1024

FF_CHUNK = D_FF // 2
PROJ_MAIN = 9 * D_MODEL
COL_X, COL_Q, COL_GM = 0, 1, 2
COL_BC, COL_I, COL_GH = 3, 4, 5
COL_Z, COL_LOGF, COL_OG = 6, 7, 8
W_Z_AT, W_X_AT, W_BC_AT, W_DT_AT = 0, M_INNER, 2 * M_INNER, M_INNER + M_CONV_DIM
W_Q_AT = W_DT_AT + LANES
W_F_AT, W_I_AT, W_OG_AT, W_GM_AT, W_GH_AT = (W_Q_AT + k * D_MODEL for k in range(1, 6))
W_COLS = W_GH_AT + D_MODEL
PROJ_PIECES = ((COL_X, W_X_AT), (COL_Q, W_Q_AT), (COL_GM, W_GM_AT), (COL_BC, W_BC_AT), (COL_I, W_I_AT),
               (COL_GH, W_GH_AT), (COL_Z, W_Z_AT), (COL_LOGF, W_F_AT), (COL_OG, W_OG_AT))

SSD_CHUNK = 128
SSD_TL = 256
HGRN_BLOCK = 128
HGRN_TL = 256
SAMPLE_ROWS = 8
SAMPLE_FIRST = M_CONV - 1
SAMPLE_BATCH_BLOCK = 8


def _sigmoid(x):
    return 1.0 / (1.0 + jnp.exp(-x))


def _silu(x):
    return x * _sigmoid(x)


def _softplus(x):
    return jnp.maximum(x, 0.0) + jnp.log(1.0 + jnp.exp(-jnp.abs(x)))


def _dot(a, b):
    return jnp.dot(a, b, preferred_element_type=F32)


def _dot_nt(a, b):
    return lax.dot_general(a, b, (((1,), (1,)), ((), ())), preferred_element_type=F32)


def _split3(x):
    hi = x.astype(BF16)
    r1 = x - hi.astype(F32)
    mid = r1.astype(BF16)
    lo = (r1 - mid.astype(F32)).astype(BF16)
    return hi, mid, lo


def _pad_rows(a):
    q = a.shape[0]
    if q == LANES:
        return a
    return jnp.concatenate([a, jnp.zeros((LANES - q, a.shape[1]), a.dtype)], axis=0)


def _chunk_cumsum(x, ch):
    rows, cols = x.shape
    if rows > ch and ch % BF16_SUBLANES == 0 and 3 * ch <= LANES:
        hi, mid, lo = _split3(x)
        row_i = lax.broadcasted_iota(jnp.int32, (ch, LANES), 0)
        col_i = lax.broadcasted_iota(jnp.int32, (ch, LANES), 1)
        sel = jnp.where((jnp.bitwise_and(col_i, ch - 1) <= row_i) & (col_i < 3 * ch), 1.0, 0.0)
        sel = sel.astype(BF16)
        pad = jnp.zeros((LANES - 3 * ch, cols), BF16)
        outs = []
        for c in range(rows // ch):
            rs = slice(c * ch, (c + 1) * ch)
            outs.append(_dot(sel, jnp.concatenate([hi[rs], mid[rs], lo[rs], pad], axis=0)))
        return jnp.concatenate(outs, axis=0)
    shift = ch.bit_length() - 1
    row_i = lax.broadcasted_iota(jnp.int32, (rows, LANES), 0)
    col_i = lax.broadcasted_iota(jnp.int32, (rows, LANES), 1)
    same = jnp.right_shift(row_i, shift) == jnp.right_shift(col_i, shift)
    sel = jnp.where(same & (col_i <= row_i), 1.0, 0.0).astype(BF16)
    hi, mid, lo = _split3(_pad_rows(x))
    return (_dot(sel, hi) + _dot(sel, mid)) + _dot(sel, lo)


def _layer_norm(y, g, b):
    mu = jnp.mean(y, axis=-1, keepdims=True)
    yc = y - mu
    var = jnp.mean(yc * yc, axis=-1, keepdims=True)
    return yc * lax.rsqrt(var + EPS) * g + b


def _swiglu(x, wg_ref, wu_ref, wd_ref):
    xb = x.astype(BF16)
    acc = None
    for c in range(D_FF // FF_CHUNK):
        sl = slice(c * FF_CHUNK, (c + 1) * FF_CHUNK)
        hg = _dot(xb, wg_ref[:, sl])
        hu = _dot(xb, wu_ref[:, sl])
        act = (_silu(hg) * hu).astype(BF16)
        part = _dot(act, wd_ref[sl, :])
        acc = part if acc is None else acc + part
    return acc


def _ffn_ln_kernel(x_ref, wg_ref, wu_ref, wd_ref, g_ref, b_ref, o_ref):
    x = x_ref[...]
    y = ALPHA * x + 0.5 * _swiglu(x, wg_ref, wu_ref, wd_ref)
    o_ref[...] = _layer_norm(y, g_ref[...], b_ref[...])


def _in_proj_kernel(x_ref, w_ref, dtb_ref, lbp_ref, o_ref, odt_ref, k_ref):
    xb = x_ref[...].astype(BF16)
    for c, w_at in PROJ_PIECES:
        cols = slice(c * D_MODEL, (c + 1) * D_MODEL)
        p = _dot(xb, w_ref[:, w_at:w_at + D_MODEL])
        if c in (COL_GM, COL_GH):
            o_ref[:, cols] = _sigmoid(p)
        elif c in (COL_Z, COL_OG):
            o_ref[:, cols] = _silu(p)
        elif c == COL_LOGF:
            log_f, k = _hgrn_gates(p, lbp_ref)
            o_ref[:, cols] = log_f
            k_ref[...] = k
        else:
            o_ref[:, cols] = p
    odt_ref[...] = _softplus(_dot(xb, w_ref[:, W_DT_AT:W_DT_AT + LANES]) + dtb_ref[...])


def _merge_ffn_kernel(x1_ref, mix_ref, wo_ref, g2_ref, b2_ref,
                      wg_ref, wu_ref, wd_ref, g3_ref, b3_ref, o_ref):
    x2 = _layer_norm(ALPHA * x1_ref[...] + _dot(mix_ref[...].astype(BF16), wo_ref[...]),
                     g2_ref[...], b2_ref[...])
    y = ALPHA * x2 + 0.5 * _swiglu(x2, wg_ref, wu_ref, wd_ref)
    o_ref[...] = _layer_norm(y, g3_ref[...], b3_ref[...])


def _ssd_streams(chunks, h_prev, a_row, stage_major):
    q = chunks[0][1].shape[0]
    n = len(chunks)
    row_i = lax.broadcasted_iota(jnp.int32, (q, LANES), 0)
    col_i = lax.broadcasted_iota(jnp.int32, (q, LANES), 1)
    causal = col_i <= row_i
    head_blk = jnp.right_shift(lax.broadcasted_iota(jnp.int32, (q, GROUP_COLS), 1),
                               M_HEAD_DIM.bit_length() - 1)
    groups = [slice(g * M_STATE, (g + 1) * M_STATE) for g in range(M_GROUPS)]
    rows_r = [slice(r * M_HEAD_DIM, (r + 1) * M_HEAD_DIM) for r in range(HEADS_PER_GROUP)]

    pre, ops = {}, {}

    def prepare(i):
        _, xm, bm, cm, dt = chunks[i]
        a_cum = _chunk_cumsum(dt * a_row, q)
        a_last = a_cum[q - 1:q, :]
        to_end = jnp.exp(a_last - a_cum) * dt
        pre[i] = dict(
            a_cum=a_cum, e_last=jnp.exp(a_last), e_cum=jnp.exp(a_cum),
            a_cum_t=jnp.transpose(_pad_rows(a_cum)),
            dt_t=jnp.transpose(_pad_rows(dt)),
            to_end_t=jnp.transpose(_pad_rows(to_end)))
        for g in range(M_GROUPS):
            xg = _pad_rows(xm[:, g * GROUP_COLS:(g + 1) * GROUP_COLS])
            ops[i, g] = dict(bg=_pad_rows(bm[:, groups[g]]).astype(BF16),
                             cg=cm[:, groups[g]].astype(BF16),
                             xg_b=xg.astype(BF16), xg_t=jnp.transpose(xg))

    def mm_cb(i, g):
        return _dot_nt(ops[i, g]["cg"], ops[i, g]["bg"])

    def mm_state(i, g, r):
        h = g * HEADS_PER_GROUP + r
        lhs = (ops[i, g]["xg_t"][rows_r[r], :] * pre[i]["to_end_t"][h:h + 1, :]).astype(BF16)
        return _dot(lhs, ops[i, g]["bg"])

    def mm_intra(i, g, r, cb):
        h = g * HEADS_PER_GROUP + r
        seg = pre[i]["a_cum"][:, h:h + 1] - pre[i]["a_cum_t"][h:h + 1, :]
        decay = jnp.where(causal, jnp.exp(jnp.where(causal, seg, 0.0)), 0.0)
        w = (cb * decay * pre[i]["dt_t"][h:h + 1, :]).astype(BF16)
        return _dot(w, ops[i, g]["xg_b"])

    def mm_inter(i, g, blocks):
        hg = jnp.concatenate(blocks[g * HEADS_PER_GROUP:(g + 1) * HEADS_PER_GROUP], axis=0)
        return _dot_nt(ops[i, g]["cg"], hg.astype(BF16))

    def combine(i, g, parts, y_inter):
        acc = jnp.zeros((q, GROUP_COLS), F32)
        for r in range(HEADS_PER_GROUP):
            h = g * HEADS_PER_GROUP + r
            acc = jnp.where(head_blk == r, parts[r] + y_inter * pre[i]["e_cum"][:, h:h + 1], acc)
        return acc

    cur = {seq: list(blocks) for seq, blocks in h_prev.items()}
    ys = []
    if not stage_major:
        for i, chunk in enumerate(chunks):
            prepare(i)
            enter = cur[chunk[0]]
            cols, new = [], []
            for g in range(M_GROUPS):
                cb = mm_cb(i, g)
                y_inter = mm_inter(i, g, enter)
                acc = jnp.zeros((q, GROUP_COLS), F32)
                for r in range(HEADS_PER_GROUP):
                    h = g * HEADS_PER_GROUP + r
                    y_h = mm_intra(i, g, r, cb) + y_inter * pre[i]["e_cum"][:, h:h + 1]
                    acc = jnp.where(head_blk == r, y_h, acc)
                    new.append(pre[i]["e_last"][:, h:h + 1] * enter[h] + mm_state(i, g, r))
                cols.append(acc)
            cur[chunk[0]] = new
            ys.append(jnp.concatenate(cols, axis=1))
        return ys, cur
    for i in range(n):
        prepare(i)
    cb = {(i, g): mm_cb(i, g) for i in range(n) for g in range(M_GROUPS)}
    st = {(i, g, r): mm_state(i, g, r)
          for i in range(n) for g in range(M_GROUPS) for r in range(HEADS_PER_GROUP)}
    part = {(i, g, r): mm_intra(i, g, r, cb[i, g])
            for i in range(n) for g in range(M_GROUPS) for r in range(HEADS_PER_GROUP)}
    enter = []
    for i, chunk in enumerate(chunks):
        seq = chunk[0]
        enter.append(list(cur[seq]))
        cur[seq] = [pre[i]["e_last"][:, h:h + 1] * cur[seq][h]
                    + st[i, h // HEADS_PER_GROUP, h % HEADS_PER_GROUP] for h in range(M_HEADS)]
    for i in range(n):
        cols = []
        for g in range(M_GROUPS):
            y_inter = mm_inter(i, g, enter[i])
            cols.append(combine(i, g, [part[i, g, r] for r in range(HEADS_PER_GROUP)], y_inter))
        ys.append(jnp.concatenate(cols, axis=1))
    return ys, cur


def _ssd_gated_norm(y_ssd, xm, z_act, dskip, mnw):
    g = (y_ssd + dskip * xm) * z_act
    outs = []
    for k in range(M_GROUPS):
        gk = g[:, k * GROUP_COLS:(k + 1) * GROUP_COLS]
        outs.append(gk * lax.rsqrt(jnp.mean(gk * gk, axis=-1, keepdims=True) + EPS))
    return jnp.concatenate(outs, axis=1) * mnw


def _state_blocks(h):
    return [h[k * M_HEAD_DIM:(k + 1) * M_HEAD_DIM, :] for k in range(M_HEADS)]


def _conv_silu(u, prev8, cw, cbias):
    tl = u.shape[0]
    ext = jnp.concatenate([prev8, u], axis=0)
    conv = cbias + cw[M_CONV - 1:M_CONV, :] * u
    for j in range(1, M_CONV):
        shifted = pltpu.roll(ext, j, 0)[SUBLANES:SUBLANES + tl, :]
        conv = conv + cw[M_CONV - 1 - j:M_CONV - j, :] * shifted
    return _silu(conv)


def _ssd_prompt_kernel(x_ref, bc_ref, z_ref, dt_ref, cw_ref, cb_ref, alog_ref, dskip_ref,
                       mnw_ref, wmo_ref, ym_ref, conv_out_ref, ssm_out_ref, prev_scr, h_scr):
    t = pl.program_id(1)
    tl = x_ref.shape[1]

    @pl.when(t == 0)
    def _():
        prev_scr[...] = jnp.zeros_like(prev_scr)
        h_scr[...] = jnp.zeros_like(h_scr)

    tail = jnp.concatenate([x_ref[0, tl - SUBLANES:tl, :], bc_ref[0, tl - SUBLANES:tl, :]], axis=1)
    xm = _conv_silu(x_ref[0], prev_scr[:, :M_INNER], cw_ref[:, :M_INNER], cb_ref[:, :M_INNER])
    bc = _conv_silu(bc_ref[0], prev_scr[:, M_INNER:], cw_ref[:, M_INNER:], cb_ref[:, M_INNER:])
    prev_scr[...] = tail
    bm = bc[:, :M_GROUPS * M_STATE]
    cm = bc[:, M_GROUPS * M_STATE:]
    dt = dt_ref[0]
    a_row = -jnp.exp(alog_ref[...])
    chunks = []
    for c in range(tl // SSD_CHUNK):
        rs = slice(c * SSD_CHUNK, (c + 1) * SSD_CHUNK)
        chunks.append((0, xm[rs], bm[rs], cm[rs], dt[rs]))
    ys, h_new = _ssd_streams(chunks, {0: _state_blocks(h_scr[...])}, a_row, stage_major=False)
    for h in range(M_HEADS):
        h_scr[h * M_HEAD_DIM:(h + 1) * M_HEAD_DIM, :] = h_new[0][h]
    gn = _ssd_gated_norm(jnp.concatenate(ys, axis=0), xm, z_ref[0], dskip_ref[...], mnw_ref[...])
    ym_ref[0] = _dot(gn.astype(BF16), wmo_ref[...])

    @pl.when(t == pl.num_programs(1) - 1)
    def _():
        conv_out_ref[0] = tail
        ssm_out_ref[0] = h_scr[...]


def _tile_rows(tokens, before):
    n, c = tokens.shape
    parts = [tokens, jnp.zeros((SAMPLE_ROWS - n - before, c), tokens.dtype)]
    if before:
        parts.insert(0, jnp.zeros((before, c), tokens.dtype))
    return jnp.concatenate(parts, axis=0)


def _ssd_sample_kernel(x_ref, bc_ref, z_ref, dt_ref, cs_ref, h0_ref, cw_ref, cb_ref, alog_ref,
                       dskip_ref, mnw_ref, wmo_ref, ym_ref, conv_out_ref, ssm_out_ref):
    bb, n_tok = x_ref.shape[0], x_ref.shape[1]
    a_row = -jnp.exp(alog_ref[...])
    chunks, zs, h_prev = [], [], {}
    for i in range(bb):
        raw = jnp.concatenate([x_ref[i], bc_ref[i]], axis=1)
        u = jnp.concatenate([cs_ref[i], raw, jnp.zeros((1, M_CONV_DIM), F32)], axis=0)
        conv_out_ref[i] = u[n_tok:n_tok + M_CONV - 1, :]
        conv = cb_ref[...] + cw_ref[M_CONV - 1:M_CONV, :] * u
        for j in range(1, M_CONV):
            conv = conv + cw_ref[M_CONV - 1 - j:M_CONV - j, :] * pltpu.roll(u, j, 0)
        xbc = _silu(conv)
        xm = xbc[:, :M_INNER]
        bm = xbc[:, M_INNER:M_INNER + M_GROUPS * M_STATE]
        cm = xbc[:, M_INNER + M_GROUPS * M_STATE:]
        chunks.append((i, xm, bm, cm, _tile_rows(dt_ref[i], SAMPLE_FIRST)))
        zs.append(_tile_rows(z_ref[i], SAMPLE_FIRST))
        h_prev[i] = _state_blocks(h0_ref[i])
    ys, h_new = _ssd_streams(chunks, h_prev, a_row, stage_major=True)
    gns = []
    for i in range(bb):
        for h in range(M_HEADS):
            ssm_out_ref[i, h * M_HEAD_DIM:(h + 1) * M_HEAD_DIM, :] = h_new[i][h]
        gns.append(_ssd_gated_norm(ys[i], chunks[i][1], zs[i], dskip_ref[...], mnw_ref[...]))
    ym = _dot(jnp.concatenate(gns, axis=0).astype(BF16), wmo_ref[...])
    for i in range(bb):
        ym_ref[i] = ym[i * SAMPLE_ROWS + SAMPLE_FIRST:i * SAMPLE_ROWS + SAMPLE_FIRST + n_tok, :]


def _hgrn_gates(f_raw, lbp_ref):
    p0 = lbp_ref[0:1, :]
    p1 = lbp_ref[1:2, :]
    m = jnp.maximum(p0, p1)
    e0 = jnp.exp(p0 - m)
    e1 = jnp.exp(p1 - m)
    lb = e0 / (e0 + e1)
    log_f = jnp.log(lb + (1.0 - lb) * _sigmoid(f_raw))
    k = (1.0 - lb) * _sigmoid(-f_raw)
    return log_f, k


def _hgrn_streams(blocks, s_prev, ch):
    rows = blocks[0][1].shape[0]
    n = len(blocks)
    shift = ch.bit_length() - 1
    n_chunks = rows // ch
    row_i = lax.broadcasted_iota(jnp.int32, (rows, LANES), 0)
    col_i = lax.broadcasted_iota(jnp.int32, (rows, LANES), 1)
    causal = (jnp.right_shift(row_i, shift) == jnp.right_shift(col_i, shift)) & (col_i <= row_i)
    chunk_t = jnp.right_shift(lax.broadcasted_iota(jnp.int32, (LANES, LANES), 1), shift)
    heads = [slice(h * H_KEY, (h + 1) * H_KEY) for h in range(H_HEADS)]
    chunks = [slice(c * ch, (c + 1) * ch) for c in range(n_chunks)]
    items = [(i, h) for i in range(n) for h in range(H_HEADS)]

    pre = []
    for (_, q, log_f, k, v) in blocks:
        b_cum = _chunk_cumsum(log_f, ch)
        lasts = [b_cum[c * ch + ch - 1:c * ch + ch, :] for c in range(n_chunks)]
        b_last = jnp.concatenate([jnp.broadcast_to(l, (ch, l.shape[1])) for l in lasts], axis=0)
        pre.append(dict(qd=q * jnp.exp(b_cum), kd=k * jnp.exp(-b_cum),
                        k_end=k * jnp.exp(b_last - b_cum), lasts=lasts, v=v))
    qh = {(i, h): pre[i]["qd"][:, heads[h]].astype(BF16) for (i, h) in items}
    vh = {(i, h): _pad_rows(pre[i]["v"][:, heads[h]]) for (i, h) in items}
    sc = {(i, h): _dot_nt(qh[i, h], _pad_rows(pre[i]["kd"][:, heads[h]]).astype(BF16))
          for (i, h) in items}
    ds = {}
    for (i, h) in items:
        v_t = jnp.transpose(vh[i, h])
        ke = _pad_rows(pre[i]["k_end"][:, heads[h]]).astype(BF16)
        ds[i, h] = [_dot((v_t if n_chunks == 1 else jnp.where(chunk_t == c, v_t, 0.0)).astype(BF16), ke)
                    for c in range(n_chunks)]
    y_intra = {(i, h): _dot(jnp.where(causal, sc[i, h], 0.0).astype(BF16), vh[i, h].astype(BF16))
               for (i, h) in items}
    cur = {seq: list(states) for seq, states in s_prev.items()}
    enter = {}
    for i, blk in enumerate(blocks):
        seq = blk[0]
        for h in range(H_HEADS):
            s = cur[seq][h]
            per_chunk = []
            for c in range(n_chunks):
                per_chunk.append(s)
                s = jnp.exp(pre[i]["lasts"][c][:, heads[h]]) * s + ds[i, h][c]
            enter[i, h] = per_chunk
            cur[seq][h] = s
    y_inter = {(i, h): [_dot_nt(qh[i, h][chunks[c]], enter[i, h][c].astype(BF16))
                        for c in range(n_chunks)] for (i, h) in items}
    outs = []
    for i in range(n):
        cols = []
        for h in range(H_HEADS):
            pieces = [y_intra[i, h][chunks[c]] + y_inter[i, h][c] for c in range(n_chunks)]
            o_h = pieces[0] if n_chunks == 1 else jnp.concatenate(pieces, axis=0)
            cols.append(o_h * lax.rsqrt(jnp.mean(o_h * o_h, axis=-1, keepdims=True) + EPS))
        outs.append(jnp.concatenate(cols, axis=1))
    return outs, cur


def _hgrn_prompt_kernel(q_ref, logf_ref, k_ref, i_ref, og_ref, ym_ref, gm_ref, gh_ref, hnw_ref, who_ref,
                        mix_ref, s_out_ref, s_scr):
    t = pl.program_id(1)
    tl = q_ref.shape[1]

    @pl.when(t == 0)
    def _():
        s_scr[...] = jnp.zeros_like(s_scr)

    log_f = logf_ref[0]
    k = k_ref[0]
    q = q_ref[0]
    v = i_ref[0]
    blocks = []
    for b in range(tl // HGRN_BLOCK):
        rs = slice(b * HGRN_BLOCK, (b + 1) * HGRN_BLOCK)
        blocks.append((0, q[rs], log_f[rs], k[rs], v[rs]))
    outs, s_new = _hgrn_streams(blocks, {0: [s_scr[h] for h in range(H_HEADS)]}, H_CHUNK)
    for h in range(H_HEADS):
        s_scr[h] = s_new[0][h]
    o = jnp.concatenate(outs, axis=0) * hnw_ref[...] * og_ref[0]
    mix_ref[0] = gm_ref[0] * ym_ref[0] + gh_ref[0] * _dot(o.astype(BF16), who_ref[...])

    @pl.when(t == pl.num_programs(1) - 1)
    def _():
        for h in range(H_HEADS):
            s_out_ref[0, h] = jnp.transpose(s_scr[h])


def _hgrn_sample_kernel(q_ref, logf_ref, k_ref, i_ref, og_ref, ym_ref, gm_ref, gh_ref, s0_ref,
                        hnw_ref, who_ref, mix_ref, s_out_ref):
    bb, n_tok = q_ref.shape[0], q_ref.shape[1]
    blocks, s_prev = [], {}
    for i in range(bb):
        blocks.append((i, _tile_rows(q_ref[i], 0), _tile_rows(logf_ref[i], 0),
                       _tile_rows(k_ref[i], 0), _tile_rows(i_ref[i], 0)))
        s_prev[i] = [jnp.transpose(s0_ref[i, h]) for h in range(H_HEADS)]
    outs, s_new = _hgrn_streams(blocks, s_prev, SAMPLE_ROWS)
    os_ = []
    for i in range(bb):
        for h in range(H_HEADS):
            s_out_ref[i, h] = jnp.transpose(s_new[i][h])
        os_.append(outs[i] * hnw_ref[...] * _tile_rows(og_ref[i], 0))
    yh = _dot(jnp.concatenate(os_, axis=0).astype(BF16), who_ref[...])
    for i in range(bb):
        mix_ref[i] = gm_ref[i] * ym_ref[i] + gh_ref[i] * yh[i * SAMPLE_ROWS:i * SAMPLE_ROWS + n_tok, :]


def _resident(shape):
    nd = len(shape)
    return pl.BlockSpec(shape, lambda *_: (0,) * nd, pipeline_mode=pl.Buffered(1))


def _params(semantics):
    return pltpu.CompilerParams(dimension_semantics=semantics, vmem_limit_bytes=VMEM_LIMIT)


def _row_tile(n_rows, want):
    tm = min(want, n_rows)
    assert n_rows % tm == 0
    return tm


def _ffn_ln(x, wg, wu, wd, g, b):
    n = x.shape[0]
    tm = _row_tile(n, 512)
    row = pl.BlockSpec((tm, D_MODEL), lambda i: (i, 0))
    return pl.pallas_call(
        _ffn_ln_kernel,
        grid=(n // tm,),
        in_specs=[row, _resident(wg.shape), _resident(wu.shape), _resident(wd.shape),
                  _resident(g.shape), _resident(b.shape)],
        out_specs=row,
        out_shape=jax.ShapeDtypeStruct((n, D_MODEL), F32),
        compiler_params=_params(("parallel",)),
        name="ffn_ln",
    )(x, wg, wu, wd, g, b)


def _in_proj(x1, proj_w):
    n = x1.shape[0]
    tm = _row_tile(n, 256)
    row = lambda width: pl.BlockSpec((tm, width), lambda i: (i, 0))
    return pl.pallas_call(
        _in_proj_kernel,
        grid=(n // tm,),
        in_specs=[row(D_MODEL)] + [_resident(w.shape) for w in proj_w],
        out_specs=[row(PROJ_MAIN), row(LANES), row(D_MODEL)],
        out_shape=[jax.ShapeDtypeStruct((n, PROJ_MAIN), F32),
                   jax.ShapeDtypeStruct((n, LANES), F32),
                   jax.ShapeDtypeStruct((n, D_MODEL), F32)],
        compiler_params=_params(("parallel",)),
        name="in_proj",
    )(x1, *proj_w)


def _merge_ffn(x1, mix, *merge_w):
    n = x1.shape[0]
    tm = _row_tile(n, 512)
    row = pl.BlockSpec((tm, D_MODEL), lambda i: (i, 0))
    return pl.pallas_call(
        _merge_ffn_kernel,
        grid=(n // tm,),
        in_specs=[row, row] + [_resident(w.shape) for w in merge_w],
        out_specs=row,
        out_shape=jax.ShapeDtypeStruct((n, D_MODEL), F32),
        compiler_params=_params(("parallel",)),
        name="merge_ffn",
    )(x1, mix, *merge_w)


def _ssd_prompt(proj3, dt3, conv_wb, ssd_w):
    bsz, length, _ = proj3.shape
    tl = SSD_TL
    col = lambda width, c: pl.BlockSpec((1, tl, width), lambda b, t: (b, t, c))
    weights = tuple(conv_wb) + tuple(ssd_w)
    return pl.pallas_call(
        _ssd_prompt_kernel,
        grid=(bsz, length // tl),
        in_specs=[col(D_MODEL, COL_X), col(D_MODEL, COL_BC), col(D_MODEL, COL_Z), col(LANES, 0)]
                 + [_resident(w.shape) for w in weights],
        out_specs=[pl.BlockSpec((1, tl, D_MODEL), lambda b, t: (b, t, 0)),
                   pl.BlockSpec((1, SUBLANES, M_CONV_DIM), lambda b, t: (b, 0, 0)),
                   pl.BlockSpec((1, M_INNER, M_STATE), lambda b, t: (b, 0, 0))],
        out_shape=[jax.ShapeDtypeStruct((bsz, length, D_MODEL), F32),
                   jax.ShapeDtypeStruct((bsz, SUBLANES, M_CONV_DIM), F32),
                   jax.ShapeDtypeStruct((bsz, M_INNER, M_STATE), F32)],
        scratch_shapes=[pltpu.VMEM((SUBLANES, M_CONV_DIM), F32),
                        pltpu.VMEM((M_INNER, M_STATE), F32)],
        compiler_params=_params(("parallel", "arbitrary")),
        name="ssd_prompt",
    )(proj3, proj3, proj3, dt3, *weights)


def _ssd_sample(proj3, dt3, conv0, ssm0, conv_wb, ssd_w):
    bsz, n_tok, _ = proj3.shape
    bb = SAMPLE_BATCH_BLOCK
    col = lambda rows, width, c: pl.BlockSpec((bb, rows, width), lambda b: (b, 0, c))
    state = pl.BlockSpec((bb, M_INNER, M_STATE), lambda b: (b, 0, 0))
    weights = tuple(conv_wb) + tuple(ssd_w)
    return pl.pallas_call(
        _ssd_sample_kernel,
        grid=(bsz // bb,),
        in_specs=[col(n_tok, D_MODEL, COL_X), col(n_tok, D_MODEL, COL_BC), col(n_tok, D_MODEL, COL_Z),
                  col(n_tok, LANES, 0), col(M_CONV - 1, M_CONV_DIM, 0), state]
                 + [_resident(w.shape) for w in weights],
        out_specs=[col(n_tok, D_MODEL, 0), col(M_CONV - 1, M_CONV_DIM, 0), state],
        out_shape=[jax.ShapeDtypeStruct((bsz, n_tok, D_MODEL), F32),
                   jax.ShapeDtypeStruct((bsz, M_CONV - 1, M_CONV_DIM), F32),
                   jax.ShapeDtypeStruct((bsz, M_INNER, M_STATE), F32)],
        compiler_params=_params(("parallel",)),
        name="ssd_sample",
    )(proj3, proj3, proj3, dt3, conv0, ssm0, *weights)


def _hgrn_prompt(proj3, k3, ym3, hgrn_w):
    bsz, length, _ = proj3.shape
    tl = HGRN_TL
    col = lambda c: pl.BlockSpec((1, tl, D_MODEL), lambda b, t: (b, t, c))
    return pl.pallas_call(
        _hgrn_prompt_kernel,
        grid=(bsz, length // tl),
        in_specs=[col(COL_Q), col(COL_LOGF), col(0), col(COL_I), col(COL_OG), col(0), col(COL_GM),
                  col(COL_GH)] + [_resident(w.shape) for w in hgrn_w],
        out_specs=[pl.BlockSpec((1, tl, D_MODEL), lambda b, t: (b, t, 0)),
                   pl.BlockSpec((1, H_HEADS, H_KEY, H_VAL), lambda b, t: (b, 0, 0, 0))],
        out_shape=[jax.ShapeDtypeStruct((bsz, length, D_MODEL), F32),
                   jax.ShapeDtypeStruct((bsz, H_HEADS, H_KEY, H_VAL), F32)],
        scratch_shapes=[pltpu.VMEM((H_HEADS, H_VAL, H_KEY), F32)],
        compiler_params=_params(("parallel", "arbitrary")),
        name="hgrn_prompt",
    )(proj3, proj3, k3, proj3, proj3, ym3, proj3, proj3, *hgrn_w)


def _hgrn_sample(proj3, k3, ym3, s0, hgrn_w):
    bsz, n_tok, _ = proj3.shape
    bb = SAMPLE_BATCH_BLOCK
    col = lambda c: pl.BlockSpec((bb, n_tok, D_MODEL), lambda b: (b, 0, c))
    state = pl.BlockSpec((bb, H_HEADS, H_KEY, H_VAL), lambda b: (b, 0, 0, 0))
    return pl.pallas_call(
        _hgrn_sample_kernel,
        grid=(bsz // bb,),
        in_specs=[col(COL_Q), col(COL_LOGF), col(0), col(COL_I), col(COL_OG), col(0), col(COL_GM),
                  col(COL_GH), state] + [_resident(w.shape) for w in hgrn_w],
        out_specs=[col(0), state],
        out_shape=[jax.ShapeDtypeStruct((bsz, n_tok, D_MODEL), F32),
                   jax.ShapeDtypeStruct((bsz, H_HEADS, H_KEY, H_VAL), F32)],
        compiler_params=_params(("parallel",)),
        name="hgrn_sample",
    )(proj3, proj3, k3, proj3, proj3, ym3, proj3, proj3, s0, *hgrn_w)


def _pad_lanes(v):
    return jnp.pad(v, (0, LANES - v.shape[0])).reshape(1, LANES)


def kernel(x_prompt, x_sample, state_conv, state_ssm, state_hgrn, ffn1_w_gate, ffn1_w_up, ffn1_w_down, ln1_g, ln1_b, w_in, conv_w, conv_b, dt_bias, a_log, d_skip, m_norm_w, w_m_out, hgrn_lb_param, h_norm_w, w_h_out, w_o, ln2_g, ln2_b, ffn2_w_gate, ffn2_w_up, ffn2_w_down, ln3_g, ln3_b):
    assert w_in.shape[0] == 1, "single trunk layer"
    bp, lp, _ = x_prompt.shape
    bs, ls, _ = x_sample.shape
    assert ls == SAMPLE_ROWS - M_CONV

    row = lambda v: v[0].reshape(1, -1)
    f1 = (ffn1_w_gate[0].astype(BF16), ffn1_w_up[0].astype(BF16), ffn1_w_down[0].astype(BF16))
    f2 = (ffn2_w_gate[0].astype(BF16), ffn2_w_up[0].astype(BF16), ffn2_w_down[0].astype(BF16))
    wi = w_in[0]
    dt_end = W_DT_AT + M_HEADS
    w_all = jnp.concatenate([wi[:, :dt_end], jnp.zeros((D_MODEL, LANES - M_HEADS), wi.dtype),
                             wi[:, dt_end:]], axis=1).astype(BF16)
    proj_w = (w_all, _pad_lanes(dt_bias[0]), hgrn_lb_param)
    conv_wb = (conv_w[0], row(conv_b))
    ssd_w = (_pad_lanes(a_log[0]), jnp.repeat(d_skip[0], M_HEAD_DIM).reshape(1, M_INNER),
             row(m_norm_w), w_m_out[0].astype(BF16))
    hgrn_w = (row(h_norm_w), w_h_out[0].astype(BF16))
    merge_w = (w_o[0].astype(BF16), row(ln2_g), row(ln2_b)) + f2 + (row(ln3_g), row(ln3_b))

    x1 = _ffn_ln(x_prompt.reshape(bp * lp, D_MODEL), *f1, row(ln1_g), row(ln1_b))
    proj, dt, kk = _in_proj(x1, proj_w)
    proj3 = proj.reshape(bp, lp, PROJ_MAIN)
    ym, conv_p, ssm_p = _ssd_prompt(proj3, dt.reshape(bp, lp, LANES), conv_wb, ssd_w)
    mix, hg_p = _hgrn_prompt(proj3, kk.reshape(bp, lp, D_MODEL), ym, hgrn_w)
    y_prompt = _merge_ffn(x1, mix.reshape(-1, D_MODEL), *merge_w).reshape(bp, lp, D_MODEL)
    new_conv_p = conv_p[:, SUBLANES - (M_CONV - 1):, :][None]
    new_ssm_p = ssm_p.reshape(1, bp, M_HEADS, M_HEAD_DIM, M_STATE)
    new_hg_p = hg_p[None]

    x1s = _ffn_ln(x_sample.reshape(bs * ls, D_MODEL), *f1, row(ln1_g), row(ln1_b))
    projs, dts, kks = _in_proj(x1s, proj_w)
    projs3 = projs.reshape(bs, ls, PROJ_MAIN)
    yms, conv_s, ssm_s = _ssd_sample(projs3, dts.reshape(bs, ls, LANES), state_conv[0],
                                     state_ssm[0].reshape(bs, M_INNER, M_STATE), conv_wb, ssd_w)
    mixs, hg_s = _hgrn_sample(projs3, kks.reshape(bs, ls, D_MODEL), yms, state_hgrn[0], hgrn_w)
    y_sample = _merge_ffn(x1s, mixs.reshape(-1, D_MODEL), *merge_w).reshape(bs, ls, D_MODEL)
    new_conv_s = conv_s[None]
    new_ssm_s = ssm_s.reshape(1, bs, M_HEADS, M_HEAD_DIM, M_STATE)
    new_hg_s = hg_s[None]

    return (y_prompt, y_sample, new_conv_p, new_ssm_p, new_hg_p, new_conv_s, new_ssm_s, new_hg_s)
```

```python
import jax
import jax.numpy as jnp
from jax import lax
from jax.experimental import pallas as pl
from jax.experimental.pallas import tpu as pltpu

F32 = jnp.float32
BF16 = jnp.bfloat16

D_MODEL = 1024
D_FF = 2816
M_HEADS = 16
M_HEAD_DIM = 64
M_GROUPS = 4
M_STATE = 128
M_INNER = M_HEADS * M_HEAD_DIM
M_CONV = 4
M_CONV_DIM = M_INNER + 2 * M_GROUPS * M_STATE
HEADS_PER_GROUP = M_HEADS // M_GROUPS
GROUP_COLS = HEADS_PER_GROUP * M_HEAD_DIM
H_HEADS = 8
H_KEY = 128
H_VAL = 128
H_CHUNK = 32
ALPHA = 2.0 ** 0.25
EPS = 1e-5

LANES = 128
SUBLANES = 8
BF16_SUBLANES = 16
VMEM_LIMIT = 56 * 1024 * 1024

FF_CHUNK = D_FF // 2
PROJ_MAIN = 9 * D_MODEL
COL_X, COL_Q, COL_GM = 0, 1, 2
COL_BC, COL_I, COL_GH = 3, 4, 5
COL_Z, COL_LOGF, COL_OG = 6, 7, 8
W_DT_AT = M_INNER + M_CONV_DIM
PROJ_PIECES = ((COL_X, ("a", M_INNER)), (COL_Q, ("b", 0)), (COL_GM, ("b", 4 * D_MODEL)),
               (COL_BC, ("a", 2 * M_INNER)), (COL_I, ("b", 2 * D_MODEL)), (COL_GH, ("b", 5 * D_MODEL)),
               (COL_Z, ("a", 0)), (COL_LOGF, ("b", D_MODEL)), (COL_OG, ("b", 3 * D_MODEL)))

SSD_CHUNK = 128
HGRN_BLOCK = 128
MIXER_TL = 256
MIXER_INTERLEAVE = (1, 2)
SAMPLE_ROWS = 8
SAMPLE_FIRST = M_CONV - 1
SAMPLE_BATCH_BLOCK = 8


def _sigmoid(x):
    return 1.0 / (1.0 + jnp.exp(-x))


def _silu(x):
    return x * _sigmoid(x)


def _softplus(x):
    return jnp.maximum(x, 0.0) + jnp.log(1.0 + jnp.exp(-jnp.abs(x)))


def _dot(a, b):
    return jnp.dot(a, b, preferred_element_type=F32)


def _dot_nt(a, b):
    return lax.dot_general(a, b, (((1,), (1,)), ((), ())), preferred_element_type=F32)


def _split3(x):
    hi = x.astype(BF16)
    r1 = x - hi.astype(F32)
    mid = r1.astype(BF16)
    lo = (r1 - mid.astype(F32)).astype(BF16)
    return hi, mid, lo


def _pad_rows(a):
    q = a.shape[0]
    if q == LANES:
        return a
    return jnp.concatenate([a, jnp.zeros((LANES - q, a.shape[1]), a.dtype)], axis=0)


def _chunk_cumsum(x, ch):
    rows, cols = x.shape
    if rows > ch and ch % BF16_SUBLANES == 0 and 3 * ch <= LANES:
        hi, mid, lo = _split3(x)
        row_i = lax.broadcasted_iota(jnp.int32, (ch, LANES), 0)
        col_i = lax.broadcasted_iota(jnp.int32, (ch, LANES), 1)
        sel = jnp.where((jnp.bitwise_and(col_i, ch - 1) <= row_i) & (col_i < 3 * ch), 1.0, 0.0)
        sel = sel.astype(BF16)
        pad = jnp.zeros((LANES - 3 * ch, cols), BF16)
        outs = []
        for c in range(rows // ch):
            rs = slice(c * ch, (c + 1) * ch)
            outs.append(_dot(sel, jnp.concatenate([hi[rs], mid[rs], lo[rs], pad], axis=0)))
        return jnp.concatenate(outs, axis=0)
    shift = ch.bit_length() - 1
    row_i = lax.broadcasted_iota(jnp.int32, (rows, LANES), 0)
    col_i = lax.broadcasted_iota(jnp.int32, (rows, LANES), 1)
    same = jnp.right_shift(row_i, shift) == jnp.right_shift(col_i, shift)
    sel = jnp.where(same & (col_i <= row_i), 1.0, 0.0).astype(BF16)
    hi, mid, lo = _split3(_pad_rows(x))
    return (_dot(sel, hi) + _dot(sel, mid)) + _dot(sel, lo)


def _layer_norm(y, g, b):
    mu = jnp.mean(y, axis=-1, keepdims=True)
    yc = y - mu
    var = jnp.mean(yc * yc, axis=-1, keepdims=True)
    return yc * lax.rsqrt(var + EPS) * g + b


def _swiglu(x, wg_ref, wu_ref, wd_ref):
    xb = x.astype(BF16)
    acc = None
    for c in range(D_FF // FF_CHUNK):
        sl = slice(c * FF_CHUNK, (c + 1) * FF_CHUNK)
        hg = _dot(xb, wg_ref[:, sl])
        hu = _dot(xb, wu_ref[:, sl])
        act = (_silu(hg) * hu).astype(BF16)
        part = _dot(act, wd_ref[sl, :])
        acc = part if acc is None else acc + part
    return acc


def _ffn_ln_kernel(x_ref, wg_ref, wu_ref, wd_ref, g_ref, b_ref, o_ref):
    x = x_ref[...]
    y = ALPHA * x + 0.5 * _swiglu(x, wg_ref, wu_ref, wd_ref)
    o_ref[...] = _layer_norm(y, g_ref[...], b_ref[...])


def _in_proj_kernel(x_ref, wa_ref, wb_ref, wdt_ref, dtb_ref, lbp_ref, o_ref, odt_ref, k_ref):
    xb = x_ref[...].astype(BF16)
    for c, (w_ref, w_at) in PROJ_PIECES:
        w_ref = wa_ref if w_ref == "a" else wb_ref
        cols = slice(c * D_MODEL, (c + 1) * D_MODEL)
        p = _dot(xb, w_ref[:, w_at:w_at + D_MODEL])
        if c in (COL_GM, COL_GH):
            o_ref[:, cols] = _sigmoid(p)
        elif c in (COL_Z, COL_OG):
            o_ref[:, cols] = _silu(p)
        elif c == COL_LOGF:
            log_f, k = _hgrn_gates(p, lbp_ref)
            o_ref[:, cols] = log_f
            k_ref[...] = k
        else:
            o_ref[:, cols] = p
    odt_ref[...] = _softplus(_dot(xb, wdt_ref[...]) + dtb_ref[...])


def _merge_ffn_kernel(x1_ref, mix_ref, wo_ref, g2_ref, b2_ref,
                      wg_ref, wu_ref, wd_ref, g3_ref, b3_ref, o_ref):
    x2 = _layer_norm(ALPHA * x1_ref[...] + _dot(mix_ref[...].astype(BF16), wo_ref[...]),
                     g2_ref[...], b2_ref[...])
    y = ALPHA * x2 + 0.5 * _swiglu(x2, wg_ref, wu_ref, wd_ref)
    o_ref[...] = _layer_norm(y, g3_ref[...], b3_ref[...])


def _ssd_streams(chunks, h_prev, a_row, stage_major):
    q = chunks[0][1].shape[0]
    n = len(chunks)
    row_i = lax.broadcasted_iota(jnp.int32, (q, LANES), 0)
    col_i = lax.broadcasted_iota(jnp.int32, (q, LANES), 1)
    causal = col_i <= row_i
    head_blk = jnp.right_shift(lax.broadcasted_iota(jnp.int32, (q, GROUP_COLS), 1),
                               M_HEAD_DIM.bit_length() - 1)
    groups = [slice(g * M_STATE, (g + 1) * M_STATE) for g in range(M_GROUPS)]
    rows_r = [slice(r * M_HEAD_DIM, (r + 1) * M_HEAD_DIM) for r in range(HEADS_PER_GROUP)]

    pre, ops = {}, {}

    def prepare(i):
        _, xm, bm, cm, dt = chunks[i]
        a_cum = _chunk_cumsum(dt * a_row, q)
        a_last = a_cum[q - 1:q, :]
        to_end = jnp.exp(a_last - a_cum) * dt
        pre[i] = dict(
            a_cum=a_cum, e_last=jnp.exp(a_last), e_cum=jnp.exp(a_cum),
            a_cum_t=jnp.transpose(_pad_rows(a_cum)),
            dt_t=jnp.transpose(_pad_rows(dt)),
            to_end_t=jnp.transpose(_pad_rows(to_end)))
        for g in range(M_GROUPS):
            xg = _pad_rows(xm[:, g * GROUP_COLS:(g + 1) * GROUP_COLS])
            ops[i, g] = dict(bg=_pad_rows(bm[:, groups[g]]).astype(BF16),
                             cg=cm[:, groups[g]].astype(BF16),
                             xg_b=xg.astype(BF16), xg_t=jnp.transpose(xg))

    def mm_cb(i, g):
        return _dot_nt(ops[i, g]["cg"], ops[i, g]["bg"])

    def mm_state(i, g, r):
        h = g * HEADS_PER_GROUP + r
        lhs = (ops[i, g]["xg_t"][rows_r[r], :] * pre[i]["to_end_t"][h:h + 1, :]).astype(BF16)
        return _dot(lhs, ops[i, g]["bg"])

    def mm_intra(i, g, r, cb):
        h = g * HEADS_PER_GROUP + r
        seg = pre[i]["a_cum"][:, h:h + 1] - pre[i]["a_cum_t"][h:h + 1, :]
        decay = jnp.where(causal, jnp.exp(jnp.where(causal, seg, 0.0)), 0.0)
        w = (cb * decay * pre[i]["dt_t"][h:h + 1, :]).astype(BF16)
        return _dot(w, ops[i, g]["xg_b"])

    def mm_inter(i, g, blocks):
        hg = jnp.concatenate(blocks[g * HEADS_PER_GROUP:(g + 1) * HEADS_PER_GROUP], axis=0)
        return _dot_nt(ops[i, g]["cg"], hg.astype(BF16))

    def combine(i, g, parts, y_inter):
        acc = jnp.zeros((q, GROUP_COLS), F32)
        for r in range(HEADS_PER_GROUP):
            h = g * HEADS_PER_GROUP + r
            acc = jnp.where(head_blk == r, parts[r] + y_inter * pre[i]["e_cum"][:, h:h + 1], acc)
        return acc

    cur = {seq: list(blocks) for seq, blocks in h_prev.items()}
    ys = []
    if not stage_major:
        for i, chunk in enumerate(chunks):
            prepare(i)
            enter = cur[chunk[0]]
            cols, new = [], []
            for g in range(M_GROUPS):
                cb = mm_cb(i, g)
                y_inter = mm_inter(i, g, enter)
                yield
                acc = jnp.zeros((q, GROUP_COLS), F32)
                for r in range(HEADS_PER_GROUP):
                    h = g * HEADS_PER_GROUP + r
                    y_h = mm_intra(i, g, r, cb) + y_inter * pre[i]["e_cum"][:, h:h + 1]
                    acc = jnp.where(head_blk == r, y_h, acc)
                    new.append(pre[i]["e_last"][:, h:h + 1] * enter[h] + mm_state(i, g, r))
                    yield
                cols.append(acc)
            cur[chunk[0]] = new
            ys.append(jnp.concatenate(cols, axis=1))
        return ys, cur
    for i in range(n):
        prepare(i)
    cb = {(i, g): mm_cb(i, g) for i in range(n) for g in range(M_GROUPS)}
    st = {(i, g, r): mm_state(i, g, r)
          for i in range(n) for g in range(M_GROUPS) for r in range(HEADS_PER_GROUP)}
    part = {(i, g, r): mm_intra(i, g, r, cb[i, g])
            for i in range(n) for g in range(M_GROUPS) for r in range(HEADS_PER_GROUP)}
    enter = []
    for i, chunk in enumerate(chunks):
        seq = chunk[0]
        enter.append(list(cur[seq]))
        cur[seq] = [pre[i]["e_last"][:, h:h + 1] * cur[seq][h]
                    + st[i, h // HEADS_PER_GROUP, h % HEADS_PER_GROUP] for h in range(M_HEADS)]
    for i in range(n):
        cols = []
        for g in range(M_GROUPS):
            y_inter = mm_inter(i, g, enter[i])
            cols.append(combine(i, g, [part[i, g, r] for r in range(HEADS_PER_GROUP)], y_inter))
        ys.append(jnp.concatenate(cols, axis=1))
    return ys, cur


def _ssd_gated_norm(y_ssd, xm, z_act, dskip, mnw):
    g = (y_ssd + dskip * xm) * z_act
    outs = []
    for k in range(M_GROUPS):
        gk = g[:, k * GROUP_COLS:(k + 1) * GROUP_COLS]
        outs.append(gk * lax.rsqrt(jnp.mean(gk * gk, axis=-1, keepdims=True) + EPS))
    return jnp.concatenate(outs, axis=1) * mnw


def _state_blocks(h):
    return [h[k * M_HEAD_DIM:(k + 1) * M_HEAD_DIM, :] for k in range(M_HEADS)]


def _conv_silu(u, prev8, cw, cbias):
    tl = u.shape[0]
    ext = jnp.concatenate([prev8, u], axis=0)
    conv = cbias + cw[M_CONV - 1:M_CONV, :] * u
    for j in range(1, M_CONV):
        shifted = pltpu.roll(ext, j, 0)[SUBLANES:SUBLANES + tl, :]
        conv = conv + cw[M_CONV - 1 - j:M_CONV - j, :] * shifted
    return _silu(conv)


def _drain(gen):
    try:
        while True:
            next(gen)
    except StopIteration as stop:
        return stop.value


def _interleave(gens, steps):
    results = [None] * len(gens)
    live = list(range(len(gens)))
    while live:
        for i in list(live):
            try:
                for _ in range(steps[i]):
                    next(gens[i])
            except StopIteration as stop:
                results[i] = stop.value
                live.remove(i)
    return results


def _mixer_prompt_kernel(x_ref, bc_ref, z_ref, dt_ref, q_ref, logf_ref, k_ref, i_ref, og_ref, gm_ref,
                         gh_ref, cw_ref, cb_ref, alog_ref, dskip_ref, mnw_ref, wmo_ref, hnw_ref, who_ref,
                         mix_ref, conv_out_ref, ssm_out_ref, s_out_ref, prev_scr, h_scr, s_scr):
    t = pl.program_id(1)
    tl = x_ref.shape[1]

    @pl.when(t == 0)
    def _():
        prev_scr[...] = jnp.zeros_like(prev_scr)
        h_scr[...] = jnp.zeros_like(h_scr)
        s_scr[...] = jnp.zeros_like(s_scr)

    tail = jnp.concatenate([x_ref[0, tl - SUBLANES:tl, :], bc_ref[0, tl - SUBLANES:tl, :]], axis=1)
    xm = _conv_silu(x_ref[0], prev_scr[:, :M_INNER], cw_ref[:, :M_INNER], cb_ref[:, :M_INNER])
    bc = _conv_silu(bc_ref[0], prev_scr[:, M_INNER:], cw_ref[:, M_INNER:], cb_ref[:, M_INNER:])
    prev_scr[...] = tail
    bm = bc[:, :M_GROUPS * M_STATE]
    cm = bc[:, M_GROUPS * M_STATE:]
    dt = dt_ref[0]
    a_row = -jnp.exp(alog_ref[...])
    chunks = []
    for c in range(tl // SSD_CHUNK):
        rs = slice(c * SSD_CHUNK, (c + 1) * SSD_CHUNK)
        chunks.append((0, xm[rs], bm[rs], cm[rs], dt[rs]))
    q, log_f, k, v = q_ref[0], logf_ref[0], k_ref[0], i_ref[0]
    blocks = []
    for b in range(tl // HGRN_BLOCK):
        rs = slice(b * HGRN_BLOCK, (b + 1) * HGRN_BLOCK)
        blocks.append((0, q[rs], log_f[rs], k[rs], v[rs]))

    ssd = _ssd_streams(chunks, {0: _state_blocks(h_scr[...])}, a_row, stage_major=False)
    hgrn = _hgrn_streams(blocks, {0: [s_scr[h] for h in range(H_HEADS)]}, H_CHUNK)
    (ys, h_new), (outs, s_new) = _interleave((ssd, hgrn), MIXER_INTERLEAVE)
    for h in range(M_HEADS):
        h_scr[h * M_HEAD_DIM:(h + 1) * M_HEAD_DIM, :] = h_new[0][h]
    for h in range(H_HEADS):
        s_scr[h] = s_new[0][h]
    gn = _ssd_gated_norm(jnp.concatenate(ys, axis=0), xm, z_ref[0], dskip_ref[...], mnw_ref[...])
    ym = _dot(gn.astype(BF16), wmo_ref[...])
    o = jnp.concatenate(outs, axis=0) * hnw_ref[...] * og_ref[0]
    yh = _dot(o.astype(BF16), who_ref[...])
    mix_ref[0] = gm_ref[0] * ym + gh_ref[0] * yh

    @pl.when(t == pl.num_programs(1) - 1)
    def _():
        conv_out_ref[0] = tail
        ssm_out_ref[0] = h_scr[...]
        for h in range(H_HEADS):
            s_out_ref[0, h] = jnp.transpose(s_scr[h])


def _tile_rows(tokens, before):
    n, c = tokens.shape
    parts = [tokens, jnp.zeros((SAMPLE_ROWS - n - before, c), tokens.dtype)]
    if before:
        parts.insert(0, jnp.zeros((before, c), tokens.dtype))
    return jnp.concatenate(parts, axis=0)


def _ssd_sample_kernel(x_ref, bc_ref, z_ref, dt_ref, cs_ref, h0_ref, cw_ref, cb_ref, alog_ref,
                       dskip_ref, mnw_ref, wmo_ref, ym_ref, conv_out_ref, ssm_out_ref):
    bb = h0_ref.shape[0]
    n_tok = x_ref.shape[0] // bb
    a_row = -jnp.exp(alog_ref[...])
    chunks, zs, h_prev = [], [], {}
    for i in range(bb):
        rs = slice(i * n_tok, (i + 1) * n_tok)
        raw = jnp.concatenate([x_ref[rs, :], bc_ref[rs, :]], axis=1)
        u = jnp.concatenate([cs_ref[i], raw, jnp.zeros((1, M_CONV_DIM), F32)], axis=0)
        conv_out_ref[i] = u[n_tok:n_tok + M_CONV - 1, :]
        conv = cb_ref[...] + cw_ref[M_CONV - 1:M_CONV, :] * u
        for j in range(1, M_CONV):
            conv = conv + cw_ref[M_CONV - 1 - j:M_CONV - j, :] * pltpu.roll(u, j, 0)
        xbc = _silu(conv)
        xm = xbc[:, :M_INNER]
        bm = xbc[:, M_INNER:M_INNER + M_GROUPS * M_STATE]
        cm = xbc[:, M_INNER + M_GROUPS * M_STATE:]
        chunks.append((i, xm, bm, cm, _tile_rows(dt_ref[rs, :], SAMPLE_FIRST)))
        zs.append(_tile_rows(z_ref[rs, :], SAMPLE_FIRST))
        h_prev[i] = _state_blocks(h0_ref[i])
    ys, h_new = _drain(_ssd_streams(chunks, h_prev, a_row, stage_major=True))
    gns = []
    for i in range(bb):
        for h in range(M_HEADS):
            ssm_out_ref[i, h * M_HEAD_DIM:(h + 1) * M_HEAD_DIM, :] = h_new[i][h]
        gns.append(_ssd_gated_norm(ys[i], chunks[i][1], zs[i], dskip_ref[...], mnw_ref[...]))
    ym = _dot(jnp.concatenate(gns, axis=0).astype(BF16), wmo_ref[...])
    for i in range(bb):
        ym_ref[i * n_tok:(i + 1) * n_tok, :] = ym[i * SAMPLE_ROWS + SAMPLE_FIRST:
                                                  i * SAMPLE_ROWS + SAMPLE_FIRST + n_tok, :]


def _hgrn_gates(f_raw, lbp_ref):
    p0 = lbp_ref[0:1, :]
    p1 = lbp_ref[1:2, :]
    m = jnp.maximum(p0, p1)
    e0 = jnp.exp(p0 - m)
    e1 = jnp.exp(p1 - m)
    lb = e0 / (e0 + e1)
    log_f = jnp.log(lb + (1.0 - lb) * _sigmoid(f_raw))
    k = (1.0 - lb) * _sigmoid(-f_raw)
    return log_f, k


def _hgrn_streams(blocks, s_prev, ch):
    rows = blocks[0][1].shape[0]
    n = len(blocks)
    shift = ch.bit_length() - 1
    n_chunks = rows // ch
    row_i = lax.broadcasted_iota(jnp.int32, (rows, LANES), 0)
    col_i = lax.broadcasted_iota(jnp.int32, (rows, LANES), 1)
    causal = (jnp.right_shift(row_i, shift) == jnp.right_shift(col_i, shift)) & (col_i <= row_i)
    chunk_t = jnp.right_shift(lax.broadcasted_iota(jnp.int32, (LANES, LANES), 1), shift)
    heads = [slice(h * H_KEY, (h + 1) * H_KEY) for h in range(H_HEADS)]
    chunks = [slice(c * ch, (c + 1) * ch) for c in range(n_chunks)]
    items = [(i, h) for i in range(n) for h in range(H_HEADS)]

    pre = []
    for (_, q, log_f, k, v) in blocks:
        b_cum = _chunk_cumsum(log_f, ch)
        lasts = [b_cum[c * ch + ch - 1:c * ch + ch, :] for c in range(n_chunks)]
        b_last = jnp.concatenate([jnp.broadcast_to(l, (ch, l.shape[1])) for l in lasts], axis=0)
        pre.append(dict(qd=q * jnp.exp(b_cum), kd=k * jnp.exp(-b_cum),
                        k_end=k * jnp.exp(b_last - b_cum), lasts=lasts, v=v))
    qh = {(i, h): pre[i]["qd"][:, heads[h]].astype(BF16) for (i, h) in items}
    vh = {(i, h): _pad_rows(pre[i]["v"][:, heads[h]]) for (i, h) in items}
    sc, ds, y_intra, y_inter = {}, {}, {}, {}
    for (i, h) in items:
        sc[i, h] = _dot_nt(qh[i, h], _pad_rows(pre[i]["kd"][:, heads[h]]).astype(BF16))
        yield
    for (i, h) in items:
        v_t = jnp.transpose(vh[i, h])
        ke = _pad_rows(pre[i]["k_end"][:, heads[h]]).astype(BF16)
        ds[i, h] = []
        for c in range(n_chunks):
            v_tc = v_t if n_chunks == 1 else jnp.where(chunk_t == c, v_t, 0.0)
            ds[i, h].append(_dot(v_tc.astype(BF16), ke))
            yield
    for (i, h) in items:
        y_intra[i, h] = _dot(jnp.where(causal, sc[i, h], 0.0).astype(BF16), vh[i, h].astype(BF16))
        yield
    cur = {seq: list(states) for seq, states in s_prev.items()}
    enter = {}
    for i, blk in enumerate(blocks):
        seq = blk[0]
        for h in range(H_HEADS):
            s = cur[seq][h]
            per_chunk = []
            for c in range(n_chunks):
                per_chunk.append(s)
                s = jnp.exp(pre[i]["lasts"][c][:, heads[h]]) * s + ds[i, h][c]
            enter[i, h] = per_chunk
            cur[seq][h] = s
    for (i, h) in items:
        y_inter[i, h] = []
        for c in range(n_chunks):
            y_inter[i, h].append(_dot_nt(qh[i, h][chunks[c]], enter[i, h][c].astype(BF16)))
            yield
    outs = []
    for i in range(n):
        cols = []
        for h in range(H_HEADS):
            pieces = [y_intra[i, h][chunks[c]] + y_inter[i, h][c] for c in range(n_chunks)]
            o_h = pieces[0] if n_chunks == 1 else jnp.concatenate(pieces, axis=0)
            cols.append(o_h * lax.rsqrt(jnp.mean(o_h * o_h, axis=-1, keepdims=True) + EPS))
        outs.append(jnp.concatenate(cols, axis=1))
    return outs, cur


def _hgrn_sample_kernel(q_ref, logf_ref, k_ref, i_ref, og_ref, ym_ref, gm_ref, gh_ref, s0_ref,
                        hnw_ref, who_ref, mix_ref, s_out_ref):
    bb = s0_ref.shape[0]
    n_tok = q_ref.shape[0] // bb
    rows = [slice(i * n_tok, (i + 1) * n_tok) for i in range(bb)]
    blocks, s_prev = [], {}
    for i in range(bb):
        blocks.append((i, _tile_rows(q_ref[rows[i], :], 0), _tile_rows(logf_ref[rows[i], :], 0),
                       _tile_rows(k_ref[rows[i], :], 0), _tile_rows(i_ref[rows[i], :], 0)))
        s_prev[i] = [jnp.transpose(s0_ref[i, h]) for h in range(H_HEADS)]
    outs, s_new = _drain(_hgrn_streams(blocks, s_prev, SAMPLE_ROWS))
    os_ = []
    for i in range(bb):
        for h in range(H_HEADS):
            s_out_ref[i, h] = jnp.transpose(s_new[i][h])
        os_.append(outs[i] * hnw_ref[...] * _tile_rows(og_ref[rows[i], :], 0))
    yh = _dot(jnp.concatenate(os_, axis=0).astype(BF16), who_ref[...])
    for i in range(bb):
        mix_ref[rows[i], :] = (gm_ref[rows[i], :] * ym_ref[rows[i], :]
                               + gh_ref[rows[i], :] * yh[i * SAMPLE_ROWS:i * SAMPLE_ROWS + n_tok, :])


def _resident(shape):
    nd = len(shape)
    return pl.BlockSpec(shape, lambda *_: (0,) * nd, pipeline_mode=pl.Buffered(1))


def _params(semantics):
    return pltpu.CompilerParams(dimension_semantics=semantics, vmem_limit_bytes=VMEM_LIMIT)


def _row_tile(n_rows, want):
    tm = min(want, n_rows)
    assert n_rows % tm == 0
    return tm


def _ffn_ln(x, wg, wu, wd, g, b):
    n = x.shape[0]
    tm = _row_tile(n, 512)
    row = pl.BlockSpec((tm, D_MODEL), lambda i: (i, 0))
    return pl.pallas_call(
        _ffn_ln_kernel,
        grid=(n // tm,),
        in_specs=[row, _resident(wg.shape), _resident(wu.shape), _resident(wd.shape),
                  _resident(g.shape), _resident(b.shape)],
        out_specs=row,
        out_shape=jax.ShapeDtypeStruct((n, D_MODEL), F32),
        compiler_params=_params(("parallel",)),
        name="ffn_ln",
    )(x, wg, wu, wd, g, b)


def _in_proj(x1, proj_w):
    n = x1.shape[0]
    tm = _row_tile(n, 256)
    row = lambda width: pl.BlockSpec((tm, width), lambda i: (i, 0))
    return pl.pallas_call(
        _in_proj_kernel,
        grid=(n // tm,),
        in_specs=[row(D_MODEL)] + [_resident(w.shape) for w in proj_w],
        out_specs=[row(PROJ_MAIN), row(LANES), row(D_MODEL)],
        out_shape=[jax.ShapeDtypeStruct((n, PROJ_MAIN), F32),
                   jax.ShapeDtypeStruct((n, LANES), F32),
                   jax.ShapeDtypeStruct((n, D_MODEL), F32)],
        compiler_params=_params(("parallel",)),
        name="in_proj",
    )(x1, *proj_w)


def _merge_ffn(x1, mix, *merge_w):
    n = x1.shape[0]
    tm = _row_tile(n, 512)
    row = pl.BlockSpec((tm, D_MODEL), lambda i: (i, 0))
    return pl.pallas_call(
        _merge_ffn_kernel,
        grid=(n // tm,),
        in_specs=[row, row] + [_resident(w.shape) for w in merge_w],
        out_specs=row,
        out_shape=jax.ShapeDtypeStruct((n, D_MODEL), F32),
        compiler_params=_params(("parallel",)),
        name="merge_ffn",
    )(x1, mix, *merge_w)


def _mixer_prompt(proj3, dt3, k3, conv_wb, ssd_w, hgrn_w):
    bsz, length, _ = proj3.shape
    tl = MIXER_TL
    col = lambda width, c: pl.BlockSpec((1, tl, width), lambda b, t: (b, t, c))
    once = lambda *shape: pl.BlockSpec((1,) + shape, lambda b, t: (b,) + (0,) * len(shape))
    weights = tuple(conv_wb) + tuple(ssd_w) + tuple(hgrn_w)
    pc = lambda c: (proj3, col(D_MODEL, c))
    ins = [pc(COL_X), pc(COL_BC), pc(COL_Z), (dt3, col(LANES, 0)), pc(COL_Q), pc(COL_LOGF),
           (k3, col(D_MODEL, 0)), pc(COL_I), pc(COL_OG), pc(COL_GM), pc(COL_GH)]
    operands = [a for a, _ in ins]
    in_specs = [s for _, s in ins]
    return pl.pallas_call(
        _mixer_prompt_kernel,
        grid=(bsz, length // tl),
        in_specs=in_specs + [_resident(w.shape) for w in weights],
        out_specs=[col(D_MODEL, 0), once(SUBLANES, M_CONV_DIM), once(M_INNER, M_STATE),
                   once(H_HEADS, H_KEY, H_VAL)],
        out_shape=[jax.ShapeDtypeStruct((bsz, length, D_MODEL), F32),
                   jax.ShapeDtypeStruct((bsz, SUBLANES, M_CONV_DIM), F32),
                   jax.ShapeDtypeStruct((bsz, M_INNER, M_STATE), F32),
                   jax.ShapeDtypeStruct((bsz, H_HEADS, H_KEY, H_VAL), F32)],
        scratch_shapes=[pltpu.VMEM((SUBLANES, M_CONV_DIM), F32),
                        pltpu.VMEM((M_INNER, M_STATE), F32),
                        pltpu.VMEM((H_HEADS, H_VAL, H_KEY), F32)],
        compiler_params=_params(("parallel", "arbitrary")),
        name="mixer_prompt",
    )(*operands, *weights)


def _ssd_sample(proj, dt, conv0, ssm0, conv_wb, ssd_w):
    bsz = ssm0.shape[0]
    bb = SAMPLE_BATCH_BLOCK
    rows = bb * (proj.shape[0] // bsz)
    col = lambda width, c: pl.BlockSpec((rows, width), lambda b: (b, c))
    seq = lambda *shape: pl.BlockSpec((bb,) + shape, lambda b: (b,) + (0,) * len(shape))
    weights = tuple(conv_wb) + tuple(ssd_w)
    return pl.pallas_call(
        _ssd_sample_kernel,
        grid=(bsz // bb,),
        in_specs=[col(D_MODEL, COL_X), col(D_MODEL, COL_BC), col(D_MODEL, COL_Z), col(LANES, 0),
                  seq(M_CONV - 1, M_CONV_DIM), seq(M_INNER, M_STATE)]
                 + [_resident(w.shape) for w in weights],
        out_specs=[col(D_MODEL, 0), seq(M_CONV - 1, M_CONV_DIM), seq(M_INNER, M_STATE)],
        out_shape=[jax.ShapeDtypeStruct((proj.shape[0], D_MODEL), F32),
                   jax.ShapeDtypeStruct((bsz, M_CONV - 1, M_CONV_DIM), F32),
                   jax.ShapeDtypeStruct((bsz, M_INNER, M_STATE), F32)],
        compiler_params=_params(("parallel",)),
        name="ssd_sample",
    )(proj, proj, proj, dt, conv0, ssm0, *weights)


def _hgrn_sample(proj, kk, ym, s0, hgrn_w):
    bsz = s0.shape[0]
    bb = SAMPLE_BATCH_BLOCK
    rows = bb * (proj.shape[0] // bsz)
    col = lambda c: pl.BlockSpec((rows, D_MODEL), lambda b: (b, c))
    state = pl.BlockSpec((bb, H_HEADS, H_KEY, H_VAL), lambda b: (b, 0, 0, 0))
    return pl.pallas_call(
        _hgrn_sample_kernel,
        grid=(bsz // bb,),
        in_specs=[col(COL_Q), col(COL_LOGF), col(0), col(COL_I), col(COL_OG), col(0), col(COL_GM),
                  col(COL_GH), state] + [_resident(w.shape) for w in hgrn_w],
        out_specs=[col(0), state],
        out_shape=[jax.ShapeDtypeStruct((proj.shape[0], D_MODEL), F32),
                   jax.ShapeDtypeStruct((bsz, H_HEADS, H_KEY, H_VAL), F32)],
        compiler_params=_params(("parallel",)),
        name="hgrn_sample",
    )(proj, proj, kk, proj, proj, ym, proj, proj, s0, *hgrn_w)


def _pad_lanes(v):
    return jnp.pad(v, (0, LANES - v.shape[0])).reshape(1, LANES)


def kernel(x_prompt, x_sample, state_conv, state_ssm, state_hgrn, ffn1_w_gate, ffn1_w_up, ffn1_w_down, ln1_g, ln1_b, w_in, conv_w, conv_b, dt_bias, a_log, d_skip, m_norm_w, w_m_out, hgrn_lb_param, h_norm_w, w_h_out, w_o, ln2_g, ln2_b, ffn2_w_gate, ffn2_w_up, ffn2_w_down, ln3_g, ln3_b):
    assert w_in.shape[0] == 1, "single trunk layer"
    bp, lp, _ = x_prompt.shape
    bs, ls, _ = x_sample.shape
    assert ls == SAMPLE_ROWS - M_CONV

    row = lambda v: v[0].reshape(1, -1)
    f1 = (ffn1_w_gate[0].astype(BF16), ffn1_w_up[0].astype(BF16), ffn1_w_down[0].astype(BF16))
    f2 = (ffn2_w_gate[0].astype(BF16), ffn2_w_up[0].astype(BF16), ffn2_w_down[0].astype(BF16))
    wi = w_in[0]
    dt_end = W_DT_AT + M_HEADS
    w_dt = jnp.pad(wi[:, W_DT_AT:dt_end], ((0, 0), (0, LANES - M_HEADS))).astype(BF16)
    proj_w = (wi[:, :W_DT_AT].astype(BF16), wi[:, dt_end:].astype(BF16), w_dt,
              _pad_lanes(dt_bias[0]), hgrn_lb_param)
    conv_wb = (conv_w[0], row(conv_b))
    ssd_w = (_pad_lanes(a_log[0]), jnp.repeat(d_skip[0], M_HEAD_DIM).reshape(1, M_INNER),
             row(m_norm_w), w_m_out[0].astype(BF16))
    hgrn_w = (row(h_norm_w), w_h_out[0].astype(BF16))
    merge_w = (w_o[0].astype(BF16), row(ln2_g), row(ln2_b)) + f2 + (row(ln3_g), row(ln3_b))

    x1 = _ffn_ln(x_prompt.reshape(bp * lp, D_MODEL), *f1, row(ln1_g), row(ln1_b))
    proj, dt, kk = _in_proj(x1, proj_w)
    proj3 = proj.reshape(bp, lp, PROJ_MAIN)
    mix, conv_p, ssm_p, hg_p = _mixer_prompt(proj3, dt.reshape(bp, lp, LANES),
                                             kk.reshape(bp, lp, D_MODEL), conv_wb, ssd_w, hgrn_w)
    y_prompt = _merge_ffn(x1, mix.reshape(-1, D_MODEL), *merge_w).reshape(bp, lp, D_MODEL)
    new_conv_p = conv_p[:, SUBLANES - (M_CONV - 1):, :][None]
    new_ssm_p = ssm_p.reshape(1, bp, M_HEADS, M_HEAD_DIM, M_STATE)
    new_hg_p = hg_p[None]

    x1s = _ffn_ln(x_sample.reshape(bs * ls, D_MODEL), *f1, row(ln1_g), row(ln1_b))
    projs, dts, kks = _in_proj(x1s, proj_w)
    yms, conv_s, ssm_s = _ssd_sample(projs, dts, state_conv[0],
                                     state_ssm[0].reshape(bs, M_INNER, M_STATE), conv_wb, ssd_w)
    mixs, hg_s = _hgrn_sample(projs, kks, yms, state_hgrn[0], hgrn_w)
    y_sample = _merge_ffn(x1s, mixs, *merge_w).reshape(bs, ls, D_MODEL)
    new_conv_s = conv_s[None]
    new_ssm_s = ssm_s.reshape(1, bs, M_HEADS, M_HEAD_DIM, M_STATE)
    new_hg_s = hg_s[None]

    return (y_prompt, y_sample, new_conv_p, new_ssm_p, new_hg_p, new_conv_s, new_ssm_s, new_hg_s)
```

```python
import jax
import jax.numpy as jnp
from jax import lax
from jax.experimental import pallas as pl
from jax.experimental.pallas import tpu as pltpu

F32 = jnp.float32
BF16 = jnp.bfloat16

D_MODEL = 1024
D_FF = 2816
M_HEADS = 16
M_HEAD_DIM = 64
M_GROUPS = 4
M_STATE = 128
M_INNER = M_HEADS * M_HEAD_DIM
M_CONV = 4
M_CONV_DIM = M_INNER + 2 * M_GROUPS * M_STATE
HEADS_PER_GROUP = M_HEADS // M_GROUPS
GROUP_COLS = HEADS_PER_GROUP * M_HEAD_DIM
H_HEADS = 8
H_KEY = 128
H_VAL = 128
H_CHUNK = 32
ALPHA = 2.0 ** 0.25
EPS = 1e-5

LANES = 128
SUBLANES = 8
BF16_SUBLANES = 16
VMEM_LIMIT = 56 * 1024 * 1024

FF_CHUNK = D_FF // 2
PROJ_MAIN = 9 * D_MODEL
COL_X, COL_Q, COL_GM = 0, 1, 2
COL_BC, COL_I, COL_GH = 3, 4, 5
COL_Z, COL_LOGF, COL_OG = 6, 7, 8
W_DT_AT = M_INNER + M_CONV_DIM
PROJ_PIECES = ((COL_X, ("a", M_INNER)), (COL_Q, ("b", 0)), (COL_GM, ("b", 4 * D_MODEL)),
               (COL_BC, ("a", 2 * M_INNER)), (COL_I, ("b", 2 * D_MODEL)), (COL_GH, ("b", 5 * D_MODEL)),
               (COL_Z, ("a", 0)), (COL_LOGF, ("b", D_MODEL)), (COL_OG, ("b", 3 * D_MODEL)))

SSD_CHUNK = 128
HGRN_BLOCK = 128
MIXER_TL = 256
MIXER_INTERLEAVE = (8, 16)
SAMPLE_ROWS = 8
SAMPLE_FIRST = M_CONV - 1
SAMPLE_BATCH_BLOCK = 8


def _sigmoid(x):
    return 1.0 / (1.0 + jnp.exp(-x))


def _silu(x):
    return x * _sigmoid(x)


def _softplus(x):
    return jnp.maximum(x, 0.0) + jnp.log(1.0 + jnp.exp(-jnp.abs(x)))


def _dot(a, b):
    return jnp.dot(a, b, preferred_element_type=F32)


def _dot_nt(a, b):
    return lax.dot_general(a, b, (((1,), (1,)), ((), ())), preferred_element_type=F32)


def _split3(x):
    hi = x.astype(BF16)
    r1 = x - hi.astype(F32)
    mid = r1.astype(BF16)
    lo = (r1 - mid.astype(F32)).astype(BF16)
    return hi, mid, lo


def _pad_rows(a):
    q = a.shape[0]
    if q == LANES:
        return a
    return jnp.concatenate([a, jnp.zeros((LANES - q, a.shape[1]), a.dtype)], axis=0)


def _chunk_cumsum(x, ch):
    rows, cols = x.shape
    if rows > ch and ch % BF16_SUBLANES == 0 and 3 * ch <= LANES:
        hi, mid, lo = _split3(x)
        row_i = lax.broadcasted_iota(jnp.int32, (ch, LANES), 0)
        col_i = lax.broadcasted_iota(jnp.int32, (ch, LANES), 1)
        sel = jnp.where((jnp.bitwise_and(col_i, ch - 1) <= row_i) & (col_i < 3 * ch), 1.0, 0.0)
        sel = sel.astype(BF16)
        pad = jnp.zeros((LANES - 3 * ch, cols), BF16)
        outs = []
        for c in range(rows // ch):
            rs = slice(c * ch, (c + 1) * ch)
            outs.append(_dot(sel, jnp.concatenate([hi[rs], mid[rs], lo[rs], pad], axis=0)))
        return jnp.concatenate(outs, axis=0)
    shift = ch.bit_length() - 1
    row_i = lax.broadcasted_iota(jnp.int32, (rows, LANES), 0)
    col_i = lax.broadcasted_iota(jnp.int32, (rows, LANES), 1)
    same = jnp.right_shift(row_i, shift) == jnp.right_shift(col_i, shift)
    sel = jnp.where(same & (col_i <= row_i), 1.0, 0.0).astype(BF16)
    hi, mid, lo = _split3(_pad_rows(x))
    return (_dot(sel, hi) + _dot(sel, mid)) + _dot(sel, lo)


def _layer_norm(y, g, b):
    mu = jnp.mean(y, axis=-1, keepdims=True)
    yc = y - mu
    var = jnp.mean(yc * yc, axis=-1, keepdims=True)
    return yc * lax.rsqrt(var + EPS) * g + b


def _swiglu(x, wg_ref, wu_ref, wd_ref):
    xb = x.astype(BF16)
    acc = None
    for c in range(D_FF // FF_CHUNK):
        sl = slice(c * FF_CHUNK, (c + 1) * FF_CHUNK)
        hg = _dot(xb, wg_ref[:, sl])
        hu = _dot(xb, wu_ref[:, sl])
        act = (_silu(hg) * hu).astype(BF16)
        part = _dot(act, wd_ref[sl, :])
        acc = part if acc is None else acc + part
    return acc


def _ffn_ln_kernel(x_ref, wg_ref, wu_ref, wd_ref, g_ref, b_ref, o_ref):
    x = x_ref[...]
    y = ALPHA * x + 0.5 * _swiglu(x, wg_ref, wu_ref, wd_ref)
    o_ref[...] = _layer_norm(y, g_ref[...], b_ref[...])


def _proj_piece(c, xb, wa_ref, wb_ref, lbp_ref):
    part, at = dict(PROJ_PIECES)[c]
    w_ref = wa_ref if part == "a" else wb_ref
    p = _dot(xb, w_ref[:, at:at + D_MODEL])
    if c in (COL_GM, COL_GH):
        return _sigmoid(p)
    if c in (COL_Z, COL_OG):
        return _silu(p)
    if c == COL_LOGF:
        return _hgrn_gates(p, lbp_ref)
    return p


def _proj_dt(xb, wdt_ref, dtb_ref):
    return _softplus(_dot(xb, wdt_ref[...]) + dtb_ref[...])


def _in_proj_kernel(x_ref, wa_ref, wb_ref, wdt_ref, dtb_ref, lbp_ref, o_ref, odt_ref, k_ref):
    xb = x_ref[...].astype(BF16)
    for c, _ in PROJ_PIECES:
        cols = slice(c * D_MODEL, (c + 1) * D_MODEL)
        p = _proj_piece(c, xb, wa_ref, wb_ref, lbp_ref)
        if c == COL_LOGF:
            o_ref[:, cols], k_ref[...] = p
        else:
            o_ref[:, cols] = p
    odt_ref[...] = _proj_dt(xb, wdt_ref, dtb_ref)


def _merge_ffn_kernel(x1_ref, mix_ref, wo_ref, g2_ref, b2_ref,
                      wg_ref, wu_ref, wd_ref, g3_ref, b3_ref, o_ref):
    x2 = _layer_norm(ALPHA * x1_ref[...] + _dot(mix_ref[...].astype(BF16), wo_ref[...]),
                     g2_ref[...], b2_ref[...])
    y = ALPHA * x2 + 0.5 * _swiglu(x2, wg_ref, wu_ref, wd_ref)
    o_ref[...] = _layer_norm(y, g3_ref[...], b3_ref[...])


def _ssd_streams(chunks, h_prev, a_row, stage_major):
    q = chunks[0][1].shape[0]
    n = len(chunks)
    row_i = lax.broadcasted_iota(jnp.int32, (q, LANES), 0)
    col_i = lax.broadcasted_iota(jnp.int32, (q, LANES), 1)
    causal = col_i <= row_i
    head_blk = jnp.right_shift(lax.broadcasted_iota(jnp.int32, (q, GROUP_COLS), 1),
                               M_HEAD_DIM.bit_length() - 1)
    groups = [slice(g * M_STATE, (g + 1) * M_STATE) for g in range(M_GROUPS)]
    rows_r = [slice(r * M_HEAD_DIM, (r + 1) * M_HEAD_DIM) for r in range(HEADS_PER_GROUP)]

    pre, ops = {}, {}

    def prepare(i):
        _, xm, bm, cm, dt = chunks[i]
        a_cum = _chunk_cumsum(dt * a_row, q)
        a_last = a_cum[q - 1:q, :]
        to_end = jnp.exp(a_last - a_cum) * dt
        pre[i] = dict(
            a_cum=a_cum, e_last=jnp.exp(a_last), e_cum=jnp.exp(a_cum),
            a_cum_t=jnp.transpose(_pad_rows(a_cum)),
            dt_t=jnp.transpose(_pad_rows(dt)),
            to_end_t=jnp.transpose(_pad_rows(to_end)))
        for g in range(M_GROUPS):
            xg = _pad_rows(xm[:, g * GROUP_COLS:(g + 1) * GROUP_COLS])
            ops[i, g] = dict(bg=_pad_rows(bm[:, groups[g]]).astype(BF16),
                             cg=cm[:, groups[g]].astype(BF16),
                             xg_b=xg.astype(BF16), xg_t=jnp.transpose(xg))

    def mm_cb(i, g):
        return _dot_nt(ops[i, g]["cg"], ops[i, g]["bg"])

    def mm_state(i, g, r):
        h = g * HEADS_PER_GROUP + r
        lhs = (ops[i, g]["xg_t"][rows_r[r], :] * pre[i]["to_end_t"][h:h + 1, :]).astype(BF16)
        return _dot(lhs, ops[i, g]["bg"])

    def mm_intra(i, g, r, cb):
        h = g * HEADS_PER_GROUP + r
        seg = pre[i]["a_cum"][:, h:h + 1] - pre[i]["a_cum_t"][h:h + 1, :]
        decay = jnp.where(causal, jnp.exp(jnp.where(causal, seg, 0.0)), 0.0)
        w = (cb * decay * pre[i]["dt_t"][h:h + 1, :]).astype(BF16)
        return _dot(w, ops[i, g]["xg_b"])

    def mm_inter(i, g, blocks):
        hg = jnp.concatenate(blocks[g * HEADS_PER_GROUP:(g + 1) * HEADS_PER_GROUP], axis=0)
        return _dot_nt(ops[i, g]["cg"], hg.astype(BF16))

    def combine(i, g, parts, y_inter):
        acc = jnp.zeros((q, GROUP_COLS), F32)
        for r in range(HEADS_PER_GROUP):
            h = g * HEADS_PER_GROUP + r
            acc = jnp.where(head_blk == r, parts[r] + y_inter * pre[i]["e_cum"][:, h:h + 1], acc)
        return acc

    cur = {seq: list(blocks) for seq, blocks in h_prev.items()}
    ys = []
    if not stage_major:
        for i, chunk in enumerate(chunks):
            prepare(i)
            enter = cur[chunk[0]]
            cols, new = [], []
            for g in range(M_GROUPS):
                cb = mm_cb(i, g)
                y_inter = mm_inter(i, g, enter)
                yield
                acc = jnp.zeros((q, GROUP_COLS), F32)
                for r in range(HEADS_PER_GROUP):
                    h = g * HEADS_PER_GROUP + r
                    y_h = mm_intra(i, g, r, cb) + y_inter * pre[i]["e_cum"][:, h:h + 1]
                    acc = jnp.where(head_blk == r, y_h, acc)
                    new.append(pre[i]["e_last"][:, h:h + 1] * enter[h] + mm_state(i, g, r))
                    yield
                cols.append(acc)
            cur[chunk[0]] = new
            ys.append(jnp.concatenate(cols, axis=1))
        return ys, cur
    for i in range(n):
        prepare(i)
    cb = {(i, g): mm_cb(i, g) for i in range(n) for g in range(M_GROUPS)}
    st = {(i, g, r): mm_state(i, g, r)
          for i in range(n) for g in range(M_GROUPS) for r in range(HEADS_PER_GROUP)}
    part = {(i, g, r): mm_intra(i, g, r, cb[i, g])
            for i in range(n) for g in range(M_GROUPS) for r in range(HEADS_PER_GROUP)}
    enter = []
    for i, chunk in enumerate(chunks):
        seq = chunk[0]
        enter.append(list(cur[seq]))
        cur[seq] = [pre[i]["e_last"][:, h:h + 1] * cur[seq][h]
                    + st[i, h // HEADS_PER_GROUP, h % HEADS_PER_GROUP] for h in range(M_HEADS)]
    for i in range(n):
        cols = []
        for g in range(M_GROUPS):
            y_inter = mm_inter(i, g, enter[i])
            cols.append(combine(i, g, [part[i, g, r] for r in range(HEADS_PER_GROUP)], y_inter))
        ys.append(jnp.concatenate(cols, axis=1))
    return ys, cur


def _ssd_gated_norm(y_ssd, xm, z_act, dskip, mnw):
    g = (y_ssd + dskip * xm) * z_act
    outs = []
    for k in range(M_GROUPS):
        gk = g[:, k * GROUP_COLS:(k + 1) * GROUP_COLS]
        outs.append(gk * lax.rsqrt(jnp.mean(gk * gk, axis=-1, keepdims=True) + EPS))
    return jnp.concatenate(outs, axis=1) * mnw


def _state_blocks(h):
    return [h[k * M_HEAD_DIM:(k + 1) * M_HEAD_DIM, :] for k in range(M_HEADS)]


def _conv_silu(u, prev8, cw, cbias):
    tl = u.shape[0]
    ext = jnp.concatenate([prev8, u], axis=0)
    conv = cbias + cw[M_CONV - 1:M_CONV, :] * u
    for j in range(1, M_CONV):
        shifted = pltpu.roll(ext, j, 0)[SUBLANES:SUBLANES + tl, :]
        conv = conv + cw[M_CONV - 1 - j:M_CONV - j, :] * shifted
    return _silu(conv)


def _drain(gen):
    try:
        while True:
            next(gen)
    except StopIteration as stop:
        return stop.value


class _Stream:
    def __init__(self, gen):
        self.gen, self.done, self.value = gen, False, None

    def step(self, n):
        for _ in range(n):
            if self.done:
                return
            try:
                next(self.gen)
            except StopIteration as stop:
                self.done, self.value = True, stop.value


def _proj_mixer_prompt_kernel(x1_ref, wa_ref, wb_ref, wdt_ref, dtb_ref, lbp_ref, cw_ref, cb_ref, alog_ref,
                              dskip_ref, mnw_ref, wmo_ref, hnw_ref, who_ref,
                              mix_ref, conv_out_ref, ssm_out_ref, s_out_ref, prev_scr, h_scr, s_scr):
    t = pl.program_id(1)
    tl = x1_ref.shape[1]

    @pl.when(t == 0)
    def _():
        prev_scr[...] = jnp.zeros_like(prev_scr)
        h_scr[...] = jnp.zeros_like(h_scr)
        s_scr[...] = jnp.zeros_like(s_scr)

    xb = x1_ref[0].astype(BF16)
    piece = lambda c: _proj_piece(c, xb, wa_ref, wb_ref, lbp_ref)
    x_raw, bc_raw = piece(COL_X), piece(COL_BC)
    dt = _proj_dt(xb, wdt_ref, dtb_ref)
    tail = jnp.concatenate([x_raw[tl - SUBLANES:tl, :], bc_raw[tl - SUBLANES:tl, :]], axis=1)
    xm = _conv_silu(x_raw, prev_scr[:, :M_INNER], cw_ref[:, :M_INNER], cb_ref[:, :M_INNER])
    bc = _conv_silu(bc_raw, prev_scr[:, M_INNER:], cw_ref[:, M_INNER:], cb_ref[:, M_INNER:])
    prev_scr[...] = tail
    bm = bc[:, :M_GROUPS * M_STATE]
    cm = bc[:, M_GROUPS * M_STATE:]
    a_row = -jnp.exp(alog_ref[...])
    chunks = []
    for c in range(tl // SSD_CHUNK):
        rs = slice(c * SSD_CHUNK, (c + 1) * SSD_CHUNK)
        chunks.append((0, xm[rs], bm[rs], cm[rs], dt[rs]))
    ssd = _Stream(_ssd_streams(chunks, {0: _state_blocks(h_scr[...])}, a_row, stage_major=False))

    got = {}

    def remaining_pieces():
        for c in (COL_LOGF, COL_Q, COL_I, COL_Z, COL_OG, COL_GM, COL_GH):
            got[c] = piece(c)
            yield

    proj = _Stream(remaining_pieces())
    n_ssd, n_hgrn = MIXER_INTERLEAVE
    while COL_I not in got:
        proj.step(1)
        ssd.step(n_ssd)
    (log_f, k), q, v = got[COL_LOGF], got[COL_Q], got[COL_I]
    blocks = []
    for b in range(tl // HGRN_BLOCK):
        rs = slice(b * HGRN_BLOCK, (b + 1) * HGRN_BLOCK)
        blocks.append((0, q[rs], log_f[rs], k[rs], v[rs]))
    hgrn = _Stream(_hgrn_streams(blocks, {0: [s_scr[h] for h in range(H_HEADS)]}, H_CHUNK))
    while not (proj.done and ssd.done and hgrn.done):
        proj.step(1)
        ssd.step(n_ssd)
        hgrn.step(n_hgrn)
    (ys, h_new), (outs, s_new) = ssd.value, hgrn.value
    for h in range(M_HEADS):
        h_scr[h * M_HEAD_DIM:(h + 1) * M_HEAD_DIM, :] = h_new[0][h]
    for h in range(H_HEADS):
        s_scr[h] = s_new[0][h]
    gn = _ssd_gated_norm(jnp.concatenate(ys, axis=0), xm, got[COL_Z], dskip_ref[...], mnw_ref[...])
    ym = _dot(gn.astype(BF16), wmo_ref[...])
    o = jnp.concatenate(outs, axis=0) * hnw_ref[...] * got[COL_OG]
    yh = _dot(o.astype(BF16), who_ref[...])
    mix_ref[0] = got[COL_GM] * ym + got[COL_GH] * yh

    @pl.when(t == pl.num_programs(1) - 1)
    def _():
        conv_out_ref[0] = tail
        ssm_out_ref[0] = h_scr[...]
        for h in range(H_HEADS):
            s_out_ref[0, h] = jnp.transpose(s_scr[h])


def _tile_rows(tokens, before):
    n, c = tokens.shape
    parts = [tokens, jnp.zeros((SAMPLE_ROWS - n - before, c), tokens.dtype)]
    if before:
        parts.insert(0, jnp.zeros((before, c), tokens.dtype))
    return jnp.concatenate(parts, axis=0)


def _ssd_sample_kernel(x_ref, bc_ref, z_ref, dt_ref, cs_ref, h0_ref, cw_ref, cb_ref, alog_ref,
                       dskip_ref, mnw_ref, wmo_ref, ym_ref, conv_out_ref, ssm_out_ref):
    bb = h0_ref.shape[0]
    n_tok = x_ref.shape[0] // bb
    a_row = -jnp.exp(alog_ref[...])
    chunks, zs, h_prev = [], [], {}
    for i in range(bb):
        rs = slice(i * n_tok, (i + 1) * n_tok)
        raw = jnp.concatenate([x_ref[rs, :], bc_ref[rs, :]], axis=1)
        u = jnp.concatenate([cs_ref[i], raw, jnp.zeros((1, M_CONV_DIM), F32)], axis=0)
        conv_out_ref[i] = u[n_tok:n_tok + M_CONV - 1, :]
        conv = cb_ref[...] + cw_ref[M_CONV - 1:M_CONV, :] * u
        for j in range(1, M_CONV):
            conv = conv + cw_ref[M_CONV - 1 - j:M_CONV - j, :] * pltpu.roll(u, j, 0)
        xbc = _silu(conv)
        xm = xbc[:, :M_INNER]
        bm = xbc[:, M_INNER:M_INNER + M_GROUPS * M_STATE]
        cm = xbc[:, M_INNER + M_GROUPS * M_STATE:]
        chunks.append((i, xm, bm, cm, _tile_rows(dt_ref[rs, :], SAMPLE_FIRST)))
        zs.append(_tile_rows(z_ref[rs, :], SAMPLE_FIRST))
        h_prev[i] = _state_blocks(h0_ref[i])
    ys, h_new = _drain(_ssd_streams(chunks, h_prev, a_row, stage_major=True))
    gns = []
    for i in range(bb):
        for h in range(M_HEADS):
            ssm_out_ref[i, h * M_HEAD_DIM:(h + 1) * M_HEAD_DIM, :] = h_new[i][h]
        gns.append(_ssd_gated_norm(ys[i], chunks[i][1], zs[i], dskip_ref[...], mnw_ref[...]))
    ym = _dot(jnp.concatenate(gns, axis=0).astype(BF16), wmo_ref[...])
    for i in range(bb):
        ym_ref[i * n_tok:(i + 1) * n_tok, :] = ym[i * SAMPLE_ROWS + SAMPLE_FIRST:
                                                  i * SAMPLE_ROWS + SAMPLE_FIRST + n_tok, :]


def _hgrn_gates(f_raw, lbp_ref):
    p0 = lbp_ref[0:1, :]
    p1 = lbp_ref[1:2, :]
    m = jnp.maximum(p0, p1)
    e0 = jnp.exp(p0 - m)
    e1 = jnp.exp(p1 - m)
    lb = e0 / (e0 + e1)
    log_f = jnp.log(lb + (1.0 - lb) * _sigmoid(f_raw))
    k = (1.0 - lb) * _sigmoid(-f_raw)
    return log_f, k


def _hgrn_streams(blocks, s_prev, ch):
    rows = blocks[0][1].shape[0]
    n = len(blocks)
    shift = ch.bit_length() - 1
    n_chunks = rows // ch
    row_i = lax.broadcasted_iota(jnp.int32, (rows, LANES), 0)
    col_i = lax.broadcasted_iota(jnp.int32, (rows, LANES), 1)
    causal = (jnp.right_shift(row_i, shift) == jnp.right_shift(col_i, shift)) & (col_i <= row_i)
    chunk_t = jnp.right_shift(lax.broadcasted_iota(jnp.int32, (LANES, LANES), 1), shift)
    heads = [slice(h * H_KEY, (h + 1) * H_KEY) for h in range(H_HEADS)]
    chunks = [slice(c * ch, (c + 1) * ch) for c in range(n_chunks)]
    items = [(i, h) for i in range(n) for h in range(H_HEADS)]

    pre = []
    for (_, q, log_f, k, v) in blocks:
        b_cum = _chunk_cumsum(log_f, ch)
        lasts = [b_cum[c * ch + ch - 1:c * ch + ch, :] for c in range(n_chunks)]
        b_last = jnp.concatenate([jnp.broadcast_to(l, (ch, l.shape[1])) for l in lasts], axis=0)
        pre.append(dict(qd=q * jnp.exp(b_cum), kd=k * jnp.exp(-b_cum),
                        k_end=k * jnp.exp(b_last - b_cum), lasts=lasts, v=v))
    qh = {(i, h): pre[i]["qd"][:, heads[h]].astype(BF16) for (i, h) in items}
    vh = {(i, h): _pad_rows(pre[i]["v"][:, heads[h]]) for (i, h) in items}
    sc, ds, y_intra, y_inter = {}, {}, {}, {}
    for (i, h) in items:
        sc[i, h] = _dot_nt(qh[i, h], _pad_rows(pre[i]["kd"][:, heads[h]]).astype(BF16))
        yield
    for (i, h) in items:
        v_t = jnp.transpose(vh[i, h])
        ke = _pad_rows(pre[i]["k_end"][:, heads[h]]).astype(BF16)
        ds[i, h] = []
        for c in range(n_chunks):
            v_tc = v_t if n_chunks == 1 else jnp.where(chunk_t == c, v_t, 0.0)
            ds[i, h].append(_dot(v_tc.astype(BF16), ke))
            yield
    for (i, h) in items:
        y_intra[i, h] = _dot(jnp.where(causal, sc[i, h], 0.0).astype(BF16), vh[i, h].astype(BF16))
        yield
    cur = {seq: list(states) for seq, states in s_prev.items()}
    enter = {}
    for i, blk in enumerate(blocks):
        seq = blk[0]
        for h in range(H_HEADS):
            s = cur[seq][h]
            per_chunk = []
            for c in range(n_chunks):
                per_chunk.append(s)
                s = jnp.exp(pre[i]["lasts"][c][:, heads[h]]) * s + ds[i, h][c]
            enter[i, h] = per_chunk
            cur[seq][h] = s
    for (i, h) in items:
        y_inter[i, h] = []
        for c in range(n_chunks):
            y_inter[i, h].append(_dot_nt(qh[i, h][chunks[c]], enter[i, h][c].astype(BF16)))
            yield
    outs = []
    for i in range(n):
        cols = []
        for h in range(H_HEADS):
            pieces = [y_intra[i, h][chunks[c]] + y_inter[i, h][c] for c in range(n_chunks)]
            o_h = pieces[0] if n_chunks == 1 else jnp.concatenate(pieces, axis=0)
            cols.append(o_h * lax.rsqrt(jnp.mean(o_h * o_h, axis=-1, keepdims=True) + EPS))
        outs.append(jnp.concatenate(cols, axis=1))
    return outs, cur


def _hgrn_sample_kernel(q_ref, logf_ref, k_ref, i_ref, og_ref, ym_ref, gm_ref, gh_ref, s0_ref,
                        hnw_ref, who_ref, mix_ref, s_out_ref):
    bb = s0_ref.shape[0]
    n_tok = q_ref.shape[0] // bb
    rows = [slice(i * n_tok, (i + 1) * n_tok) for i in range(bb)]
    blocks, s_prev = [], {}
    for i in range(bb):
        blocks.append((i, _tile_rows(q_ref[rows[i], :], 0), _tile_rows(logf_ref[rows[i], :], 0),
                       _tile_rows(k_ref[rows[i], :], 0), _tile_rows(i_ref[rows[i], :], 0)))
        s_prev[i] = [jnp.transpose(s0_ref[i, h]) for h in range(H_HEADS)]
    outs, s_new = _drain(_hgrn_streams(blocks, s_prev, SAMPLE_ROWS))
    os_ = []
    for i in range(bb):
        for h in range(H_HEADS):
            s_out_ref[i, h] = jnp.transpose(s_new[i][h])
        os_.append(outs[i] * hnw_ref[...] * _tile_rows(og_ref[rows[i], :], 0))
    yh = _dot(jnp.concatenate(os_, axis=0).astype(BF16), who_ref[...])
    for i in range(bb):
        mix_ref[rows[i], :] = (gm_ref[rows[i], :] * ym_ref[rows[i], :]
                               + gh_ref[rows[i], :] * yh[i * SAMPLE_ROWS:i * SAMPLE_ROWS + n_tok, :])


def _resident(shape):
    nd = len(shape)
    return pl.BlockSpec(shape, lambda *_: (0,) * nd, pipeline_mode=pl.Buffered(1))


def _params(semantics):
    return pltpu.CompilerParams(dimension_semantics=semantics, vmem_limit_bytes=VMEM_LIMIT)


def _row_tile(n_rows, want):
    tm = min(want, n_rows)
    assert n_rows % tm == 0
    return tm


def _ffn_ln(x, wg, wu, wd, g, b):
    n = x.shape[0]
    tm = _row_tile(n, 512)
    row = pl.BlockSpec((tm, D_MODEL), lambda i: (i, 0))
    return pl.pallas_call(
        _ffn_ln_kernel,
        grid=(n // tm,),
        in_specs=[row, _resident(wg.shape), _resident(wu.shape), _resident(wd.shape),
                  _resident(g.shape), _resident(b.shape)],
        out_specs=row,
        out_shape=jax.ShapeDtypeStruct((n, D_MODEL), F32),
        compiler_params=_params(("parallel",)),
        name="ffn_ln",
    )(x, wg, wu, wd, g, b)


def _in_proj(x1, proj_w):
    n = x1.shape[0]
    tm = _row_tile(n, 256)
    row = lambda width: pl.BlockSpec((tm, width), lambda i: (i, 0))
    return pl.pallas_call(
        _in_proj_kernel,
        grid=(n // tm,),
        in_specs=[row(D_MODEL)] + [_resident(w.shape) for w in proj_w],
        out_specs=[row(PROJ_MAIN), row(LANES), row(D_MODEL)],
        out_shape=[jax.ShapeDtypeStruct((n, PROJ_MAIN), F32),
                   jax.ShapeDtypeStruct((n, LANES), F32),
                   jax.ShapeDtypeStruct((n, D_MODEL), F32)],
        compiler_params=_params(("parallel",)),
        name="in_proj",
    )(x1, *proj_w)


def _merge_ffn(x1, mix, *merge_w):
    n = x1.shape[0]
    tm = _row_tile(n, 512)
    row = pl.BlockSpec((tm, D_MODEL), lambda i: (i, 0))
    return pl.pallas_call(
        _merge_ffn_kernel,
        grid=(n // tm,),
        in_specs=[row, row] + [_resident(w.shape) for w in merge_w],
        out_specs=row,
        out_shape=jax.ShapeDtypeStruct((n, D_MODEL), F32),
        compiler_params=_params(("parallel",)),
        name="merge_ffn",
    )(x1, mix, *merge_w)


def _proj_mixer_prompt(x1_3, proj_w, conv_wb, ssd_w, hgrn_w):
    bsz, length, _ = x1_3.shape
    tl = MIXER_TL
    rows = pl.BlockSpec((1, tl, D_MODEL), lambda b, t: (b, t, 0))
    once = lambda *shape: pl.BlockSpec((1,) + shape, lambda b, t: (b,) + (0,) * len(shape))
    weights = tuple(proj_w) + tuple(conv_wb) + tuple(ssd_w) + tuple(hgrn_w)
    return pl.pallas_call(
        _proj_mixer_prompt_kernel,
        grid=(bsz, length // tl),
        in_specs=[rows] + [_resident(w.shape) for w in weights],
        out_specs=[rows, once(SUBLANES, M_CONV_DIM), once(M_INNER, M_STATE),
                   once(H_HEADS, H_KEY, H_VAL)],
        out_shape=[jax.ShapeDtypeStruct((bsz, length, D_MODEL), F32),
                   jax.ShapeDtypeStruct((bsz, SUBLANES, M_CONV_DIM), F32),
                   jax.ShapeDtypeStruct((bsz, M_INNER, M_STATE), F32),
                   jax.ShapeDtypeStruct((bsz, H_HEADS, H_KEY, H_VAL), F32)],
        scratch_shapes=[pltpu.VMEM((SUBLANES, M_CONV_DIM), F32),
                        pltpu.VMEM((M_INNER, M_STATE), F32),
                        pltpu.VMEM((H_HEADS, H_VAL, H_KEY), F32)],
        compiler_params=_params(("parallel", "arbitrary")),
        name="proj_mixer_prompt",
    )(x1_3, *weights)


def _ssd_sample(proj, dt, conv0, ssm0, conv_wb, ssd_w):
    bsz = ssm0.shape[0]
    bb = SAMPLE_BATCH_BLOCK
    rows = bb * (proj.shape[0] // bsz)
    col = lambda width, c: pl.BlockSpec((rows, width), lambda b: (b, c))
    seq = lambda *shape: pl.BlockSpec((bb,) + shape, lambda b: (b,) + (0,) * len(shape))
    weights = tuple(conv_wb) + tuple(ssd_w)
    return pl.pallas_call(
        _ssd_sample_kernel,
        grid=(bsz // bb,),
        in_specs=[col(D_MODEL, COL_X), col(D_MODEL, COL_BC), col(D_MODEL, COL_Z), col(LANES, 0),
                  seq(M_CONV - 1, M_CONV_DIM), seq(M_INNER, M_STATE)]
                 + [_resident(w.shape) for w in weights],
        out_specs=[col(D_MODEL, 0), seq(M_CONV - 1, M_CONV_DIM), seq(M_INNER, M_STATE)],
        out_shape=[jax.ShapeDtypeStruct((proj.shape[0], D_MODEL), F32),
                   jax.ShapeDtypeStruct((bsz, M_CONV - 1, M_CONV_DIM), F32),
                   jax.ShapeDtypeStruct((bsz, M_INNER, M_STATE), F32)],
        compiler_params=_params(("parallel",)),
        name="ssd_sample",
    )(proj, proj, proj, dt, conv0, ssm0, *weights)


def _hgrn_sample(proj, kk, ym, s0, hgrn_w):
    bsz = s0.shape[0]
    bb = SAMPLE_BATCH_BLOCK
    rows = bb * (proj.shape[0] // bsz)
    col = lambda c: pl.BlockSpec((rows, D_MODEL), lambda b: (b, c))
    state = pl.BlockSpec((bb, H_HEADS, H_KEY, H_VAL), lambda b: (b, 0, 0, 0))
    return pl.pallas_call(
        _hgrn_sample_kernel,
        grid=(bsz // bb,),
        in_specs=[col(COL_Q), col(COL_LOGF), col(0), col(COL_I), col(COL_OG), col(0), col(COL_GM),
                  col(COL_GH), state] + [_resident(w.shape) for w in hgrn_w],
        out_specs=[col(0), state],
        out_shape=[jax.ShapeDtypeStruct((proj.shape[0], D_MODEL), F32),
                   jax.ShapeDtypeStruct((bsz, H_HEADS, H_KEY, H_VAL), F32)],
        compiler_params=_params(("parallel",)),
        name="hgrn_sample",
    )(proj, proj, kk, proj, proj, ym, proj, proj, s0, *hgrn_w)


def _pad_lanes(v):
    return jnp.pad(v, (0, LANES - v.shape[0])).reshape(1, LANES)


def kernel(x_prompt, x_sample, state_conv, state_ssm, state_hgrn, ffn1_w_gate, ffn1_w_up, ffn1_w_down, ln1_g, ln1_b, w_in, conv_w, conv_b, dt_bias, a_log, d_skip, m_norm_w, w_m_out, hgrn_lb_param, h_norm_w, w_h_out, w_o, ln2_g, ln2_b, ffn2_w_gate, ffn2_w_up, ffn2_w_down, ln3_g, ln3_b):
    assert w_in.shape[0] == 1, "single trunk layer"
    bp, lp, _ = x_prompt.shape
    bs, ls, _ = x_sample.shape
    assert ls == SAMPLE_ROWS - M_CONV

    row = lambda v: v[0].reshape(1, -1)
    f1 = (ffn1_w_gate[0].astype(BF16), ffn1_w_up[0].astype(BF16), ffn1_w_down[0].astype(BF16))
    f2 = (ffn2_w_gate[0].astype(BF16), ffn2_w_up[0].astype(BF16), ffn2_w_down[0].astype(BF16))
    wi = w_in[0]
    dt_end = W_DT_AT + M_HEADS
    w_dt = jnp.pad(wi[:, W_DT_AT:dt_end], ((0, 0), (0, LANES - M_HEADS))).astype(BF16)
    proj_w = (wi[:, :W_DT_AT].astype(BF16), wi[:, dt_end:].astype(BF16), w_dt,
              _pad_lanes(dt_bias[0]), hgrn_lb_param)
    conv_wb = (conv_w[0], row(conv_b))
    ssd_w = (_pad_lanes(a_log[0]), jnp.repeat(d_skip[0], M_HEAD_DIM).reshape(1, M_INNER),
             row(m_norm_w), w_m_out[0].astype(BF16))
    hgrn_w = (row(h_norm_w), w_h_out[0].astype(BF16))
    merge_w = (w_o[0].astype(BF16), row(ln2_g), row(ln2_b)) + f2 + (row(ln3_g), row(ln3_b))

    x1 = _ffn_ln(x_prompt.reshape(bp * lp, D_MODEL), *f1, row(ln1_g), row(ln1_b))
    mix, conv_p, ssm_p, hg_p = _proj_mixer_prompt(x1.reshape(bp, lp, D_MODEL), proj_w, conv_wb,
                                                  ssd_w, hgrn_w)
    y_prompt = _merge_ffn(x1, mix.reshape(-1, D_MODEL), *merge_w).reshape(bp, lp, D_MODEL)
    new_conv_p = conv_p[:, SUBLANES - (M_CONV - 1):, :][None]
    new_ssm_p = ssm_p.reshape(1, bp, M_HEADS, M_HEAD_DIM, M_STATE)
    new_hg_p = hg_p[None]

    x1s = _ffn_ln(x_sample.reshape(bs * ls, D_MODEL), *f1, row(ln1_g), row(ln1_b))
    projs, dts, kks = _in_proj(x1s, proj_w)
    yms, conv_s, ssm_s = _ssd_sample(projs, dts, state_conv[0],
                                     state_ssm[0].reshape(bs, M_INNER, M_STATE), conv_wb, ssd_w)
    mixs, hg_s = _hgrn_sample(projs, kks, yms, state_hgrn[0], hgrn_w)
    y_sample = _merge_ffn(x1s, mixs, *merge_w).reshape(bs, ls, D_MODEL)
    new_conv_s = conv_s[None]
    new_ssm_s = ssm_s.reshape(1, bs, M_HEADS, M_HEAD_DIM, M_STATE)
    new_hg_s = hg_s[None]

    return (y_prompt, y_sample, new_conv_p, new_ssm_p, new_hg_p, new_conv_s, new_ssm_s, new_hg_s)
```

```python
import jax
import jax.numpy as jnp
from jax import lax
from jax.experimental import pallas as pl
from jax.experimental.pallas import tpu as pltpu

F32 = jnp.float32
BF16 = jnp.bfloat16

D_MODEL = 1024
D_FF = 2816
M_HEADS = 16
M_HEAD_DIM = 64
M_GROUPS = 4
M_STATE = 128
M_INNER = M_HEADS * M_HEAD_DIM
M_CONV = 4
M_CONV_DIM = M_INNER + 2 * M_GROUPS * M_STATE
HEADS_PER_GROUP = M_HEADS // M_GROUPS
GROUP_COLS = HEADS_PER_GROUP * M_HEAD_DIM
H_HEADS = 8
H_KEY = 128
H_VAL = 128
H_CHUNK = 32
ALPHA = 2.0 ** 0.25
EPS = 1e-5

LANES = 128
SUBLANES = 8
VMEM_LIMIT = 56 * 1024 * 1024

FF_CHUNK = D_FF // 2
PROJ_MAIN = 9 * D_MODEL
COL_X, COL_Q, COL_GM = 0, 1, 2
COL_BC, COL_I, COL_GH = 3, 4, 5
COL_Z, COL_LOGF, COL_OG = 6, 7, 8
W_DT_AT = M_INNER + M_CONV_DIM
PROJ_PIECES = ((COL_X, ("a", M_INNER)), (COL_Q, ("b", 0)), (COL_GM, ("b", 4 * D_MODEL)),
               (COL_BC, ("a", 2 * M_INNER)), (COL_I, ("b", 2 * D_MODEL)), (COL_GH, ("b", 5 * D_MODEL)),
               (COL_Z, ("a", 0)), (COL_LOGF, ("b", D_MODEL)), (COL_OG, ("b", 3 * D_MODEL)))

SSD_CHUNK = 128
HGRN_BLOCK = 128
MIXER_TL = 256
PROJ_SUB_COLS = 1024
MIXER_INTERLEAVE = (8, 16)
SAMPLE_ROWS = 8
SAMPLE_FIRST = M_CONV - 1
SAMPLE_BATCH_BLOCK = 8


def _sigmoid(x):
    return 1.0 / (1.0 + jnp.exp(-x))


def _silu(x):
    return x * _sigmoid(x)


def _softplus(x):
    return jnp.maximum(x, 0.0) + jnp.log(1.0 + jnp.exp(-jnp.abs(x)))


def _dot(a, b):
    return jnp.dot(a, b, preferred_element_type=F32)


def _dot_nt(a, b):
    return lax.dot_general(a, b, (((1,), (1,)), ((), ())), preferred_element_type=F32)


def _pad_rows(a):
    q = a.shape[0]
    if q == LANES:
        return a
    return jnp.concatenate([a, jnp.zeros((LANES - q, a.shape[1]), a.dtype)], axis=0)


def _chunk_cumsum(x, ch):
    rows, cols = x.shape
    g = rows // SUBLANES
    y = x.reshape(g, SUBLANES, cols)
    sub = lax.broadcasted_iota(jnp.int32, (g, SUBLANES, cols), 1)
    s = 1
    while s < SUBLANES:
        y = y + jnp.where(sub >= s, pltpu.roll(y, s, 1), 0.0)
        s *= 2
    per = ch // SUBLANES
    if per > 1:
        y4 = y.reshape(g // per, per, SUBLANES, cols)
        carry, outs = None, []
        for j in range(per):
            yj = y4[:, j]
            outs.append(yj if carry is None else yj + carry)
            tot = jnp.broadcast_to(yj[:, SUBLANES - 1:SUBLANES, :], yj.shape)
            carry = tot if carry is None else carry + tot
        y = jnp.stack(outs, axis=1).reshape(g, SUBLANES, cols)
    return y.reshape(rows, cols)


def _layer_norm(y, g, b):
    mu = jnp.mean(y, axis=-1, keepdims=True)
    yc = y - mu
    var = jnp.mean(yc * yc, axis=-1, keepdims=True)
    return yc * lax.rsqrt(var + EPS) * g + b


def _swiglu(x, wg_ref, wu_ref, wd_ref):
    xb = x.astype(wg_ref.dtype)
    acc = None
    for c in range(D_FF // FF_CHUNK):
        sl = slice(c * FF_CHUNK, (c + 1) * FF_CHUNK)
        hg = _dot(xb, wg_ref[:, sl])
        hu = _dot(xb, wu_ref[:, sl])
        act = (_silu(hg) * hu).astype(wd_ref.dtype)
        part = _dot(act, wd_ref[sl, :])
        acc = part if acc is None else acc + part
    return acc


def _ffn_ln_kernel(x_ref, wg_ref, wu_ref, wd_ref, g_ref, b_ref, o_ref):
    x = x_ref[...]
    y = ALPHA * x + 0.5 * _swiglu(x, wg_ref, wu_ref, wd_ref)
    o_ref[...] = _layer_norm(y, g_ref[...], b_ref[...])


def _proj_piece(c, xb, wa_ref, wb_ref, lbp_ref, sub=slice(0, D_MODEL)):
    part, at = dict(PROJ_PIECES)[c]
    w_ref = wa_ref if part == "a" else wb_ref
    p = _dot(xb, w_ref[:, at + sub.start:at + sub.stop])
    if c in (COL_GM, COL_GH):
        return _sigmoid(p)
    if c in (COL_Z, COL_OG):
        return _silu(p)
    if c == COL_LOGF:
        return _hgrn_gates(p, lbp_ref, sub)
    return p


def _proj_dt(xb, wdt_ref, dtb_ref):
    return _softplus(_dot(xb, wdt_ref[...]) + dtb_ref[...])


def _in_proj_kernel(x_ref, wa_ref, wb_ref, wdt_ref, dtb_ref, lbp_ref, o_ref, odt_ref, k_ref):
    xb = x_ref[...].astype(BF16)
    for c, _ in PROJ_PIECES:
        cols = slice(c * D_MODEL, (c + 1) * D_MODEL)
        p = _proj_piece(c, xb, wa_ref, wb_ref, lbp_ref)
        if c == COL_LOGF:
            o_ref[:, cols], k_ref[...] = p
        else:
            o_ref[:, cols] = p
    odt_ref[...] = _proj_dt(xb, wdt_ref, dtb_ref)


def _merge_ffn_kernel(x1_ref, mix_ref, wo_ref, g2_ref, b2_ref,
                      wg_ref, wu_ref, wd_ref, g3_ref, b3_ref, o_ref):
    x2 = _layer_norm(ALPHA * x1_ref[...] + _dot(mix_ref[...], wo_ref[...]),
                     g2_ref[...], b2_ref[...])
    y = ALPHA * x2 + 0.5 * _swiglu(x2, wg_ref, wu_ref, wd_ref)
    o_ref[...] = _layer_norm(y, g3_ref[...], b3_ref[...])


def _ssd_streams(chunks, h_prev, a_row, stage_major):
    q = chunks[0][1].shape[0]
    n = len(chunks)
    row_i = lax.broadcasted_iota(jnp.int32, (q, LANES), 0)
    col_i = lax.broadcasted_iota(jnp.int32, (q, LANES), 1)
    causal = col_i <= row_i
    head_blk = jnp.right_shift(lax.broadcasted_iota(jnp.int32, (q, GROUP_COLS), 1),
                               M_HEAD_DIM.bit_length() - 1)
    groups = [slice(g * M_STATE, (g + 1) * M_STATE) for g in range(M_GROUPS)]
    rows_r = [slice(r * M_HEAD_DIM, (r + 1) * M_HEAD_DIM) for r in range(HEADS_PER_GROUP)]

    pre, ops = {}, {}

    def prepare(i):
        _, xm, bm, cm, dt = chunks[i]
        a_cum = _chunk_cumsum(dt * a_row, q)
        a_last = a_cum[q - 1:q, :]
        to_end = jnp.exp(a_last - a_cum) * dt
        pre[i] = dict(
            a_cum=a_cum, e_last=jnp.exp(a_last), e_cum=jnp.exp(a_cum),
            a_cum_t=jnp.transpose(_pad_rows(a_cum)),
            dt_t=jnp.transpose(_pad_rows(dt)),
            to_end_t=jnp.transpose(_pad_rows(to_end)))
        for g in range(M_GROUPS):
            xg = _pad_rows(xm[:, g * GROUP_COLS:(g + 1) * GROUP_COLS])
            ops[i, g] = dict(bg=_pad_rows(bm[:, groups[g]]).astype(BF16),
                             cg=cm[:, groups[g]].astype(BF16),
                             xg_b=xg.astype(BF16), xg_t=jnp.transpose(xg))

    def mm_cb(i, g):
        return _dot_nt(ops[i, g]["cg"], ops[i, g]["bg"])

    def mm_state(i, g, r):
        h = g * HEADS_PER_GROUP + r
        lhs = (ops[i, g]["xg_t"][rows_r[r], :] * pre[i]["to_end_t"][h:h + 1, :]).astype(BF16)
        return _dot(lhs, ops[i, g]["bg"])

    def mm_intra(i, g, r, cb):
        h = g * HEADS_PER_GROUP + r
        seg = pre[i]["a_cum"][:, h:h + 1] - pre[i]["a_cum_t"][h:h + 1, :]
        decay = jnp.where(causal, jnp.exp(jnp.where(causal, seg, 0.0)), 0.0)
        w = (cb * decay * pre[i]["dt_t"][h:h + 1, :]).astype(BF16)
        return _dot(w, ops[i, g]["xg_b"])

    def mm_inter(i, g, blocks):
        hg = jnp.concatenate(blocks[g * HEADS_PER_GROUP:(g + 1) * HEADS_PER_GROUP], axis=0)
        return _dot_nt(ops[i, g]["cg"], hg.astype(BF16))

    def combine(i, g, parts, y_inter):
        acc = jnp.zeros((q, GROUP_COLS), F32)
        for r in range(HEADS_PER_GROUP):
            h = g * HEADS_PER_GROUP + r
            acc = jnp.where(head_blk == r, parts[r] + y_inter * pre[i]["e_cum"][:, h:h + 1], acc)
        return acc

    cur = {seq: list(blocks) for seq, blocks in h_prev.items()}
    ys = []
    if not stage_major:
        for i, chunk in enumerate(chunks):
            prepare(i)
            enter = cur[chunk[0]]
            cols, new = [], []
            for g in range(M_GROUPS):
                cb = mm_cb(i, g)
                y_inter = mm_inter(i, g, enter)
                yield
                acc = jnp.zeros((q, GROUP_COLS), F32)
                for r in range(HEADS_PER_GROUP):
                    h = g * HEADS_PER_GROUP + r
                    y_h = mm_intra(i, g, r, cb) + y_inter * pre[i]["e_cum"][:, h:h + 1]
                    acc = jnp.where(head_blk == r, y_h, acc)
                    new.append(pre[i]["e_last"][:, h:h + 1] * enter[h] + mm_state(i, g, r))
                    yield
                cols.append(acc)
            cur[chunk[0]] = new
            ys.append(jnp.concatenate(cols, axis=1))
        return ys, cur
    for i in range(n):
        prepare(i)
    cb = {(i, g): mm_cb(i, g) for i in range(n) for g in range(M_GROUPS)}
    st = {(i, g, r): mm_state(i, g, r)
          for i in range(n) for g in range(M_GROUPS) for r in range(HEADS_PER_GROUP)}
    part = {(i, g, r): mm_intra(i, g, r, cb[i, g])
            for i in range(n) for g in range(M_GROUPS) for r in range(HEADS_PER_GROUP)}
    enter = []
    for i, chunk in enumerate(chunks):
        seq = chunk[0]
        enter.append(list(cur[seq]))
        cur[seq] = [pre[i]["e_last"][:, h:h + 1] * cur[seq][h]
                    + st[i, h // HEADS_PER_GROUP, h % HEADS_PER_GROUP] for h in range(M_HEADS)]
    for i in range(n):
        cols = []
        for g in range(M_GROUPS):
            y_inter = mm_inter(i, g, enter[i])
            cols.append(combine(i, g, [part[i, g, r] for r in range(HEADS_PER_GROUP)], y_inter))
        ys.append(jnp.concatenate(cols, axis=1))
    return ys, cur


def _ssd_gated_norm(y_ssd, xm, z_act, dskip, mnw):
    g = (y_ssd + dskip * xm) * z_act
    outs = []
    for k in range(M_GROUPS):
        gk = g[:, k * GROUP_COLS:(k + 1) * GROUP_COLS]
        outs.append(gk * lax.rsqrt(jnp.mean(gk * gk, axis=-1, keepdims=True) + EPS))
    return jnp.concatenate(outs, axis=1) * mnw


def _state_blocks(h):
    return [h[k * M_HEAD_DIM:(k + 1) * M_HEAD_DIM, :] for k in range(M_HEADS)]


def _conv_silu(u, prev8, cw, cbias):
    tl = u.shape[0]
    ext = jnp.concatenate([prev8, u], axis=0)
    conv = cbias + cw[M_CONV - 1:M_CONV, :] * u
    for j in range(1, M_CONV):
        shifted = pltpu.roll(ext, j, 0)[SUBLANES:SUBLANES + tl, :]
        conv = conv + cw[M_CONV - 1 - j:M_CONV - j, :] * shifted
    return _silu(conv)


def _drain(gen):
    try:
        while True:
            next(gen)
    except StopIteration as stop:
        return stop.value


class _Stream:
    def __init__(self, gen):
        self.gen, self.done, self.value = gen, False, None

    def step(self, n):
        for _ in range(n):
            if self.done:
                return
            try:
                next(self.gen)
            except StopIteration as stop:
                self.done, self.value = True, stop.value


def _proj_mixer_prompt_kernel(x1_ref, wa_ref, wb_ref, wdt_ref, dtb_ref, lbp_ref, cw_ref, cb_ref, alog_ref,
                              dskip_ref, mnw_ref, wmo_ref, hnw_ref, who_ref,
                              mix_ref, conv_out_ref, ssm_out_ref, s_out_ref, prev_scr, h_scr, s_scr):
    t = pl.program_id(1)
    tl = x1_ref.shape[1]

    @pl.when(t == 0)
    def _():
        prev_scr[...] = jnp.zeros_like(prev_scr)
        h_scr[...] = jnp.zeros_like(h_scr)
        s_scr[...] = jnp.zeros_like(s_scr)

    xb = x1_ref[0].astype(BF16)
    piece = lambda c: _proj_piece(c, xb, wa_ref, wb_ref, lbp_ref)
    x_raw, bc_raw = piece(COL_X), piece(COL_BC)
    dt = _proj_dt(xb, wdt_ref, dtb_ref)
    tail = jnp.concatenate([x_raw[tl - SUBLANES:tl, :], bc_raw[tl - SUBLANES:tl, :]], axis=1)
    xm = _conv_silu(x_raw, prev_scr[:, :M_INNER], cw_ref[:, :M_INNER], cb_ref[:, :M_INNER])
    bc = _conv_silu(bc_raw, prev_scr[:, M_INNER:], cw_ref[:, M_INNER:], cb_ref[:, M_INNER:])
    prev_scr[...] = tail
    bm = bc[:, :M_GROUPS * M_STATE]
    cm = bc[:, M_GROUPS * M_STATE:]
    a_row = -jnp.exp(alog_ref[...])
    chunks = []
    for c in range(tl // SSD_CHUNK):
        rs = slice(c * SSD_CHUNK, (c + 1) * SSD_CHUNK)
        chunks.append((0, xm[rs], bm[rs], cm[rs], dt[rs]))
    got = {}

    def remaining_pieces():
        cat = lambda parts: jnp.concatenate(parts, axis=1)
        for c in (COL_LOGF, COL_Q, COL_I, COL_Z, COL_OG, COL_GM, COL_GH):
            subs = []
            for j in range(0, D_MODEL, PROJ_SUB_COLS):
                subs.append(_proj_piece(c, xb, wa_ref, wb_ref, lbp_ref, slice(j, j + PROJ_SUB_COLS)))
                yield
            got[c] = (cat([s[0] for s in subs]), cat([s[1] for s in subs])) if c == COL_LOGF else cat(subs)

    proj = _Stream(remaining_pieces())

    def need(c):
        while c not in got:
            proj.step(1)
        return got[c]

    def ssd_branch():
        ys, h_new = yield from _ssd_streams(chunks, {0: _state_blocks(h_scr[...])}, a_row,
                                            stage_major=False)
        gn = _ssd_gated_norm(jnp.concatenate(ys, axis=0), xm, need(COL_Z), dskip_ref[...], mnw_ref[...])
        ym = _dot(gn.astype(BF16), wmo_ref[...])
        yield
        return ym, h_new

    def hgrn_branch():
        (log_f, k), q, v = need(COL_LOGF), need(COL_Q), need(COL_I)
        blocks = []
        for b in range(tl // HGRN_BLOCK):
            rs = slice(b * HGRN_BLOCK, (b + 1) * HGRN_BLOCK)
            blocks.append((0, q[rs], log_f[rs], k[rs], v[rs]))
        outs, s_new = yield from _hgrn_streams(blocks, {0: [s_scr[h] for h in range(H_HEADS)]}, H_CHUNK)
        o = jnp.concatenate(outs, axis=0) * hnw_ref[...] * need(COL_OG)
        yh = _dot(o.astype(BF16), who_ref[...])
        yield
        return yh, s_new

    ssd, hgrn = _Stream(ssd_branch()), _Stream(hgrn_branch())
    n_ssd, n_hgrn = MIXER_INTERLEAVE
    while COL_I not in got:
        proj.step(1)
        ssd.step(n_ssd)
    while not (proj.done and ssd.done and hgrn.done):
        proj.step(1)
        ssd.step(n_ssd)
        hgrn.step(n_hgrn)
    (ym, h_new), (yh, s_new) = ssd.value, hgrn.value
    for h in range(M_HEADS):
        h_scr[h * M_HEAD_DIM:(h + 1) * M_HEAD_DIM, :] = h_new[0][h]
    for h in range(H_HEADS):
        s_scr[h] = s_new[0][h]
    mix_ref[0] = got[COL_GM] * ym + got[COL_GH] * yh

    @pl.when(t == pl.num_programs(1) - 1)
    def _():
        conv_out_ref[0] = tail
        ssm_out_ref[0] = h_scr[...]
        for h in range(H_HEADS):
            s_out_ref[0, h] = jnp.transpose(s_scr[h])


def _tile_rows(tokens, before):
    n, c = tokens.shape
    parts = [tokens, jnp.zeros((SAMPLE_ROWS - n - before, c), tokens.dtype)]
    if before:
        parts.insert(0, jnp.zeros((before, c), tokens.dtype))
    return jnp.concatenate(parts, axis=0)


def _ssd_sample_kernel(x_ref, bc_ref, z_ref, dt_ref, cs_ref, h0_ref, cw_ref, cb_ref, alog_ref,
                       dskip_ref, mnw_ref, wmo_ref, ym_ref, conv_out_ref, ssm_out_ref):
    bb = h0_ref.shape[0]
    n_tok = x_ref.shape[0] // bb
    a_row = -jnp.exp(alog_ref[...])
    chunks, zs, h_prev = [], [], {}
    for i in range(bb):
        rs = slice(i * n_tok, (i + 1) * n_tok)
        raw = jnp.concatenate([x_ref[rs, :], bc_ref[rs, :]], axis=1)
        u = jnp.concatenate([cs_ref[i], raw, jnp.zeros((1, M_CONV_DIM), F32)], axis=0)
        conv_out_ref[i] = u[n_tok:n_tok + M_CONV - 1, :]
        conv = cb_ref[...] + cw_ref[M_CONV - 1:M_CONV, :] * u
        for j in range(1, M_CONV):
            conv = conv + cw_ref[M_CONV - 1 - j:M_CONV - j, :] * pltpu.roll(u, j, 0)
        xbc = _silu(conv)
        xm = xbc[:, :M_INNER]
        bm = xbc[:, M_INNER:M_INNER + M_GROUPS * M_STATE]
        cm = xbc[:, M_INNER + M_GROUPS * M_STATE:]
        chunks.append((i, xm, bm, cm, _tile_rows(dt_ref[rs, :], SAMPLE_FIRST)))
        zs.append(_tile_rows(z_ref[rs, :], SAMPLE_FIRST))
        h_prev[i] = _state_blocks(h0_ref[i])
    ys, h_new = _drain(_ssd_streams(chunks, h_prev, a_row, stage_major=True))
    gns = []
    for i in range(bb):
        for h in range(M_HEADS):
            ssm_out_ref[i, h * M_HEAD_DIM:(h + 1) * M_HEAD_DIM, :] = h_new[i][h]
        gns.append(_ssd_gated_norm(ys[i], chunks[i][1], zs[i], dskip_ref[...], mnw_ref[...]))
    ym = _dot(jnp.concatenate(gns, axis=0).astype(BF16), wmo_ref[...])
    for i in range(bb):
        ym_ref[i * n_tok:(i + 1) * n_tok, :] = ym[i * SAMPLE_ROWS + SAMPLE_FIRST:
                                                  i * SAMPLE_ROWS + SAMPLE_FIRST + n_tok, :]


def _hgrn_gates(f_raw, lbp_ref, cols=slice(0, H_HEADS * H_KEY)):
    p0 = lbp_ref[0:1, cols]
    p1 = lbp_ref[1:2, cols]
    m = jnp.maximum(p0, p1)
    e0 = jnp.exp(p0 - m)
    e1 = jnp.exp(p1 - m)
    lb = e0 / (e0 + e1)
    log_f = jnp.log(lb + (1.0 - lb) * _sigmoid(f_raw))
    k = (1.0 - lb) * _sigmoid(-f_raw)
    return log_f, k


def _hgrn_streams(blocks, s_prev, ch):
    rows = blocks[0][1].shape[0]
    n = len(blocks)
    shift = ch.bit_length() - 1
    n_chunks = rows // ch
    row_i = lax.broadcasted_iota(jnp.int32, (rows, LANES), 0)
    col_i = lax.broadcasted_iota(jnp.int32, (rows, LANES), 1)
    causal = (jnp.right_shift(row_i, shift) == jnp.right_shift(col_i, shift)) & (col_i <= row_i)
    chunk_t = jnp.right_shift(lax.broadcasted_iota(jnp.int32, (LANES, LANES), 1), shift)
    heads = [slice(h * H_KEY, (h + 1) * H_KEY) for h in range(H_HEADS)]
    chunks = [slice(c * ch, (c + 1) * ch) for c in range(n_chunks)]
    items = [(i, h) for i in range(n) for h in range(H_HEADS)]

    pre = []
    for (_, q, log_f, k, v) in blocks:
        b_cum = _chunk_cumsum(log_f, ch)
        lasts = [b_cum[c * ch + ch - 1:c * ch + ch, :] for c in range(n_chunks)]
        b_last = jnp.concatenate([jnp.broadcast_to(l, (ch, l.shape[1])) for l in lasts], axis=0)
        pre.append(dict(qd=q * jnp.exp(b_cum), kd=k * jnp.exp(-b_cum),
                        k_end=k * jnp.exp(b_last - b_cum), lasts=lasts, v=v))
    qh = {(i, h): pre[i]["qd"][:, heads[h]].astype(BF16) for (i, h) in items}
    vh = {(i, h): _pad_rows(pre[i]["v"][:, heads[h]]) for (i, h) in items}
    sc, ds, y_intra, y_inter = {}, {}, {}, {}
    for (i, h) in items:
        sc[i, h] = _dot_nt(qh[i, h], _pad_rows(pre[i]["kd"][:, heads[h]]).astype(BF16))
        yield
    for (i, h) in items:
        v_t = jnp.transpose(vh[i, h])
        ke = _pad_rows(pre[i]["k_end"][:, heads[h]]).astype(BF16)
        ds[i, h] = []
        for c in range(n_chunks):
            v_tc = v_t if n_chunks == 1 else jnp.where(chunk_t == c, v_t, 0.0)
            ds[i, h].append(_dot(v_tc.astype(BF16), ke))
            yield
    for (i, h) in items:
        y_intra[i, h] = _dot(jnp.where(causal, sc[i, h], 0.0).astype(BF16), vh[i, h].astype(BF16))
        yield
    cur = {seq: list(states) for seq, states in s_prev.items()}
    enter = {}
    for i, blk in enumerate(blocks):
        seq = blk[0]
        for h in range(H_HEADS):
            s = cur[seq][h]
            per_chunk = []
            for c in range(n_chunks):
                per_chunk.append(s)
                s = jnp.exp(pre[i]["lasts"][c][:, heads[h]]) * s + ds[i, h][c]
            enter[i, h] = per_chunk
            cur[seq][h] = s
    for (i, h) in items:
        y_inter[i, h] = []
        for c in range(n_chunks):
            y_inter[i, h].append(_dot_nt(qh[i, h][chunks[c]], enter[i, h][c].astype(BF16)))
            yield
    outs = []
    for i in range(n):
        cols = []
        for h in range(H_HEADS):
            pieces = [y_intra[i, h][chunks[c]] + y_inter[i, h][c] for c in range(n_chunks)]
            o_h = pieces[0] if n_chunks == 1 else jnp.concatenate(pieces, axis=0)
            cols.append(o_h * lax.rsqrt(jnp.mean(o_h * o_h, axis=-1, keepdims=True) + EPS))
        outs.append(jnp.concatenate(cols, axis=1))
    return outs, cur


def _hgrn_sample_kernel(q_ref, logf_ref, k_ref, i_ref, og_ref, ym_ref, gm_ref, gh_ref, s0_ref,
                        hnw_ref, who_ref, mix_ref, s_out_ref):
    bb = s0_ref.shape[0]
    n_tok = q_ref.shape[0] // bb
    rows = [slice(i * n_tok, (i + 1) * n_tok) for i in range(bb)]
    blocks, s_prev = [], {}
    for i in range(bb):
        blocks.append((i, _tile_rows(q_ref[rows[i], :], 0), _tile_rows(logf_ref[rows[i], :], 0),
                       _tile_rows(k_ref[rows[i], :], 0), _tile_rows(i_ref[rows[i], :], 0)))
        s_prev[i] = [jnp.transpose(s0_ref[i, h]) for h in range(H_HEADS)]
    outs, s_new = _drain(_hgrn_streams(blocks, s_prev, SAMPLE_ROWS))
    os_ = []
    for i in range(bb):
        for h in range(H_HEADS):
            s_out_ref[i, h] = jnp.transpose(s_new[i][h])
        os_.append(outs[i] * hnw_ref[...] * _tile_rows(og_ref[rows[i], :], 0))
    yh = _dot(jnp.concatenate(os_, axis=0).astype(BF16), who_ref[...])
    for i in range(bb):
        mix_ref[rows[i], :] = (gm_ref[rows[i], :] * ym_ref[rows[i], :]
                               + gh_ref[rows[i], :] * yh[i * SAMPLE_ROWS:i * SAMPLE_ROWS + n_tok, :])


def _resident(shape):
    nd = len(shape)
    return pl.BlockSpec(shape, lambda *_: (0,) * nd, pipeline_mode=pl.Buffered(1))


def _params(semantics):
    return pltpu.CompilerParams(dimension_semantics=semantics, vmem_limit_bytes=VMEM_LIMIT)


def _row_tile(n_rows, want):
    tm = min(want, n_rows)
    assert n_rows % tm == 0
    return tm


def _ffn_ln(x, wg, wu, wd, g, b):
    n = x.shape[0]
    tm = _row_tile(n, 256)
    row = pl.BlockSpec((tm, D_MODEL), lambda i: (i, 0))
    return pl.pallas_call(
        _ffn_ln_kernel,
        grid=(n // tm,),
        in_specs=[row, _resident(wg.shape), _resident(wu.shape), _resident(wd.shape),
                  _resident(g.shape), _resident(b.shape)],
        out_specs=row,
        out_shape=jax.ShapeDtypeStruct((n, D_MODEL), F32),
        compiler_params=_params(("parallel",)),
        name="ffn_ln",
    )(x, wg, wu, wd, g, b)


def _in_proj(x1, proj_w):
    n = x1.shape[0]
    tm = _row_tile(n, 256)
    row = lambda width: pl.BlockSpec((tm, width), lambda i: (i, 0))
    return pl.pallas_call(
        _in_proj_kernel,
        grid=(n // tm,),
        in_specs=[row(D_MODEL)] + [_resident(w.shape) for w in proj_w],
        out_specs=[row(PROJ_MAIN), row(LANES), row(D_MODEL)],
        out_shape=[jax.ShapeDtypeStruct((n, PROJ_MAIN), F32),
                   jax.ShapeDtypeStruct((n, LANES), F32),
                   jax.ShapeDtypeStruct((n, D_MODEL), F32)],
        compiler_params=_params(("parallel",)),
        name="in_proj",
    )(x1, *proj_w)


def _merge_ffn(x1, mix, *merge_w):
    n = x1.shape[0]
    tm = _row_tile(n, 512)
    row = pl.BlockSpec((tm, D_MODEL), lambda i: (i, 0))
    return pl.pallas_call(
        _merge_ffn_kernel,
        grid=(n // tm,),
        in_specs=[row, row] + [_resident(w.shape) for w in merge_w],
        out_specs=row,
        out_shape=jax.ShapeDtypeStruct((n, D_MODEL), F32),
        compiler_params=_params(("parallel",)),
        name="merge_ffn",
    )(x1, mix, *merge_w)


def _proj_mixer_prompt(x1_3, proj_w, conv_wb, ssd_w, hgrn_w):
    bsz, length, _ = x1_3.shape
    tl = MIXER_TL
    rows = pl.BlockSpec((1, tl, D_MODEL), lambda b, t: (b, t, 0))
    once = lambda *shape: pl.BlockSpec((1,) + shape, lambda b, t: (b,) + (0,) * len(shape))
    weights = tuple(proj_w) + tuple(conv_wb) + tuple(ssd_w) + tuple(hgrn_w)
    return pl.pallas_call(
        _proj_mixer_prompt_kernel,
        grid=(bsz, length // tl),
        in_specs=[rows] + [_resident(w.shape) for w in weights],
        out_specs=[rows, once(SUBLANES, M_CONV_DIM), once(M_INNER, M_STATE),
                   once(H_HEADS, H_KEY, H_VAL)],
        out_shape=[jax.ShapeDtypeStruct((bsz, length, D_MODEL), F32),
                   jax.ShapeDtypeStruct((bsz, SUBLANES, M_CONV_DIM), F32),
                   jax.ShapeDtypeStruct((bsz, M_INNER, M_STATE), F32),
                   jax.ShapeDtypeStruct((bsz, H_HEADS, H_KEY, H_VAL), F32)],
        scratch_shapes=[pltpu.VMEM((SUBLANES, M_CONV_DIM), F32),
                        pltpu.VMEM((M_INNER, M_STATE), F32),
                        pltpu.VMEM((H_HEADS, H_VAL, H_KEY), F32)],
        compiler_params=_params(("parallel", "arbitrary")),
        name="proj_mixer_prompt",
    )(x1_3, *weights)


def _ssd_sample(proj, dt, conv0, ssm0, conv_wb, ssd_w):
    bsz = ssm0.shape[0]
    bb = SAMPLE_BATCH_BLOCK
    rows = bb * (proj.shape[0] // bsz)
    col = lambda width, c: pl.BlockSpec((rows, width), lambda b: (b, c))
    seq = lambda *shape: pl.BlockSpec((bb,) + shape, lambda b: (b,) + (0,) * len(shape))
    weights = tuple(conv_wb) + tuple(ssd_w)
    return pl.pallas_call(
        _ssd_sample_kernel,
        grid=(bsz // bb,),
        in_specs=[col(D_MODEL, COL_X), col(D_MODEL, COL_BC), col(D_MODEL, COL_Z), col(LANES, 0),
                  seq(M_CONV - 1, M_CONV_DIM), seq(M_INNER, M_STATE)]
                 + [_resident(w.shape) for w in weights],
        out_specs=[col(D_MODEL, 0), seq(M_CONV - 1, M_CONV_DIM), seq(M_INNER, M_STATE)],
        out_shape=[jax.ShapeDtypeStruct((proj.shape[0], D_MODEL), F32),
                   jax.ShapeDtypeStruct((bsz, M_CONV - 1, M_CONV_DIM), F32),
                   jax.ShapeDtypeStruct((bsz, M_INNER, M_STATE), F32)],
        compiler_params=_params(("parallel",)),
        name="ssd_sample",
    )(proj, proj, proj, dt, conv0, ssm0, *weights)


def _hgrn_sample(proj, kk, ym, s0, hgrn_w):
    bsz = s0.shape[0]
    bb = SAMPLE_BATCH_BLOCK
    rows = bb * (proj.shape[0] // bsz)
    col = lambda c: pl.BlockSpec((rows, D_MODEL), lambda b: (b, c))
    state = pl.BlockSpec((bb, H_HEADS, H_KEY, H_VAL), lambda b: (b, 0, 0, 0))
    return pl.pallas_call(
        _hgrn_sample_kernel,
        grid=(bsz // bb,),
        in_specs=[col(COL_Q), col(COL_LOGF), col(0), col(COL_I), col(COL_OG), col(0), col(COL_GM),
                  col(COL_GH), state] + [_resident(w.shape) for w in hgrn_w],
        out_specs=[col(0), state],
        out_shape=[jax.ShapeDtypeStruct((proj.shape[0], D_MODEL), F32),
                   jax.ShapeDtypeStruct((bsz, H_HEADS, H_KEY, H_VAL), F32)],
        compiler_params=_params(("parallel",)),
        name="hgrn_sample",
    )(proj, proj, kk, proj, proj, ym, proj, proj, s0, *hgrn_w)


def _pad_lanes(v):
    return jnp.pad(v, (0, LANES - v.shape[0])).reshape(1, LANES)


def kernel(x_prompt, x_sample, state_conv, state_ssm, state_hgrn, ffn1_w_gate, ffn1_w_up, ffn1_w_down, ln1_g, ln1_b, w_in, conv_w, conv_b, dt_bias, a_log, d_skip, m_norm_w, w_m_out, hgrn_lb_param, h_norm_w, w_h_out, w_o, ln2_g, ln2_b, ffn2_w_gate, ffn2_w_up, ffn2_w_down, ln3_g, ln3_b):
    assert w_in.shape[0] == 1, "single trunk layer"
    bp, lp, _ = x_prompt.shape
    bs, ls, _ = x_sample.shape
    assert ls == SAMPLE_ROWS - M_CONV

    row = lambda v: v[0].reshape(1, -1)
    f1 = (ffn1_w_gate[0], ffn1_w_up[0], ffn1_w_down[0])
    f2 = (ffn2_w_gate[0].astype(BF16), ffn2_w_up[0].astype(BF16), ffn2_w_down[0].astype(BF16))
    wi = w_in[0]
    dt_end = W_DT_AT + M_HEADS
    w_dt = jnp.pad(wi[:, W_DT_AT:dt_end], ((0, 0), (0, LANES - M_HEADS))).astype(BF16)
    proj_w = (wi[:, :W_DT_AT].astype(BF16), wi[:, dt_end:].astype(BF16), w_dt,
              _pad_lanes(dt_bias[0]), hgrn_lb_param)
    conv_wb = (conv_w[0], row(conv_b))
    ssd_w = (_pad_lanes(a_log[0]), jnp.repeat(d_skip[0], M_HEAD_DIM).reshape(1, M_INNER),
             row(m_norm_w), w_m_out[0].astype(BF16))
    hgrn_w = (row(h_norm_w), w_h_out[0].astype(BF16))
    merge_w = (w_o[0], row(ln2_g), row(ln2_b)) + f2 + (row(ln3_g), row(ln3_b))

    x1 = _ffn_ln(x_prompt.reshape(bp * lp, D_MODEL), *f1, row(ln1_g), row(ln1_b))
    mix, conv_p, ssm_p, hg_p = _proj_mixer_prompt(x1.reshape(bp, lp, D_MODEL), proj_w, conv_wb,
                                                  ssd_w, hgrn_w)
    y_prompt = _merge_ffn(x1, mix.reshape(-1, D_MODEL), *merge_w).reshape(bp, lp, D_MODEL)
    new_conv_p = conv_p[:, SUBLANES - (M_CONV - 1):, :][None]
    new_ssm_p = ssm_p.reshape(1, bp, M_HEADS, M_HEAD_DIM, M_STATE)
    new_hg_p = hg_p[None]

    x1s = _ffn_ln(x_sample.reshape(bs * ls, D_MODEL), *f1, row(ln1_g), row(ln1_b))
    projs, dts, kks = _in_proj(x1s, proj_w)
    yms, conv_s, ssm_s = _ssd_sample(projs, dts, state_conv[0],
                                     state_ssm[0].reshape(bs, M_INNER, M_STATE), conv_wb, ssd_w)
    mixs, hg_s = _hgrn_sample(projs, kks, yms, state_hgrn[0], hgrn_w)
    y_sample = _merge_ffn(x1s, mixs, *merge_w).reshape(bs, ls, D_MODEL)
    new_conv_s = conv_s[None]
    new_ssm_s = ssm_s.reshape(1, bs, M_HEADS, M_HEAD_DIM, M_STATE)
    new_hg_s = hg_s[None]

    return (y_prompt, y_sample, new_conv_p, new_ssm_p, new_hg_p, new_conv_s, new_ssm_s, new_hg_s)
```

```python
import jax
import jax.numpy as jnp
from jax import lax
from jax.experimental import pallas as pl
from jax.experimental.pallas import tpu as pltpu

F32 = jnp.float32
BF16 = jnp.bfloat16

D_MODEL = 1024
D_FF = 2816
M_HEADS = 16
M_HEAD_DIM = 64
M_GROUPS = 4
M_STATE = 128
M_INNER = M_HEADS * M_HEAD_DIM
M_CONV = 4
M_CONV_DIM = M_INNER + 2 * M_GROUPS * M_STATE
HEADS_PER_GROUP = M_HEADS // M_GROUPS
GROUP_COLS = HEADS_PER_GROUP * M_HEAD_DIM
H_HEADS = 8
H_KEY = 128
H_VAL = 128
H_CHUNK = 32
ALPHA = 2.0 ** 0.25
EPS = 1e-5

LANES = 128
SUBLANES = 8
VMEM_LIMIT = 56 * 1024 * 1024

FF_CHUNK = D_FF // 2
FFN1_TM = 256
FFN2_TM = 512
PROJ_MAIN = 9 * D_MODEL
COL_X, COL_Q, COL_GM = 0, 1, 2
COL_BC, COL_I, COL_GH = 3, 4, 5
COL_Z, COL_LOGF, COL_OG = 6, 7, 8
W_DT_AT = M_INNER + M_CONV_DIM
PROJ_PIECES = ((COL_X, ("a", M_INNER)), (COL_Q, ("b", 0)), (COL_GM, ("b", 4 * D_MODEL)),
               (COL_BC, ("a", 2 * M_INNER)), (COL_I, ("b", 2 * D_MODEL)), (COL_GH, ("b", 5 * D_MODEL)),
               (COL_Z, ("a", 0)), (COL_LOGF, ("b", D_MODEL)), (COL_OG, ("b", 3 * D_MODEL)))

SSD_CHUNK = 128
HGRN_BLOCK = 128
MIXER_TL = 256
MIXER_INTERLEAVE = (5, 20)
SAMPLE_ROWS = 8
SAMPLE_FIRST = M_CONV - 1
SAMPLE_BATCH_BLOCK = 8


def _sigmoid(x):
    return 1.0 / (1.0 + jnp.exp(-x))


def _silu(x):
    return x * _sigmoid(x)


def _softplus(x):
    return jnp.maximum(x, 0.0) + jnp.log(1.0 + jnp.exp(-jnp.abs(x)))


def _dot(a, b):
    return jnp.dot(a, b, preferred_element_type=F32)


def _dot_nt(a, b):
    return lax.dot_general(a, b, (((1,), (1,)), ((), ())), preferred_element_type=F32)


def _pad_rows(a):
    q = a.shape[0]
    if q == LANES:
        return a
    return jnp.concatenate([a, jnp.zeros((LANES - q, a.shape[1]), a.dtype)], axis=0)


def _chunk_cumsum(x, ch):
    rows, cols = x.shape
    g = rows // SUBLANES
    y = x.reshape(g, SUBLANES, cols)
    sub = lax.broadcasted_iota(jnp.int32, (g, SUBLANES, cols), 1)
    s = 1
    while s < SUBLANES:
        y = y + jnp.where(sub >= s, pltpu.roll(y, s, 1), 0.0)
        s *= 2
    per = ch // SUBLANES
    if per > 1:
        y4 = y.reshape(g // per, per, SUBLANES, cols)
        carry, outs = None, []
        for j in range(per):
            yj = y4[:, j]
            outs.append(yj if carry is None else yj + carry)
            tot = jnp.broadcast_to(yj[:, SUBLANES - 1:SUBLANES, :], yj.shape)
            carry = tot if carry is None else carry + tot
        y = jnp.stack(outs, axis=1).reshape(g, SUBLANES, cols)
    return y.reshape(rows, cols)


def _layer_norm(y, g, b):
    mu = jnp.mean(y, axis=-1, keepdims=True)
    yc = y - mu
    var = jnp.mean(yc * yc, axis=-1, keepdims=True)
    return yc * lax.rsqrt(var + EPS) * g + b


def _swiglu(x, wg_ref, wu_ref, wd_ref):
    xb = x.astype(wg_ref.dtype)
    acc = None
    for c in range(D_FF // FF_CHUNK):
        sl = slice(c * FF_CHUNK, (c + 1) * FF_CHUNK)
        hg = _dot(xb, wg_ref[:, sl])
        hu = _dot(xb, wu_ref[:, sl])
        act = (_silu(hg) * hu).astype(wd_ref.dtype)
        part = _dot(act, wd_ref[sl, :])
        acc = part if acc is None else acc + part
    return acc


def _ffn_ln_kernel(x_ref, wg_ref, wu_ref, wd_ref, g_ref, b_ref, o_ref):
    x = x_ref[...]
    y = ALPHA * x + 0.5 * _swiglu(x, wg_ref, wu_ref, wd_ref)
    o_ref[...] = _layer_norm(y, g_ref[...], b_ref[...])


def _activate_piece(c, p, lbp_ref, sub=slice(0, D_MODEL)):
    if c in (COL_GM, COL_GH):
        return _sigmoid(p)
    if c in (COL_Z, COL_OG):
        return _silu(p)
    if c == COL_LOGF:
        return _hgrn_gates(p, lbp_ref, sub)
    return p


def _proj_piece(c, xb, wa_ref, wb_ref, lbp_ref, sub=slice(0, D_MODEL), activate=True):
    part, at = dict(PROJ_PIECES)[c]
    w_ref = wa_ref if part == "a" else wb_ref
    p = _dot(xb, w_ref[:, at + sub.start:at + sub.stop])
    return _activate_piece(c, p, lbp_ref, sub) if activate else p


def _proj_dt(xb, wdt_ref, dtb_ref):
    return _softplus(_dot(xb, wdt_ref[...]) + dtb_ref[...])


def _in_proj_kernel(x_ref, wa_ref, wb_ref, wdt_ref, dtb_ref, lbp_ref, o_ref, odt_ref, k_ref):
    xb = x_ref[...].astype(BF16)
    for c, _ in PROJ_PIECES:
        cols = slice(c * D_MODEL, (c + 1) * D_MODEL)
        p = _proj_piece(c, xb, wa_ref, wb_ref, lbp_ref)
        if c == COL_LOGF:
            o_ref[:, cols], k_ref[...] = p
        else:
            o_ref[:, cols] = p
    odt_ref[...] = _proj_dt(xb, wdt_ref, dtb_ref)


def _merge_ffn_kernel(x1_ref, mix_ref, wo_ref, g2_ref, b2_ref,
                      wg_ref, wu_ref, wd_ref, g3_ref, b3_ref, o_ref):
    x2 = _layer_norm(ALPHA * x1_ref[...] + _dot(mix_ref[...], wo_ref[...]),
                     g2_ref[...], b2_ref[...])
    y = ALPHA * x2 + 0.5 * _swiglu(x2, wg_ref, wu_ref, wd_ref)
    o_ref[...] = _layer_norm(y, g3_ref[...], b3_ref[...])


def _ssd_streams(chunks, h_prev, a_row, stage_major):
    q = chunks[0][1].shape[0]
    n = len(chunks)
    row_i = lax.broadcasted_iota(jnp.int32, (q, LANES), 0)
    col_i = lax.broadcasted_iota(jnp.int32, (q, LANES), 1)
    causal = col_i <= row_i
    head_blk = jnp.right_shift(lax.broadcasted_iota(jnp.int32, (q, GROUP_COLS), 1),
                               M_HEAD_DIM.bit_length() - 1)
    groups = [slice(g * M_STATE, (g + 1) * M_STATE) for g in range(M_GROUPS)]
    rows_r = [slice(r * M_HEAD_DIM, (r + 1) * M_HEAD_DIM) for r in range(HEADS_PER_GROUP)]

    pre, ops = {}, {}

    def prepare(i):
        _, xm, bm, cm, dt = chunks[i]
        a_cum = _chunk_cumsum(dt * a_row, q)
        a_last = a_cum[q - 1:q, :]
        to_end = jnp.exp(a_last - a_cum) * dt
        pre[i] = dict(
            a_cum=a_cum, e_last=jnp.exp(a_last), e_cum=jnp.exp(a_cum),
            a_cum_t=jnp.transpose(_pad_rows(a_cum)),
            dt_t=jnp.transpose(_pad_rows(dt)),
            to_end_t=jnp.transpose(_pad_rows(to_end)))
        for g in range(M_GROUPS):
            xg = _pad_rows(xm[:, g * GROUP_COLS:(g + 1) * GROUP_COLS])
            ops[i, g] = dict(bg=_pad_rows(bm[:, groups[g]]).astype(BF16),
                             cg=cm[:, groups[g]].astype(BF16),
                             xg_b=xg.astype(BF16), xg_t=jnp.transpose(xg))

    def mm_cb(i, g):
        return _dot_nt(ops[i, g]["cg"], ops[i, g]["bg"])

    def state_lhs(i, g, r):
        h = g * HEADS_PER_GROUP + r
        return (ops[i, g]["xg_t"][rows_r[r], :] * pre[i]["to_end_t"][h:h + 1, :]).astype(BF16)

    def mm_state(i, g, r):
        return _dot(state_lhs(i, g, r), ops[i, g]["bg"])

    def intra_lhs(i, g, r, cb):
        h = g * HEADS_PER_GROUP + r
        seg = pre[i]["a_cum"][:, h:h + 1] - pre[i]["a_cum_t"][h:h + 1, :]
        decay = jnp.where(causal, jnp.exp(jnp.where(causal, seg, 0.0)), 0.0)
        return (cb * decay * pre[i]["dt_t"][h:h + 1, :]).astype(BF16)

    def mm_intra(i, g, r, cb):
        return _dot(intra_lhs(i, g, r, cb), ops[i, g]["xg_b"])

    def mm_inter(i, g, blocks):
        hg = jnp.concatenate(blocks[g * HEADS_PER_GROUP:(g + 1) * HEADS_PER_GROUP], axis=0)
        return _dot_nt(ops[i, g]["cg"], hg.astype(BF16))

    def combine(i, g, parts, y_inter):
        acc = jnp.zeros((q, GROUP_COLS), F32)
        for r in range(HEADS_PER_GROUP):
            h = g * HEADS_PER_GROUP + r
            acc = jnp.where(head_blk == r, parts[r] + y_inter * pre[i]["e_cum"][:, h:h + 1], acc)
        return acc

    cur = {seq: list(blocks) for seq, blocks in h_prev.items()}
    ys = []
    if not stage_major:
        for i, chunk in enumerate(chunks):
            prepare(i)
            enter = cur[chunk[0]]
            cbs = [mm_cb(i, g) for g in range(M_GROUPS)]
            y_inters = [mm_inter(i, g, enter) for g in range(M_GROUPS)]
            yield
            cols, new = [], []
            for g in range(M_GROUPS):
                acc = jnp.zeros((q, GROUP_COLS), F32)
                for r in range(HEADS_PER_GROUP):
                    h = g * HEADS_PER_GROUP + r
                    y_h = mm_intra(i, g, r, cbs[g]) + y_inters[g] * pre[i]["e_cum"][:, h:h + 1]
                    acc = jnp.where(head_blk == r, y_h, acc)
                    new.append(pre[i]["e_last"][:, h:h + 1] * enter[h] + mm_state(i, g, r))
                    yield
                cols.append(acc)
            cur[chunk[0]] = new
            ys.append(jnp.concatenate(cols, axis=1))
        return ys, cur
    for i in range(n):
        prepare(i)
    cb = {(i, g): mm_cb(i, g) for i in range(n) for g in range(M_GROUPS)}
    st = {(i, g, r): mm_state(i, g, r)
          for i in range(n) for g in range(M_GROUPS) for r in range(HEADS_PER_GROUP)}
    part = {(i, g, r): mm_intra(i, g, r, cb[i, g])
            for i in range(n) for g in range(M_GROUPS) for r in range(HEADS_PER_GROUP)}
    enter = []
    for i, chunk in enumerate(chunks):
        seq = chunk[0]
        enter.append(list(cur[seq]))
        cur[seq] = [pre[i]["e_last"][:, h:h + 1] * cur[seq][h]
                    + st[i, h // HEADS_PER_GROUP, h % HEADS_PER_GROUP] for h in range(M_HEADS)]
    for i in range(n):
        cols = []
        for g in range(M_GROUPS):
            y_inter = mm_inter(i, g, enter[i])
            cols.append(combine(i, g, [part[i, g, r] for r in range(HEADS_PER_GROUP)], y_inter))
        ys.append(jnp.concatenate(cols, axis=1))
    return ys, cur


def _ssd_gated_norm(y_ssd, xm, z_act, dskip, mnw):
    g = (y_ssd + dskip * xm) * z_act
    outs = []
    for k in range(M_GROUPS):
        gk = g[:, k * GROUP_COLS:(k + 1) * GROUP_COLS]
        outs.append(gk * lax.rsqrt(jnp.mean(gk * gk, axis=-1, keepdims=True) + EPS))
    return jnp.concatenate(outs, axis=1) * mnw


def _state_blocks(h):
    return [h[k * M_HEAD_DIM:(k + 1) * M_HEAD_DIM, :] for k in range(M_HEADS)]


def _conv_silu(u, prev8, cw, cbias):
    tl = u.shape[0]
    ext = jnp.concatenate([prev8, u], axis=0)
    conv = cbias + cw[M_CONV - 1:M_CONV, :] * u
    for j in range(1, M_CONV):
        shifted = pltpu.roll(ext, j, 0)[SUBLANES:SUBLANES + tl, :]
        conv = conv + cw[M_CONV - 1 - j:M_CONV - j, :] * shifted
    return _silu(conv)


def _drain(gen):
    try:
        while True:
            next(gen)
    except StopIteration as stop:
        return stop.value


class _Stream:
    def __init__(self, gen):
        self.gen, self.done, self.value = gen, False, None

    def step(self, n):
        for _ in range(n):
            if self.done:
                return
            try:
                next(self.gen)
            except StopIteration as stop:
                self.done, self.value = True, stop.value


def _proj_mixer_prompt_kernel(x1_ref, wa_ref, wb_ref, wdt_ref, dtb_ref, lbp_ref, cw_ref, cb_ref, alog_ref,
                              dskip_ref, mnw_ref, wmo_ref, hnw_ref, who_ref,
                              mix_ref, conv_out_ref, ssm_out_ref, s_out_ref, prev_scr, h_scr, s_scr):
    t = pl.program_id(1)
    tl = x1_ref.shape[1]

    @pl.when(t == 0)
    def _():
        prev_scr[...] = jnp.zeros_like(prev_scr)
        h_scr[...] = jnp.zeros_like(h_scr)
        s_scr[...] = jnp.zeros_like(s_scr)

    xb = x1_ref[0].astype(BF16)
    piece = lambda c: _proj_piece(c, xb, wa_ref, wb_ref, lbp_ref)
    x_raw, bc_raw = piece(COL_X), piece(COL_BC)
    dt = _proj_dt(xb, wdt_ref, dtb_ref)
    tail = jnp.concatenate([x_raw[tl - SUBLANES:tl, :], bc_raw[tl - SUBLANES:tl, :]], axis=1)
    xm = _conv_silu(x_raw, prev_scr[:, :M_INNER], cw_ref[:, :M_INNER], cb_ref[:, :M_INNER])
    bc = _conv_silu(bc_raw, prev_scr[:, M_INNER:], cw_ref[:, M_INNER:], cb_ref[:, M_INNER:])
    prev_scr[...] = tail
    bm = bc[:, :M_GROUPS * M_STATE]
    cm = bc[:, M_GROUPS * M_STATE:]
    a_row = -jnp.exp(alog_ref[...])
    chunks = []
    for c in range(tl // SSD_CHUNK):
        rs = slice(c * SSD_CHUNK, (c + 1) * SSD_CHUNK)
        chunks.append((0, xm[rs], bm[rs], cm[rs], dt[rs]))
    got = {}

    def remaining_pieces():
        for c in (COL_LOGF, COL_Q, COL_I, COL_Z, COL_OG, COL_GM, COL_GH):
            got[c] = _proj_piece(c, xb, wa_ref, wb_ref, lbp_ref, activate=(c == COL_LOGF))
            yield

    proj = _Stream(remaining_pieces())

    def need(c):
        while c not in got:
            proj.step(1)
        return got[c] if c == COL_LOGF else _activate_piece(c, got[c], lbp_ref)

    def ssd_branch():
        ys, h_new = yield from _ssd_streams(chunks, {0: _state_blocks(h_scr[...])}, a_row,
                                            stage_major=False)
        gn = _ssd_gated_norm(jnp.concatenate(ys, axis=0), xm, need(COL_Z), dskip_ref[...], mnw_ref[...])
        ym = _dot(gn.astype(BF16), wmo_ref[...])
        yield
        return ym, h_new

    def hgrn_branch():
        (log_f, k), q, v = need(COL_LOGF), need(COL_Q), need(COL_I)
        blocks = []
        for b in range(tl // HGRN_BLOCK):
            rs = slice(b * HGRN_BLOCK, (b + 1) * HGRN_BLOCK)
            blocks.append((0, q[rs], log_f[rs], k[rs], v[rs]))
        outs, s_new = yield from _hgrn_streams(blocks, {0: [s_scr[h] for h in range(H_HEADS)]}, H_CHUNK)
        o = jnp.concatenate(outs, axis=0) * hnw_ref[...] * need(COL_OG)
        yh = _dot(o.astype(BF16), who_ref[...])
        yield
        return yh, s_new

    ssd, hgrn = _Stream(ssd_branch()), _Stream(hgrn_branch())
    n_ssd, n_hgrn = MIXER_INTERLEAVE
    while COL_I not in got:
        proj.step(1)
        ssd.step(n_ssd)
    while not (proj.done and ssd.done and hgrn.done):
        proj.step(1)
        ssd.step(n_ssd)
        hgrn.step(n_hgrn)
    (ym, h_new), (yh, s_new) = ssd.value, hgrn.value
    for h in range(M_HEADS):
        h_scr[h * M_HEAD_DIM:(h + 1) * M_HEAD_DIM, :] = h_new[0][h]
    for h in range(H_HEADS):
        s_scr[h] = s_new[0][h]
    mix_ref[0] = need(COL_GM) * ym + need(COL_GH) * yh

    @pl.when(t == pl.num_programs(1) - 1)
    def _():
        conv_out_ref[0] = tail
        ssm_out_ref[0] = h_scr[...]
        for h in range(H_HEADS):
            s_out_ref[0, h] = jnp.transpose(s_scr[h])


def _tile_rows(tokens, before):
    n, c = tokens.shape
    parts = [tokens, jnp.zeros((SAMPLE_ROWS - n - before, c), tokens.dtype)]
    if before:
        parts.insert(0, jnp.zeros((before, c), tokens.dtype))
    return jnp.concatenate(parts, axis=0)


def _ssd_sample_kernel(x_ref, bc_ref, z_ref, dt_ref, cs_ref, h0_ref, cw_ref, cb_ref, alog_ref,
                       dskip_ref, mnw_ref, wmo_ref, ym_ref, conv_out_ref, ssm_out_ref):
    bb = h0_ref.shape[0]
    n_tok = x_ref.shape[0] // bb
    a_row = -jnp.exp(alog_ref[...])
    chunks, zs, h_prev = [], [], {}
    for i in range(bb):
        rs = slice(i * n_tok, (i + 1) * n_tok)
        raw = jnp.concatenate([x_ref[rs, :], bc_ref[rs, :]], axis=1)
        u = jnp.concatenate([cs_ref[i], raw, jnp.zeros((1, M_CONV_DIM), F32)], axis=0)
        conv_out_ref[i] = u[n_tok:n_tok + M_CONV - 1, :]
        conv = cb_ref[...] + cw_ref[M_CONV - 1:M_CONV, :] * u
        for j in range(1, M_CONV):
            conv = conv + cw_ref[M_CONV - 1 - j:M_CONV - j, :] * pltpu.roll(u, j, 0)
        xbc = _silu(conv)
        xm = xbc[:, :M_INNER]
        bm = xbc[:, M_INNER:M_INNER + M_GROUPS * M_STATE]
        cm = xbc[:, M_INNER + M_GROUPS * M_STATE:]
        chunks.append((i, xm, bm, cm, _tile_rows(dt_ref[rs, :], SAMPLE_FIRST)))
        zs.append(_tile_rows(z_ref[rs, :], SAMPLE_FIRST))
        h_prev[i] = _state_blocks(h0_ref[i])
    ys, h_new = _drain(_ssd_streams(chunks, h_prev, a_row, stage_major=True))
    gns = []
    for i in range(bb):
        for h in range(M_HEADS):
            ssm_out_ref[i, h * M_HEAD_DIM:(h + 1) * M_HEAD_DIM, :] = h_new[i][h]
        gns.append(_ssd_gated_norm(ys[i], chunks[i][1], zs[i], dskip_ref[...], mnw_ref[...]))
    ym = _dot(jnp.concatenate(gns, axis=0).astype(BF16), wmo_ref[...])
    for i in range(bb):
        ym_ref[i * n_tok:(i + 1) * n_tok, :] = ym[i * SAMPLE_ROWS + SAMPLE_FIRST:
                                                  i * SAMPLE_ROWS + SAMPLE_FIRST + n_tok, :]


def _hgrn_gates(f_raw, lbp_ref, cols=slice(0, H_HEADS * H_KEY)):
    p0 = lbp_ref[0:1, cols]
    p1 = lbp_ref[1:2, cols]
    m = jnp.maximum(p0, p1)
    e0 = jnp.exp(p0 - m)
    e1 = jnp.exp(p1 - m)
    lb = e0 / (e0 + e1)
    log_f = jnp.log(lb + (1.0 - lb) * _sigmoid(f_raw))
    k = (1.0 - lb) * _sigmoid(-f_raw)
    return log_f, k


def _hgrn_streams(blocks, s_prev, ch):
    rows = blocks[0][1].shape[0]
    n = len(blocks)
    shift = ch.bit_length() - 1
    n_chunks = rows // ch
    row_i = lax.broadcasted_iota(jnp.int32, (rows, LANES), 0)
    col_i = lax.broadcasted_iota(jnp.int32, (rows, LANES), 1)
    causal = (jnp.right_shift(row_i, shift) == jnp.right_shift(col_i, shift)) & (col_i <= row_i)
    chunk_t = jnp.right_shift(lax.broadcasted_iota(jnp.int32, (LANES, LANES), 1), shift)
    heads = [slice(h * H_KEY, (h + 1) * H_KEY) for h in range(H_HEADS)]
    chunks = [slice(c * ch, (c + 1) * ch) for c in range(n_chunks)]
    items = [(i, h) for i in range(n) for h in range(H_HEADS)]

    pre = []
    for (_, q, log_f, k, v) in blocks:
        b_cum = _chunk_cumsum(log_f, ch)
        lasts = [b_cum[c * ch + ch - 1:c * ch + ch, :] for c in range(n_chunks)]
        b_last = jnp.concatenate([jnp.broadcast_to(l, (ch, l.shape[1])) for l in lasts], axis=0)
        pre.append(dict(qd=q * jnp.exp(b_cum), kd=k * jnp.exp(-b_cum),
                        k_end=k * jnp.exp(b_last - b_cum), lasts=lasts, v=v))
    qh = {(i, h): pre[i]["qd"][:, heads[h]].astype(BF16) for (i, h) in items}
    vh = {(i, h): _pad_rows(pre[i]["v"][:, heads[h]]) for (i, h) in items}
    sc, ds, y_intra, y_inter = {}, {}, {}, {}
    for (i, h) in items:
        sc[i, h] = _dot_nt(qh[i, h], _pad_rows(pre[i]["kd"][:, heads[h]]).astype(BF16))
        yield
    for (i, h) in items:
        v_t = jnp.transpose(vh[i, h])
        ke = _pad_rows(pre[i]["k_end"][:, heads[h]]).astype(BF16)
        ds[i, h] = []
        for c in range(n_chunks):
            v_tc = v_t if n_chunks == 1 else jnp.where(chunk_t == c, v_t, 0.0)
            ds[i, h].append(_dot(v_tc.astype(BF16), ke))
            yield
    for (i, h) in items:
        y_intra[i, h] = _dot(jnp.where(causal, sc[i, h], 0.0).astype(BF16), vh[i, h].astype(BF16))
        yield
    cur = {seq: list(states) for seq, states in s_prev.items()}
    enter = {}
    for i, blk in enumerate(blocks):
        seq = blk[0]
        for h in range(H_HEADS):
            s = cur[seq][h]
            per_chunk = []
            for c in range(n_chunks):
                per_chunk.append(s)
                s = jnp.exp(pre[i]["lasts"][c][:, heads[h]]) * s + ds[i, h][c]
            enter[i, h] = per_chunk
            cur[seq][h] = s
    for (i, h) in items:
        y_inter[i, h] = []
        for c in range(n_chunks):
            y_inter[i, h].append(_dot_nt(qh[i, h][chunks[c]], enter[i, h][c].astype(BF16)))
            yield
    outs = []
    for i in range(n):
        cols = []
        for h in range(H_HEADS):
            pieces = [y_intra[i, h][chunks[c]] + y_inter[i, h][c] for c in range(n_chunks)]
            o_h = pieces[0] if n_chunks == 1 else jnp.concatenate(pieces, axis=0)
            cols.append(o_h * lax.rsqrt(jnp.mean(o_h * o_h, axis=-1, keepdims=True) + EPS))
        outs.append(jnp.concatenate(cols, axis=1))
    return outs, cur


def _hgrn_sample_kernel(q_ref, logf_ref, k_ref, i_ref, og_ref, ym_ref, gm_ref, gh_ref, s0_ref,
                        hnw_ref, who_ref, mix_ref, s_out_ref):
    bb = s0_ref.shape[0]
    n_tok = q_ref.shape[0] // bb
    rows = [slice(i * n_tok, (i + 1) * n_tok) for i in range(bb)]
    blocks, s_prev = [], {}
    for i in range(bb):
        blocks.append((i, _tile_rows(q_ref[rows[i], :], 0), _tile_rows(logf_ref[rows[i], :], 0),
                       _tile_rows(k_ref[rows[i], :], 0), _tile_rows(i_ref[rows[i], :], 0)))
        s_prev[i] = [jnp.transpose(s0_ref[i, h]) for h in range(H_HEADS)]
    outs, s_new = _drain(_hgrn_streams(blocks, s_prev, SAMPLE_ROWS))
    os_ = []
    for i in range(bb):
        for h in range(H_HEADS):
            s_out_ref[i, h] = jnp.transpose(s_new[i][h])
        os_.append(outs[i] * hnw_ref[...] * _tile_rows(og_ref[rows[i], :], 0))
    yh = _dot(jnp.concatenate(os_, axis=0).astype(BF16), who_ref[...])
    for i in range(bb):
        mix_ref[rows[i], :] = (gm_ref[rows[i], :] * ym_ref[rows[i], :]
                               + gh_ref[rows[i], :] * yh[i * SAMPLE_ROWS:i * SAMPLE_ROWS + n_tok, :])


def _resident(shape):
    nd = len(shape)
    return pl.BlockSpec(shape, lambda *_: (0,) * nd, pipeline_mode=pl.Buffered(1))


def _params(semantics):
    return pltpu.CompilerParams(dimension_semantics=semantics, vmem_limit_bytes=VMEM_LIMIT)


def _row_tile(n_rows, want):
    tm = min(want, n_rows)
    assert n_rows % tm == 0
    return tm


def _ffn_ln(x, wg, wu, wd, g, b):
    n = x.shape[0]
    tm = _row_tile(n, FFN1_TM)
    row = pl.BlockSpec((tm, D_MODEL), lambda i: (i, 0))
    return pl.pallas_call(
        _ffn_ln_kernel,
        grid=(n // tm,),
        in_specs=[row, _resident(wg.shape), _resident(wu.shape), _resident(wd.shape),
                  _resident(g.shape), _resident(b.shape)],
        out_specs=row,
        out_shape=jax.ShapeDtypeStruct((n, D_MODEL), F32),
        compiler_params=_params(("parallel",)),
        name="ffn_ln",
    )(x, wg, wu, wd, g, b)


def _merge_ffn(x1, mix, *merge_w):
    n = x1.shape[0]
    tm = _row_tile(n, FFN2_TM)
    row = pl.BlockSpec((tm, D_MODEL), lambda i: (i, 0))
    return pl.pallas_call(
        _merge_ffn_kernel,
        grid=(n // tm,),
        in_specs=[row, row] + [_resident(w.shape) for w in merge_w],
        out_specs=row,
        out_shape=jax.ShapeDtypeStruct((n, D_MODEL), F32),
        compiler_params=_params(("parallel",)),
        name="merge_ffn",
    )(x1, mix, *merge_w)


def _in_proj(x1, proj_w):
    n = x1.shape[0]
    tm = _row_tile(n, 256)
    row = lambda width: pl.BlockSpec((tm, width), lambda i: (i, 0))
    return pl.pallas_call(
        _in_proj_kernel,
        grid=(n // tm,),
        in_specs=[row(D_MODEL)] + [_resident(w.shape) for w in proj_w],
        out_specs=[row(PROJ_MAIN), row(LANES), row(D_MODEL)],
        out_shape=[jax.ShapeDtypeStruct((n, PROJ_MAIN), F32),
                   jax.ShapeDtypeStruct((n, LANES), F32),
                   jax.ShapeDtypeStruct((n, D_MODEL), F32)],
        compiler_params=_params(("parallel",)),
        name="in_proj",
    )(x1, *proj_w)


def _proj_mixer_prompt(x1_3, proj_w, conv_wb, ssd_w, hgrn_w):
    bsz, length, _ = x1_3.shape
    tl = MIXER_TL
    rows = pl.BlockSpec((1, tl, D_MODEL), lambda b, t: (b, t, 0))
    once = lambda *shape: pl.BlockSpec((1,) + shape, lambda b, t: (b,) + (0,) * len(shape))
    weights = tuple(proj_w) + tuple(conv_wb) + tuple(ssd_w) + tuple(hgrn_w)
    return pl.pallas_call(
        _proj_mixer_prompt_kernel,
        grid=(bsz, length // tl),
        in_specs=[rows] + [_resident(w.shape) for w in weights],
        out_specs=[rows, once(SUBLANES, M_CONV_DIM), once(M_INNER, M_STATE),
                   once(H_HEADS, H_KEY, H_VAL)],
        out_shape=[jax.ShapeDtypeStruct((bsz, length, D_MODEL), F32),
                   jax.ShapeDtypeStruct((bsz, SUBLANES, M_CONV_DIM), F32),
                   jax.ShapeDtypeStruct((bsz, M_INNER, M_STATE), F32),
                   jax.ShapeDtypeStruct((bsz, H_HEADS, H_KEY, H_VAL), F32)],
        scratch_shapes=[pltpu.VMEM((SUBLANES, M_CONV_DIM), F32),
                        pltpu.VMEM((M_INNER, M_STATE), F32),
                        pltpu.VMEM((H_HEADS, H_VAL, H_KEY), F32)],
        compiler_params=_params(("parallel", "arbitrary")),
        name="proj_mixer_prompt",
    )(x1_3, *weights)


def _ssd_sample(proj, dt, conv0, ssm0, conv_wb, ssd_w):
    bsz = ssm0.shape[0]
    bb = SAMPLE_BATCH_BLOCK
    rows = bb * (proj.shape[0] // bsz)
    col = lambda width, c: pl.BlockSpec((rows, width), lambda b: (b, c))
    seq = lambda *shape: pl.BlockSpec((bb,) + shape, lambda b: (b,) + (0,) * len(shape))
    weights = tuple(conv_wb) + tuple(ssd_w)
    return pl.pallas_call(
        _ssd_sample_kernel,
        grid=(bsz // bb,),
        in_specs=[col(D_MODEL, COL_X), col(D_MODEL, COL_BC), col(D_MODEL, COL_Z), col(LANES, 0),
                  seq(M_CONV - 1, M_CONV_DIM), seq(M_INNER, M_STATE)]
                 + [_resident(w.shape) for w in weights],
        out_specs=[col(D_MODEL, 0), seq(M_CONV - 1, M_CONV_DIM), seq(M_INNER, M_STATE)],
        out_shape=[jax.ShapeDtypeStruct((proj.shape[0], D_MODEL), F32),
                   jax.ShapeDtypeStruct((bsz, M_CONV - 1, M_CONV_DIM), F32),
                   jax.ShapeDtypeStruct((bsz, M_INNER, M_STATE), F32)],
        compiler_params=_params(("parallel",)),
        name="ssd_sample",
    )(proj, proj, proj, dt, conv0, ssm0, *weights)


def _hgrn_sample(proj, kk, ym, s0, hgrn_w):
    bsz = s0.shape[0]
    bb = SAMPLE_BATCH_BLOCK
    rows = bb * (proj.shape[0] // bsz)
    col = lambda c: pl.BlockSpec((rows, D_MODEL), lambda b: (b, c))
    state = pl.BlockSpec((bb, H_HEADS, H_KEY, H_VAL), lambda b: (b, 0, 0, 0))
    return pl.pallas_call(
        _hgrn_sample_kernel,
        grid=(bsz // bb,),
        in_specs=[col(COL_Q), col(COL_LOGF), col(0), col(COL_I), col(COL_OG), col(0), col(COL_GM),
                  col(COL_GH), state] + [_resident(w.shape) for w in hgrn_w],
        out_specs=[col(0), state],
        out_shape=[jax.ShapeDtypeStruct((proj.shape[0], D_MODEL), F32),
                   jax.ShapeDtypeStruct((bsz, H_HEADS, H_KEY, H_VAL), F32)],
        compiler_params=_params(("parallel",)),
        name="hgrn_sample",
    )(proj, proj, kk, proj, proj, ym, proj, proj, s0, *hgrn_w)


def _pad_lanes(v):
    return jnp.pad(v, (0, LANES - v.shape[0])).reshape(1, LANES)


def kernel(x_prompt, x_sample, state_conv, state_ssm, state_hgrn, ffn1_w_gate, ffn1_w_up, ffn1_w_down, ln1_g, ln1_b, w_in, conv_w, conv_b, dt_bias, a_log, d_skip, m_norm_w, w_m_out, hgrn_lb_param, h_norm_w, w_h_out, w_o, ln2_g, ln2_b, ffn2_w_gate, ffn2_w_up, ffn2_w_down, ln3_g, ln3_b):
    assert w_in.shape[0] == 1, "single trunk layer"
    bp, lp, _ = x_prompt.shape
    bs, ls, _ = x_sample.shape
    assert ls == SAMPLE_ROWS - M_CONV

    row = lambda v: v[0].reshape(1, -1)
    f1 = (ffn1_w_gate[0], ffn1_w_up[0], ffn1_w_down[0])
    f2 = (ffn2_w_gate[0].astype(BF16), ffn2_w_up[0].astype(BF16), ffn2_w_down[0].astype(BF16))
    wi = w_in[0]
    dt_end = W_DT_AT + M_HEADS
    w_dt = jnp.pad(wi[:, W_DT_AT:dt_end], ((0, 0), (0, LANES - M_HEADS))).astype(BF16)
    proj_w = (wi[:, :W_DT_AT].astype(BF16), wi[:, dt_end:].astype(BF16), w_dt,
              _pad_lanes(dt_bias[0]), hgrn_lb_param)
    conv_wb = (conv_w[0], row(conv_b))
    ssd_w = (_pad_lanes(a_log[0]), jnp.repeat(d_skip[0], M_HEAD_DIM).reshape(1, M_INNER),
             row(m_norm_w), w_m_out[0].astype(BF16))
    hgrn_w = (row(h_norm_w), w_h_out[0].astype(BF16))
    merge_w = (w_o[0], row(ln2_g), row(ln2_b)) + f2 + (row(ln3_g), row(ln3_b))

    x1 = _ffn_ln(x_prompt.reshape(bp * lp, D_MODEL), *f1, row(ln1_g), row(ln1_b))
    mix, conv_p, ssm_p, hg_p = _proj_mixer_prompt(x1.reshape(bp, lp, D_MODEL), proj_w, conv_wb,
                                                  ssd_w, hgrn_w)
    y_prompt = _merge_ffn(x1, mix.reshape(-1, D_MODEL), *merge_w).reshape(bp, lp, D_MODEL)
    new_conv_p = conv_p[:, SUBLANES - (M_CONV - 1):, :][None]
    new_ssm_p = ssm_p.reshape(1, bp, M_HEADS, M_HEAD_DIM, M_STATE)
    new_hg_p = hg_p[None]

    x1s = _ffn_ln(x_sample.reshape(bs * ls, D_MODEL), *f1, row(ln1_g), row(ln1_b))
    projs, dts, kks = _in_proj(x1s, proj_w)
    yms, conv_s, ssm_s = _ssd_sample(projs, dts, state_conv[0],
                                     state_ssm[0].reshape(bs, M_INNER, M_STATE), conv_wb, ssd_w)
    mixs, hg_s = _hgrn_sample(projs, kks, yms, state_hgrn[0], hgrn_w)
    y_sample = _merge_ffn(x1s, mixs, *merge_w).reshape(bs, ls, D_MODEL)
    new_conv_s = conv_s[None]
    new_ssm_s = ssm_s.reshape(1, bs, M_HEADS, M_HEAD_DIM, M_STATE)
    new_hg_s = hg_s[None]

    return (y_prompt, y_sample, new_conv_p, new_ssm_p, new_hg_p, new_conv_s, new_ssm_s, new_hg_s)
```

```python
import jax
import jax.numpy as jnp
from jax import lax
from jax.experimental import pallas as pl
from jax.experimental.pallas import tpu as pltpu

F32 = jnp.float32
BF16 = jnp.bfloat16

D_MODEL = 1024
D_FF = 2816
M_HEADS = 16
M_HEAD_DIM = 64
M_GROUPS = 4
M_STATE = 128
M_INNER = M_HEADS * M_HEAD_DIM
M_CONV = 4
M_CONV_DIM = M_INNER + 2 * M_GROUPS * M_STATE
HEADS_PER_GROUP = M_HEADS // M_GROUPS
GROUP_COLS = HEADS_PER_GROUP * M_HEAD_DIM
H_HEADS = 8
H_KEY = 128
H_VAL = 128
H_CHUNK = 32
ALPHA = 2.0 ** 0.25
EPS = 1e-5

LANES = 128
SUBLANES = 8
VMEM_LIMIT = 56 * 1024 * 1024

FF_CHUNK = D_FF // 2
FFN1_TM = 512
FFN2_TM = 512
PROJ_MAIN = 9 * D_MODEL
COL_X, COL_Q, COL_GM = 0, 1, 2
COL_BC, COL_I, COL_GH = 3, 4, 5
COL_Z, COL_LOGF, COL_OG = 6, 7, 8
W_DT_AT = M_INNER + M_CONV_DIM
PROJ_PIECES = ((COL_X, ("a", M_INNER)), (COL_Q, ("b", 0)), (COL_GM, ("b", 4 * D_MODEL)),
               (COL_BC, ("a", 2 * M_INNER)), (COL_I, ("b", 2 * D_MODEL)), (COL_GH, ("b", 5 * D_MODEL)),
               (COL_Z, ("a", 0)), (COL_LOGF, ("b", D_MODEL)), (COL_OG, ("b", 3 * D_MODEL)))

SSD_CHUNK = 128
HGRN_BLOCK = 128
MIXER_TL = 256
MIXER_INTERLEAVE = (5, 20)
SAMPLE_ROWS = 8
SAMPLE_FIRST = M_CONV - 1
SAMPLE_BATCH_BLOCK = 8


def _sigmoid(x):
    return 1.0 / (1.0 + jnp.exp(-x))


def _silu(x):
    return x * _sigmoid(x)


def _softplus(x):
    return jnp.maximum(x, 0.0) + jnp.log(1.0 + jnp.exp(-jnp.abs(x)))


def _dot(a, b):
    return jnp.dot(a, b, preferred_element_type=F32)


def _dot_nt(a, b):
    return lax.dot_general(a, b, (((1,), (1,)), ((), ())), preferred_element_type=F32)


def _pad_rows(a):
    q = a.shape[0]
    if q == LANES:
        return a
    return jnp.concatenate([a, jnp.zeros((LANES - q, a.shape[1]), a.dtype)], axis=0)


def _chunk_cumsum(x, ch):
    rows, cols = x.shape
    g = rows // SUBLANES
    y = x.reshape(g, SUBLANES, cols)
    sub = lax.broadcasted_iota(jnp.int32, (g, SUBLANES, cols), 1)
    s = 1
    while s < SUBLANES:
        y = y + jnp.where(sub >= s, pltpu.roll(y, s, 1), 0.0)
        s *= 2
    per = ch // SUBLANES
    if per > 1:
        y4 = y.reshape(g // per, per, SUBLANES, cols)
        carry, outs = None, []
        for j in range(per):
            yj = y4[:, j]
            outs.append(yj if carry is None else yj + carry)
            tot = jnp.broadcast_to(yj[:, SUBLANES - 1:SUBLANES, :], yj.shape)
            carry = tot if carry is None else carry + tot
        y = jnp.stack(outs, axis=1).reshape(g, SUBLANES, cols)
    return y.reshape(rows, cols)


def _layer_norm(y, g, b):
    mu = jnp.mean(y, axis=-1, keepdims=True)
    yc = y - mu
    var = jnp.mean(yc * yc, axis=-1, keepdims=True)
    return yc * lax.rsqrt(var + EPS) * g + b


def _swiglu(x, wg_ref, wu_ref, wd_ref):
    xb = x.astype(wg_ref.dtype)
    acc = None
    for c in range(D_FF // FF_CHUNK):
        sl = slice(c * FF_CHUNK, (c + 1) * FF_CHUNK)
        hg = _dot(xb, wg_ref[:, sl])
        hu = _dot(xb, wu_ref[:, sl])
        act = (_silu(hg) * hu).astype(wd_ref.dtype)
        part = _dot(act, wd_ref[sl, :])
        acc = part if acc is None else acc + part
    return acc


def _ffn_ln_kernel(x_ref, wg_ref, wu_ref, wd_ref, g_ref, b_ref, o_ref):
    x = x_ref[...]
    y = ALPHA * x + 0.5 * _swiglu(x, wg_ref, wu_ref, wd_ref)
    o_ref[...] = _layer_norm(y, g_ref[...], b_ref[...])


def _activate_piece(c, p, lbp_ref, sub=slice(0, D_MODEL)):
    if c in (COL_GM, COL_GH):
        return _sigmoid(p)
    if c in (COL_Z, COL_OG):
        return _silu(p)
    if c == COL_LOGF:
        return _hgrn_gates(p, lbp_ref, sub)
    return p


def _proj_piece(c, xb, wa_ref, wb_ref, lbp_ref, sub=slice(0, D_MODEL), activate=True):
    part, at = dict(PROJ_PIECES)[c]
    w_ref = wa_ref if part == "a" else wb_ref
    p = _dot(xb, w_ref[:, at + sub.start:at + sub.stop])
    return _activate_piece(c, p, lbp_ref, sub) if activate else p


def _proj_dt(xb, wdt_ref, dtb_ref):
    return _softplus(_dot(xb, wdt_ref[...]) + dtb_ref[...])


def _in_proj_kernel(x_ref, wa_ref, wb_ref, wdt_ref, dtb_ref, lbp_ref, o_ref, odt_ref, k_ref):
    xb = x_ref[...].astype(BF16)
    for c, _ in PROJ_PIECES:
        cols = slice(c * D_MODEL, (c + 1) * D_MODEL)
        p = _proj_piece(c, xb, wa_ref, wb_ref, lbp_ref)
        if c == COL_LOGF:
            o_ref[:, cols], k_ref[...] = p
        else:
            o_ref[:, cols] = p
    odt_ref[...] = _proj_dt(xb, wdt_ref, dtb_ref)


def _merge_ffn_kernel(x1_ref, mix_ref, wo_ref, g2_ref, b2_ref,
                      wg_ref, wu_ref, wd_ref, g3_ref, b3_ref, o_ref):
    x2 = _layer_norm(ALPHA * x1_ref[...] + _dot(mix_ref[...], wo_ref[...]),
                     g2_ref[...], b2_ref[...])
    y = ALPHA * x2 + 0.5 * _swiglu(x2, wg_ref, wu_ref, wd_ref)
    o_ref[...] = _layer_norm(y, g3_ref[...], b3_ref[...])


def _ssd_streams(chunks, h_prev, a_row, stage_major):
    q = chunks[0][1].shape[0]
    n = len(chunks)
    row_i = lax.broadcasted_iota(jnp.int32, (q, LANES), 0)
    col_i = lax.broadcasted_iota(jnp.int32, (q, LANES), 1)
    causal = col_i <= row_i
    head_blk = jnp.right_shift(lax.broadcasted_iota(jnp.int32, (q, GROUP_COLS), 1),
                               M_HEAD_DIM.bit_length() - 1)
    groups = [slice(g * M_STATE, (g + 1) * M_STATE) for g in range(M_GROUPS)]
    rows_r = [slice(r * M_HEAD_DIM, (r + 1) * M_HEAD_DIM) for r in range(HEADS_PER_GROUP)]

    pre, ops = {}, {}

    def prepare(i):
        _, xm, bm, cm, dt = chunks[i]
        a_cum = _chunk_cumsum(dt * a_row, q)
        a_last = a_cum[q - 1:q, :]
        to_end = jnp.exp(a_last - a_cum) * dt
        pre[i] = dict(
            a_cum=a_cum, e_last=jnp.exp(a_last), e_cum=jnp.exp(a_cum),
            a_cum_t=jnp.transpose(_pad_rows(a_cum)),
            dt_t=jnp.transpose(_pad_rows(dt)),
            to_end_t=jnp.transpose(_pad_rows(to_end)))
        for g in range(M_GROUPS):
            xg = _pad_rows(xm[:, g * GROUP_COLS:(g + 1) * GROUP_COLS])
            ops[i, g] = dict(bg=_pad_rows(bm[:, groups[g]]).astype(BF16),
                             cg=cm[:, groups[g]].astype(BF16),
                             xg_b=xg.astype(BF16), xg_t=jnp.transpose(xg))

    def mm_cb(i, g):
        return _dot_nt(ops[i, g]["cg"], ops[i, g]["bg"])

    def state_lhs(i, g, r):
        h = g * HEADS_PER_GROUP + r
        return (ops[i, g]["xg_t"][rows_r[r], :] * pre[i]["to_end_t"][h:h + 1, :]).astype(BF16)

    def mm_state(i, g, r):
        return _dot(state_lhs(i, g, r), ops[i, g]["bg"])

    def intra_lhs(i, g, r, cb):
        h = g * HEADS_PER_GROUP + r
        seg = pre[i]["a_cum"][:, h:h + 1] - pre[i]["a_cum_t"][h:h + 1, :]
        decay = jnp.where(causal, jnp.exp(jnp.where(causal, seg, 0.0)), 0.0)
        return (cb * decay * pre[i]["dt_t"][h:h + 1, :]).astype(BF16)

    def mm_intra(i, g, r, cb):
        return _dot(intra_lhs(i, g, r, cb), ops[i, g]["xg_b"])

    def mm_inter(i, g, blocks):
        hg = jnp.concatenate(blocks[g * HEADS_PER_GROUP:(g + 1) * HEADS_PER_GROUP], axis=0)
        return _dot_nt(ops[i, g]["cg"], hg.astype(BF16))

    def combine(i, g, parts, y_inter):
        acc = jnp.zeros((q, GROUP_COLS), F32)
        for r in range(HEADS_PER_GROUP):
            h = g * HEADS_PER_GROUP + r
            acc = jnp.where(head_blk == r, parts[r] + y_inter * pre[i]["e_cum"][:, h:h + 1], acc)
        return acc

    cur = {seq: list(blocks) for seq, blocks in h_prev.items()}
    ys = []
    if not stage_major:
        for i, chunk in enumerate(chunks):
            prepare(i)
            enter = cur[chunk[0]]
            cbs = [mm_cb(i, g) for g in range(M_GROUPS)]
            y_inters = [mm_inter(i, g, enter) for g in range(M_GROUPS)]
            yield
            cols, new = [], []
            for g in range(M_GROUPS):
                acc = jnp.zeros((q, GROUP_COLS), F32)
                for r in range(HEADS_PER_GROUP):
                    h = g * HEADS_PER_GROUP + r
                    y_h = mm_intra(i, g, r, cbs[g]) + y_inters[g] * pre[i]["e_cum"][:, h:h + 1]
                    acc = jnp.where(head_blk == r, y_h, acc)
                    new.append(pre[i]["e_last"][:, h:h + 1] * enter[h] + mm_state(i, g, r))
                    yield
                cols.append(acc)
            cur[chunk[0]] = new
            ys.append(jnp.concatenate(cols, axis=1))
        return ys, cur
    for i in range(n):
        prepare(i)
    cb = {(i, g): mm_cb(i, g) for i in range(n) for g in range(M_GROUPS)}
    st = {(i, g, r): mm_state(i, g, r)
          for i in range(n) for g in range(M_GROUPS) for r in range(HEADS_PER_GROUP)}
    part = {(i, g, r): mm_intra(i, g, r, cb[i, g])
            for i in range(n) for g in range(M_GROUPS) for r in range(HEADS_PER_GROUP)}
    enter = []
    for i, chunk in enumerate(chunks):
        seq = chunk[0]
        enter.append(list(cur[seq]))
        cur[seq] = [pre[i]["e_last"][:, h:h + 1] * cur[seq][h]
                    + st[i, h // HEADS_PER_GROUP, h % HEADS_PER_GROUP] for h in range(M_HEADS)]
    for i in range(n):
        cols = []
        for g in range(M_GROUPS):
            y_inter = mm_inter(i, g, enter[i])
            cols.append(combine(i, g, [part[i, g, r] for r in range(HEADS_PER_GROUP)], y_inter))
        ys.append(jnp.concatenate(cols, axis=1))
    return ys, cur


def _ssd_gated_norm(y_ssd, xm, z_act, dskip, mnw):
    g = (y_ssd + dskip * xm) * z_act
    outs = []
    for k in range(M_GROUPS):
        gk = g[:, k * GROUP_COLS:(k + 1) * GROUP_COLS]
        outs.append(gk * lax.rsqrt(jnp.mean(gk * gk, axis=-1, keepdims=True) + EPS))
    return jnp.concatenate(outs, axis=1) * mnw


def _state_blocks(h):
    return [h[k * M_HEAD_DIM:(k + 1) * M_HEAD_DIM, :] for k in range(M_HEADS)]


def _conv_silu(u, prev8, cw, cbias):
    tl = u.shape[0]
    ext = jnp.concatenate([prev8, u], axis=0)
    conv = cbias + cw[M_CONV - 1:M_CONV, :] * u
    for j in range(1, M_CONV):
        shifted = pltpu.roll(ext, j, 0)[SUBLANES:SUBLANES + tl, :]
        conv = conv + cw[M_CONV - 1 - j:M_CONV - j, :] * shifted
    return _silu(conv)


def _drain(gen):
    try:
        while True:
            next(gen)
    except StopIteration as stop:
        return stop.value


class _Stream:
    def __init__(self, gen):
        self.gen, self.done, self.value = gen, False, None

    def step(self, n):
        for _ in range(n):
            if self.done:
                return
            try:
                next(self.gen)
            except StopIteration as stop:
                self.done, self.value = True, stop.value


def _proj_mixer_prompt_kernel(x1_ref, wa_ref, wb_ref, wdt_ref, dtb_ref, lbp_ref, cw_ref, cb_ref, alog_ref,
                              dskip_ref, mnw_ref, wmo_ref, hnw_ref, who_ref,
                              mix_ref, conv_out_ref, ssm_out_ref, s_out_ref, prev_scr, h_scr, s_scr):
    t = pl.program_id(1)
    tl = x1_ref.shape[1]

    @pl.when(t == 0)
    def _():
        prev_scr[...] = jnp.zeros_like(prev_scr)
        h_scr[...] = jnp.zeros_like(h_scr)
        s_scr[...] = jnp.zeros_like(s_scr)

    xb = x1_ref[0].astype(BF16)
    piece = lambda c: _proj_piece(c, xb, wa_ref, wb_ref, lbp_ref)
    x_raw, bc_raw = piece(COL_X), piece(COL_BC)
    dt = _proj_dt(xb, wdt_ref, dtb_ref)
    tail = jnp.concatenate([x_raw[tl - SUBLANES:tl, :], bc_raw[tl - SUBLANES:tl, :]], axis=1)
    xm = _conv_silu(x_raw, prev_scr[:, :M_INNER], cw_ref[:, :M_INNER], cb_ref[:, :M_INNER])
    bc = _conv_silu(bc_raw, prev_scr[:, M_INNER:], cw_ref[:, M_INNER:], cb_ref[:, M_INNER:])
    prev_scr[...] = tail
    bm = bc[:, :M_GROUPS * M_STATE]
    cm = bc[:, M_GROUPS * M_STATE:]
    a_row = -jnp.exp(alog_ref[...])
    chunks = []
    for c in range(tl // SSD_CHUNK):
        rs = slice(c * SSD_CHUNK, (c + 1) * SSD_CHUNK)
        chunks.append((0, xm[rs], bm[rs], cm[rs], dt[rs]))
    got = {}

    def remaining_pieces():
        for c in (COL_LOGF, COL_Q, COL_I, COL_Z, COL_OG, COL_GM, COL_GH):
            got[c] = _proj_piece(c, xb, wa_ref, wb_ref, lbp_ref, activate=(c == COL_LOGF))
            yield

    proj = _Stream(remaining_pieces())

    def need(c):
        while c not in got:
            proj.step(1)
        return got[c] if c == COL_LOGF else _activate_piece(c, got[c], lbp_ref)

    def ssd_branch():
        ys, h_new = yield from _ssd_streams(chunks, {0: _state_blocks(h_scr[...])}, a_row,
                                            stage_major=False)
        gn = _ssd_gated_norm(jnp.concatenate(ys, axis=0), xm, need(COL_Z), dskip_ref[...], mnw_ref[...])
        ym = _dot(gn.astype(BF16), wmo_ref[...])
        yield
        return ym, h_new

    def hgrn_branch():
        (log_f, k), q, v = need(COL_LOGF), need(COL_Q), need(COL_I)
        blocks = []
        for b in range(tl // HGRN_BLOCK):
            rs = slice(b * HGRN_BLOCK, (b + 1) * HGRN_BLOCK)
            blocks.append((0, q[rs], log_f[rs], k[rs], v[rs]))
        outs, s_new = yield from _hgrn_streams(blocks, {0: [s_scr[h] for h in range(H_HEADS)]}, H_CHUNK)
        o = jnp.concatenate(outs, axis=0) * hnw_ref[...] * need(COL_OG)
        yh = _dot(o.astype(BF16), who_ref[...])
        yield
        return yh, s_new

    ssd, hgrn = _Stream(ssd_branch()), _Stream(hgrn_branch())
    n_ssd, n_hgrn = MIXER_INTERLEAVE
    while COL_I not in got:
        proj.step(1)
        ssd.step(n_ssd)
    while not (proj.done and ssd.done and hgrn.done):
        proj.step(1)
        ssd.step(n_ssd)
        hgrn.step(n_hgrn)
    (ym, h_new), (yh, s_new) = ssd.value, hgrn.value
    for h in range(M_HEADS):
        h_scr[h * M_HEAD_DIM:(h + 1) * M_HEAD_DIM, :] = h_new[0][h]
    for h in range(H_HEADS):
        s_scr[h] = s_new[0][h]
    mix_ref[0] = need(COL_GM) * ym + need(COL_GH) * yh

    @pl.when(t == pl.num_programs(1) - 1)
    def _():
        conv_out_ref[0] = tail
        ssm_out_ref[0] = h_scr[...]
        for h in range(H_HEADS):
            s_out_ref[0, h] = jnp.transpose(s_scr[h])


def _tile_rows(tokens, before):
    n, c = tokens.shape
    parts = [tokens, jnp.zeros((SAMPLE_ROWS - n - before, c), tokens.dtype)]
    if before:
        parts.insert(0, jnp.zeros((before, c), tokens.dtype))
    return jnp.concatenate(parts, axis=0)


def _ssd_sample_kernel(x_ref, bc_ref, z_ref, dt_ref, cs_ref, h0_ref, cw_ref, cb_ref, alog_ref,
                       dskip_ref, mnw_ref, wmo_ref, ym_ref, conv_out_ref, ssm_out_ref):
    bb = h0_ref.shape[0]
    n_tok = x_ref.shape[0] // bb
    a_row = -jnp.exp(alog_ref[...])
    chunks, zs, h_prev = [], [], {}
    for i in range(bb):
        rs = slice(i * n_tok, (i + 1) * n_tok)
        raw = jnp.concatenate([x_ref[rs, :], bc_ref[rs, :]], axis=1)
        u = jnp.concatenate([cs_ref[i], raw, jnp.zeros((1, M_CONV_DIM), F32)], axis=0)
        conv_out_ref[i] = u[n_tok:n_tok + M_CONV - 1, :]
        conv = cb_ref[...] + cw_ref[M_CONV - 1:M_CONV, :] * u
        for j in range(1, M_CONV):
            conv = conv + cw_ref[M_CONV - 1 - j:M_CONV - j, :] * pltpu.roll(u, j, 0)
        xbc = _silu(conv)
        xm = xbc[:, :M_INNER]
        bm = xbc[:, M_INNER:M_INNER + M_GROUPS * M_STATE]
        cm = xbc[:, M_INNER + M_GROUPS * M_STATE:]
        chunks.append((i, xm, bm, cm, _tile_rows(dt_ref[rs, :], SAMPLE_FIRST)))
        zs.append(_tile_rows(z_ref[rs, :], SAMPLE_FIRST))
        h_prev[i] = _state_blocks(h0_ref[i])
    ys, h_new = _drain(_ssd_streams(chunks, h_prev, a_row, stage_major=True))
    gns = []
    for i in range(bb):
        for h in range(M_HEADS):
            ssm_out_ref[i, h * M_HEAD_DIM:(h + 1) * M_HEAD_DIM, :] = h_new[i][h]
        gns.append(_ssd_gated_norm(ys[i], chunks[i][1], zs[i], dskip_ref[...], mnw_ref[...]))
    ym = _dot(jnp.concatenate(gns, axis=0).astype(BF16), wmo_ref[...])
    for i in range(bb):
        ym_ref[i * n_tok:(i + 1) * n_tok, :] = ym[i * SAMPLE_ROWS + SAMPLE_FIRST:
                                                  i * SAMPLE_ROWS + SAMPLE_FIRST + n_tok, :]


def _hgrn_gates(f_raw, lbp_ref, cols=slice(0, H_HEADS * H_KEY)):
    p0 = lbp_ref[0:1, cols]
    p1 = lbp_ref[1:2, cols]
    m = jnp.maximum(p0, p1)
    e0 = jnp.exp(p0 - m)
    e1 = jnp.exp(p1 - m)
    lb = e0 / (e0 + e1)
    log_f = jnp.log(lb + (1.0 - lb) * _sigmoid(f_raw))
    k = (1.0 - lb) * _sigmoid(-f_raw)
    return log_f, k


def _hgrn_streams(blocks, s_prev, ch):
    rows = blocks[0][1].shape[0]
    n = len(blocks)
    shift = ch.bit_length() - 1
    n_chunks = rows // ch
    row_i = lax.broadcasted_iota(jnp.int32, (rows, LANES), 0)
    col_i = lax.broadcasted_iota(jnp.int32, (rows, LANES), 1)
    causal = (jnp.right_shift(row_i, shift) == jnp.right_shift(col_i, shift)) & (col_i <= row_i)
    chunk_t = jnp.right_shift(lax.broadcasted_iota(jnp.int32, (LANES, LANES), 1), shift)
    heads = [slice(h * H_KEY, (h + 1) * H_KEY) for h in range(H_HEADS)]
    chunks = [slice(c * ch, (c + 1) * ch) for c in range(n_chunks)]
    items = [(i, h) for i in range(n) for h in range(H_HEADS)]

    pre = []
    for (_, q, log_f, k, v) in blocks:
        b_cum = _chunk_cumsum(log_f, ch)
        lasts = [b_cum[c * ch + ch - 1:c * ch + ch, :] for c in range(n_chunks)]
        b_last = jnp.concatenate([jnp.broadcast_to(l, (ch, l.shape[1])) for l in lasts], axis=0)
        pre.append(dict(qd=q * jnp.exp(b_cum), kd=k * jnp.exp(-b_cum),
                        k_end=k * jnp.exp(b_last - b_cum), lasts=lasts, v=v))
    qh = {(i, h): pre[i]["qd"][:, heads[h]].astype(BF16) for (i, h) in items}
    vh = {(i, h): _pad_rows(pre[i]["v"][:, heads[h]]) for (i, h) in items}
    sc, ds, y_intra, y_inter = {}, {}, {}, {}
    for (i, h) in items:
        sc[i, h] = _dot_nt(qh[i, h], _pad_rows(pre[i]["kd"][:, heads[h]]).astype(BF16))
        yield
    for (i, h) in items:
        v_t = jnp.transpose(vh[i, h])
        ke = _pad_rows(pre[i]["k_end"][:, heads[h]]).astype(BF16)
        ds[i, h] = []
        for c in range(n_chunks):
            v_tc = v_t if n_chunks == 1 else jnp.where(chunk_t == c, v_t, 0.0)
            ds[i, h].append(_dot(v_tc.astype(BF16), ke))
            yield
    for (i, h) in items:
        y_intra[i, h] = _dot(jnp.where(causal, sc[i, h], 0.0).astype(BF16), vh[i, h].astype(BF16))
        yield
    cur = {seq: list(states) for seq, states in s_prev.items()}
    enter = {}
    for i, blk in enumerate(blocks):
        seq = blk[0]
        for h in range(H_HEADS):
            s = cur[seq][h]
            per_chunk = []
            for c in range(n_chunks):
                per_chunk.append(s)
                s = jnp.exp(pre[i]["lasts"][c][:, heads[h]]) * s + ds[i, h][c]
            enter[i, h] = per_chunk
            cur[seq][h] = s
    for (i, h) in items:
        y_inter[i, h] = []
        for c in range(n_chunks):
            y_inter[i, h].append(_dot_nt(qh[i, h][chunks[c]], enter[i, h][c].astype(BF16)))
            yield
    outs = []
    for i in range(n):
        cols = []
        for h in range(H_HEADS):
            pieces = [y_intra[i, h][chunks[c]] + y_inter[i, h][c] for c in range(n_chunks)]
            o_h = pieces[0] if n_chunks == 1 else jnp.concatenate(pieces, axis=0)
            cols.append(o_h * lax.rsqrt(jnp.mean(o_h * o_h, axis=-1, keepdims=True) + EPS))
        outs.append(jnp.concatenate(cols, axis=1))
    return outs, cur


def _hgrn_sample_kernel(q_ref, logf_ref, k_ref, i_ref, og_ref, ym_ref, gm_ref, gh_ref, s0_ref,
                        hnw_ref, who_ref, mix_ref, s_out_ref):
    bb = s0_ref.shape[0]
    n_tok = q_ref.shape[0] // bb
    rows = [slice(i * n_tok, (i + 1) * n_tok) for i in range(bb)]
    blocks, s_prev = [], {}
    for i in range(bb):
        blocks.append((i, _tile_rows(q_ref[rows[i], :], 0), _tile_rows(logf_ref[rows[i], :], 0),
                       _tile_rows(k_ref[rows[i], :], 0), _tile_rows(i_ref[rows[i], :], 0)))
        s_prev[i] = [jnp.transpose(s0_ref[i, h]) for h in range(H_HEADS)]
    outs, s_new = _drain(_hgrn_streams(blocks, s_prev, SAMPLE_ROWS))
    os_ = []
    for i in range(bb):
        for h in range(H_HEADS):
            s_out_ref[i, h] = jnp.transpose(s_new[i][h])
        os_.append(outs[i] * hnw_ref[...] * _tile_rows(og_ref[rows[i], :], 0))
    yh = _dot(jnp.concatenate(os_, axis=0).astype(BF16), who_ref[...])
    for i in range(bb):
        mix_ref[rows[i], :] = (gm_ref[rows[i], :] * ym_ref[rows[i], :]
                               + gh_ref[rows[i], :] * yh[i * SAMPLE_ROWS:i * SAMPLE_ROWS + n_tok, :])


def _resident(shape):
    nd = len(shape)
    return pl.BlockSpec(shape, lambda *_: (0,) * nd, pipeline_mode=pl.Buffered(1))


def _params(semantics):
    return pltpu.CompilerParams(dimension_semantics=semantics, vmem_limit_bytes=VMEM_LIMIT)


def _row_tile(n_rows, want):
    tm = min(want, n_rows)
    assert n_rows % tm == 0
    return tm


def _ffn_ln(x, wg, wu, wd, g, b):
    n = x.shape[0]
    tm = _row_tile(n, FFN1_TM)
    row = pl.BlockSpec((tm, D_MODEL), lambda i: (i, 0))
    return pl.pallas_call(
        _ffn_ln_kernel,
        grid=(n // tm,),
        in_specs=[row, _resident(wg.shape), _resident(wu.shape), _resident(wd.shape),
                  _resident(g.shape), _resident(b.shape)],
        out_specs=row,
        out_shape=jax.ShapeDtypeStruct((n, D_MODEL), F32),
        compiler_params=_params(("parallel",)),
        name="ffn_ln",
    )(x, wg, wu, wd, g, b)


def _merge_ffn(x1, mix, *merge_w):
    n = x1.shape[0]
    tm = _row_tile(n, FFN2_TM)
    row = pl.BlockSpec((tm, D_MODEL), lambda i: (i, 0))
    return pl.pallas_call(
        _merge_ffn_kernel,
        grid=(n // tm,),
        in_specs=[row, row] + [_resident(w.shape) for w in merge_w],
        out_specs=row,
        out_shape=jax.ShapeDtypeStruct((n, D_MODEL), F32),
        compiler_params=_params(("parallel",)),
        name="merge_ffn",
    )(x1, mix, *merge_w)


def _in_proj(x1, proj_w):
    n = x1.shape[0]
    tm = _row_tile(n, 256)
    row = lambda width: pl.BlockSpec((tm, width), lambda i: (i, 0))
    return pl.pallas_call(
        _in_proj_kernel,
        grid=(n // tm,),
        in_specs=[row(D_MODEL)] + [_resident(w.shape) for w in proj_w],
        out_specs=[row(PROJ_MAIN), row(LANES), row(D_MODEL)],
        out_shape=[jax.ShapeDtypeStruct((n, PROJ_MAIN), F32),
                   jax.ShapeDtypeStruct((n, LANES), F32),
                   jax.ShapeDtypeStruct((n, D_MODEL), F32)],
        compiler_params=_params(("parallel",)),
        name="in_proj",
    )(x1, *proj_w)


def _proj_mixer_prompt(x1_3, proj_w, conv_wb, ssd_w, hgrn_w):
    bsz, length, _ = x1_3.shape
    tl = MIXER_TL
    rows = pl.BlockSpec((1, tl, D_MODEL), lambda b, t: (b, t, 0))
    once = lambda *shape: pl.BlockSpec((1,) + shape, lambda b, t: (b,) + (0,) * len(shape))
    weights = tuple(proj_w) + tuple(conv_wb) + tuple(ssd_w) + tuple(hgrn_w)
    return pl.pallas_call(
        _proj_mixer_prompt_kernel,
        grid=(bsz, length // tl),
        in_specs=[rows] + [_resident(w.shape) for w in weights],
        out_specs=[rows, once(SUBLANES, M_CONV_DIM), once(M_INNER, M_STATE),
                   once(H_HEADS, H_KEY, H_VAL)],
        out_shape=[jax.ShapeDtypeStruct((bsz, length, D_MODEL), F32),
                   jax.ShapeDtypeStruct((bsz, SUBLANES, M_CONV_DIM), F32),
                   jax.ShapeDtypeStruct((bsz, M_INNER, M_STATE), F32),
                   jax.ShapeDtypeStruct((bsz, H_HEADS, H_KEY, H_VAL), F32)],
        scratch_shapes=[pltpu.VMEM((SUBLANES, M_CONV_DIM), F32),
                        pltpu.VMEM((M_INNER, M_STATE), F32),
                        pltpu.VMEM((H_HEADS, H_VAL, H_KEY), F32)],
        compiler_params=_params(("parallel", "arbitrary")),
        name="proj_mixer_prompt",
    )(x1_3, *weights)


def _ssd_sample(proj, dt, conv0, ssm0, conv_wb, ssd_w):
    bsz = ssm0.shape[0]
    bb = SAMPLE_BATCH_BLOCK
    rows = bb * (proj.shape[0] // bsz)
    col = lambda width, c: pl.BlockSpec((rows, width), lambda b: (b, c))
    seq = lambda *shape: pl.BlockSpec((bb,) + shape, lambda b: (b,) + (0,) * len(shape))
    weights = tuple(conv_wb) + tuple(ssd_w)
    return pl.pallas_call(
        _ssd_sample_kernel,
        grid=(bsz // bb,),
        in_specs=[col(D_MODEL, COL_X), col(D_MODEL, COL_BC), col(D_MODEL, COL_Z), col(LANES, 0),
                  seq(M_CONV - 1, M_CONV_DIM), seq(M_INNER, M_STATE)]
                 + [_resident(w.shape) for w in weights],
        out_specs=[col(D_MODEL, 0), seq(M_CONV - 1, M_CONV_DIM), seq(M_INNER, M_STATE)],
        out_shape=[jax.ShapeDtypeStruct((proj.shape[0], D_MODEL), F32),
                   jax.ShapeDtypeStruct((bsz, M_CONV - 1, M_CONV_DIM), F32),
                   jax.ShapeDtypeStruct((bsz, M_INNER, M_STATE), F32)],
        compiler_params=_params(("parallel",)),
        name="ssd_sample",
    )(proj, proj, proj, dt, conv0, ssm0, *weights)


def _hgrn_sample(proj, kk, ym, s0, hgrn_w):
    bsz = s0.shape[0]
    bb = SAMPLE_BATCH_BLOCK
    rows = bb * (proj.shape[0] // bsz)
    col = lambda c: pl.BlockSpec((rows, D_MODEL), lambda b: (b, c))
    state = pl.BlockSpec((bb, H_HEADS, H_KEY, H_VAL), lambda b: (b, 0, 0, 0))
    return pl.pallas_call(
        _hgrn_sample_kernel,
        grid=(bsz // bb,),
        in_specs=[col(COL_Q), col(COL_LOGF), col(0), col(COL_I), col(COL_OG), col(0), col(COL_GM),
                  col(COL_GH), state] + [_resident(w.shape) for w in hgrn_w],
        out_specs=[col(0), state],
        out_shape=[jax.ShapeDtypeStruct((proj.shape[0], D_MODEL), F32),
                   jax.ShapeDtypeStruct((bsz, H_HEADS, H_KEY, H_VAL), F32)],
        compiler_params=_params(("parallel",)),
        name="hgrn_sample",
    )(proj, proj, kk, proj, proj, ym, proj, proj, s0, *hgrn_w)


def _pad_lanes(v):
    return jnp.pad(v, (0, LANES - v.shape[0])).reshape(1, LANES)


def kernel(x_prompt, x_sample, state_conv, state_ssm, state_hgrn, ffn1_w_gate, ffn1_w_up, ffn1_w_down, ln1_g, ln1_b, w_in, conv_w, conv_b, dt_bias, a_log, d_skip, m_norm_w, w_m_out, hgrn_lb_param, h_norm_w, w_h_out, w_o, ln2_g, ln2_b, ffn2_w_gate, ffn2_w_up, ffn2_w_down, ln3_g, ln3_b):
    assert w_in.shape[0] == 1, "single trunk layer"
    bp, lp, _ = x_prompt.shape
    bs, ls, _ = x_sample.shape
    assert ls == SAMPLE_ROWS - M_CONV

    row = lambda v: v[0].reshape(1, -1)
    f1 = (ffn1_w_gate[0], ffn1_w_up[0], ffn1_w_down[0])
    f2 = (ffn2_w_gate[0].astype(BF16), ffn2_w_up[0].astype(BF16), ffn2_w_down[0].astype(BF16))
    wi = w_in[0]
    dt_end = W_DT_AT + M_HEADS
    w_dt = jnp.pad(wi[:, W_DT_AT:dt_end], ((0, 0), (0, LANES - M_HEADS))).astype(BF16)
    proj_w = (wi[:, :W_DT_AT].astype(BF16), wi[:, dt_end:].astype(BF16), w_dt,
              _pad_lanes(dt_bias[0]), hgrn_lb_param)
    conv_wb = (conv_w[0], row(conv_b))
    ssd_w = (_pad_lanes(a_log[0]), jnp.repeat(d_skip[0], M_HEAD_DIM).reshape(1, M_INNER),
             row(m_norm_w), w_m_out[0].astype(BF16))
    hgrn_w = (row(h_norm_w), w_h_out[0].astype(BF16))
    merge_w = (w_o[0], row(ln2_g), row(ln2_b)) + f2 + (row(ln3_g), row(ln3_b))

    x1 = _ffn_ln(x_prompt.reshape(bp * lp, D_MODEL), *f1, row(ln1_g), row(ln1_b))
    mix, conv_p, ssm_p, hg_p = _proj_mixer_prompt(x1.reshape(bp, lp, D_MODEL), proj_w, conv_wb,
                                                  ssd_w, hgrn_w)
    y_prompt = _merge_ffn(x1, mix.reshape(-1, D_MODEL), *merge_w).reshape(bp, lp, D_MODEL)
    new_conv_p = conv_p[:, SUBLANES - (M_CONV - 1):, :][None]
    new_ssm_p = ssm_p.reshape(1, bp, M_HEADS, M_HEAD_DIM, M_STATE)
    new_hg_p = hg_p[None]

    x1s = _ffn_ln(x_sample.reshape(bs * ls, D_MODEL), *f1, row(ln1_g), row(ln1_b))
    projs, dts, kks = _in_proj(x1s, proj_w)
    yms, conv_s, ssm_s = _ssd_sample(projs, dts, state_conv[0],
                                     state_ssm[0].reshape(bs, M_INNER, M_STATE), conv_wb, ssd_w)
    mixs, hg_s = _hgrn_sample(projs, kks, yms, state_hgrn[0], hgrn_w)
    y_sample = _merge_ffn(x1s, mixs, *merge_w).reshape(bs, ls, D_MODEL)
    new_conv_s = conv_s[None]
    new_ssm_s = ssm_s.reshape(1, bs, M_HEADS, M_HEAD_DIM, M_STATE)
    new_hg_s = hg_s[None]

    return (y_prompt, y_sample, new_conv_p, new_ssm_p, new_hg_p, new_conv_s, new_ssm_s, new_hg_s)
```

```python
import jax
import jax.numpy as jnp
from jax import lax
from jax.experimental import pallas as pl
from jax.experimental.pallas import tpu as pltpu

F32 = jnp.float32
BF16 = jnp.bfloat16

D_MODEL = 1024
D_FF = 2816
M_HEADS = 16
M_HEAD_DIM = 64
M_GROUPS = 4
M_STATE = 128
M_INNER = M_HEADS * M_HEAD_DIM
M_CONV = 4
M_CONV_DIM = M_INNER + 2 * M_GROUPS * M_STATE
HEADS_PER_GROUP = M_HEADS // M_GROUPS
GROUP_COLS = HEADS_PER_GROUP * M_HEAD_DIM
H_HEADS = 8
H_KEY = 128
H_VAL = 128
H_CHUNK = 32
ALPHA = 2.0 ** 0.25
EPS = 1e-5

LANES = 128
SUBLANES = 8
VMEM_LIMIT = 56 * 1024 * 1024

FF_CHUNK = D_FF // 2
FFN1_TM = 512
FFN2_TM = 512
PROJ_MAIN = 9 * D_MODEL
COL_X, COL_Q, COL_GM = 0, 1, 2
COL_BC, COL_I, COL_GH = 3, 4, 5
COL_Z, COL_LOGF, COL_OG = 6, 7, 8
W_DT_AT = M_INNER + M_CONV_DIM
PROJ_PIECES = ((COL_X, ("a", M_INNER)), (COL_Q, ("b", 0)), (COL_GM, ("b", 4 * D_MODEL)),
               (COL_BC, ("a", 2 * M_INNER)), (COL_I, ("b", 2 * D_MODEL)), (COL_GH, ("b", 5 * D_MODEL)),
               (COL_Z, ("a", 0)), (COL_LOGF, ("b", D_MODEL)), (COL_OG, ("b", 3 * D_MODEL)))

SSD_CHUNK = 128
HGRN_BLOCK = 128
MIXER_TL = 256
MIXER_INTERLEAVE = (5, 20)
SAMPLE_ROWS = 8
SAMPLE_FIRST = M_CONV - 1
SAMPLE_BATCH_BLOCK = 8


def _sigmoid(x):
    return 1.0 / (1.0 + jnp.exp(-x))


def _silu(x):
    return x * _sigmoid(x)


def _softplus(x):
    return jnp.maximum(x, 0.0) + jnp.log(1.0 + jnp.exp(-jnp.abs(x)))


def _dot(a, b):
    return jnp.dot(a, b, preferred_element_type=F32)


def _dot_nt(a, b):
    return lax.dot_general(a, b, (((1,), (1,)), ((), ())), preferred_element_type=F32)


def _pad_rows(a):
    q = a.shape[0]
    if q == LANES:
        return a
    return jnp.concatenate([a, jnp.zeros((LANES - q, a.shape[1]), a.dtype)], axis=0)


def _chunk_cumsum(x, ch):
    rows, cols = x.shape
    g = rows // SUBLANES
    y = x.reshape(g, SUBLANES, cols)
    sub = lax.broadcasted_iota(jnp.int32, (g, SUBLANES, cols), 1)
    s = 1
    while s < SUBLANES:
        y = y + jnp.where(sub >= s, pltpu.roll(y, s, 1), 0.0)
        s *= 2
    per = ch // SUBLANES
    if per > 1:
        y4 = y.reshape(g // per, per, SUBLANES, cols)
        carry, outs = None, []
        for j in range(per):
            yj = y4[:, j]
            outs.append(yj if carry is None else yj + carry)
            tot = jnp.broadcast_to(yj[:, SUBLANES - 1:SUBLANES, :], yj.shape)
            carry = tot if carry is None else carry + tot
        y = jnp.stack(outs, axis=1).reshape(g, SUBLANES, cols)
    return y.reshape(rows, cols)


def _layer_norm(y, g, b):
    mu = jnp.mean(y, axis=-1, keepdims=True)
    yc = y - mu
    var = jnp.mean(yc * yc, axis=-1, keepdims=True)
    return yc * lax.rsqrt(var + EPS) * g + b


def _swiglu(x, wg_ref, wu_ref, wd_ref):
    xb = x.astype(wg_ref.dtype)
    acc = None
    for c in range(D_FF // FF_CHUNK):
        sl = slice(c * FF_CHUNK, (c + 1) * FF_CHUNK)
        hg = _dot(xb, wg_ref[:, sl])
        hu = _dot(xb, wu_ref[:, sl])
        act = (_silu(hg) * hu).astype(wd_ref.dtype)
        part = _dot(act, wd_ref[sl, :])
        acc = part if acc is None else acc + part
    return acc


def _ffn_ln_kernel(x_ref, wg_ref, wu_ref, wd_ref, g_ref, b_ref, o_ref):
    x = x_ref[...]
    y = ALPHA * x + 0.5 * _swiglu(x, wg_ref, wu_ref, wd_ref)
    o_ref[...] = _layer_norm(y, g_ref[...], b_ref[...])


def _activate_piece(c, p, lbp_ref, sub=slice(0, D_MODEL)):
    if c in (COL_GM, COL_GH):
        return _sigmoid(p)
    if c in (COL_Z, COL_OG):
        return _silu(p)
    if c == COL_LOGF:
        return _hgrn_gates(p, lbp_ref, sub)
    return p


def _proj_piece(c, xb, wa_ref, wb_ref, lbp_ref, sub=slice(0, D_MODEL), activate=True):
    part, at = dict(PROJ_PIECES)[c]
    w_ref = wa_ref if part == "a" else wb_ref
    p = _dot(xb, w_ref[:, at + sub.start:at + sub.stop])
    return _activate_piece(c, p, lbp_ref, sub) if activate else p


def _proj_dt(xb, wdt_ref, dtb_ref):
    return _softplus(_dot(xb, wdt_ref[...]) + dtb_ref[...])


def _in_proj_kernel(x_ref, wa_ref, wb_ref, wdt_ref, dtb_ref, lbp_ref, o_ref, odt_ref, k_ref):
    xb = x_ref[...].astype(BF16)
    for c, _ in PROJ_PIECES:
        cols = slice(c * D_MODEL, (c + 1) * D_MODEL)
        p = _proj_piece(c, xb, wa_ref, wb_ref, lbp_ref)
        if c == COL_LOGF:
            o_ref[:, cols], k_ref[...] = p
        else:
            o_ref[:, cols] = p
    odt_ref[...] = _proj_dt(xb, wdt_ref, dtb_ref)


def _merge_ffn_kernel(x1_ref, mix_ref, wo_ref, g2_ref, b2_ref,
                      wg_ref, wu_ref, wd_ref, g3_ref, b3_ref, o_ref):
    x2 = _layer_norm(ALPHA * x1_ref[...] + _dot(mix_ref[...], wo_ref[...]),
                     g2_ref[...], b2_ref[...])
    y = ALPHA * x2 + 0.5 * _swiglu(x2, wg_ref, wu_ref, wd_ref)
    o_ref[...] = _layer_norm(y, g3_ref[...], b3_ref[...])


def _ssd_streams(chunks, h_prev, a_row, stage_major):
    q = chunks[0][1].shape[0]
    n = len(chunks)
    row_i = lax.broadcasted_iota(jnp.int32, (q, LANES), 0)
    col_i = lax.broadcasted_iota(jnp.int32, (q, LANES), 1)
    causal = col_i <= row_i
    head_blk = jnp.right_shift(lax.broadcasted_iota(jnp.int32, (q, GROUP_COLS), 1),
                               M_HEAD_DIM.bit_length() - 1)
    groups = [slice(g * M_STATE, (g + 1) * M_STATE) for g in range(M_GROUPS)]
    rows_r = [slice(r * M_HEAD_DIM, (r + 1) * M_HEAD_DIM) for r in range(HEADS_PER_GROUP)]

    pre, ops = {}, {}

    def prepare(i):
        _, xm, bm, cm, dt = chunks[i]
        a_cum = _chunk_cumsum(dt * a_row, q)
        a_last = a_cum[q - 1:q, :]
        to_end = jnp.exp(a_last - a_cum) * dt
        pre[i] = dict(
            a_cum=a_cum, e_last=jnp.exp(a_last), e_cum=jnp.exp(a_cum),
            a_cum_t=jnp.transpose(_pad_rows(a_cum)),
            dt_t=jnp.transpose(_pad_rows(dt)),
            to_end_t=jnp.transpose(_pad_rows(to_end)))
        for g in range(M_GROUPS):
            xg = _pad_rows(xm[:, g * GROUP_COLS:(g + 1) * GROUP_COLS])
            ops[i, g] = dict(bg=_pad_rows(bm[:, groups[g]]).astype(BF16),
                             cg=cm[:, groups[g]].astype(BF16),
                             xg_b=xg.astype(BF16), xg_t=jnp.transpose(xg))

    def mm_cb(i, g):
        return _dot_nt(ops[i, g]["cg"], ops[i, g]["bg"])

    def state_lhs(i, g, r):
        h = g * HEADS_PER_GROUP + r
        return (ops[i, g]["xg_t"][rows_r[r], :] * pre[i]["to_end_t"][h:h + 1, :]).astype(BF16)

    def mm_state(i, g, r):
        return _dot(state_lhs(i, g, r), ops[i, g]["bg"])

    def intra_lhs(i, g, r, cb):
        h = g * HEADS_PER_GROUP + r
        seg = pre[i]["a_cum"][:, h:h + 1] - pre[i]["a_cum_t"][h:h + 1, :]
        decay = jnp.where(causal, jnp.exp(jnp.where(causal, seg, 0.0)), 0.0)
        return (cb * decay * pre[i]["dt_t"][h:h + 1, :]).astype(BF16)

    def mm_intra(i, g, r, cb):
        return _dot(intra_lhs(i, g, r, cb), ops[i, g]["xg_b"])

    def mm_inter(i, g, blocks):
        hg = jnp.concatenate(blocks[g * HEADS_PER_GROUP:(g + 1) * HEADS_PER_GROUP], axis=0)
        return _dot_nt(ops[i, g]["cg"], hg.astype(BF16))

    def combine(i, g, parts, y_inter):
        acc = jnp.zeros((q, GROUP_COLS), F32)
        for r in range(HEADS_PER_GROUP):
            h = g * HEADS_PER_GROUP + r
            acc = jnp.where(head_blk == r, parts[r] + y_inter * pre[i]["e_cum"][:, h:h + 1], acc)
        return acc

    cur = {seq: list(blocks) for seq, blocks in h_prev.items()}
    ys = []
    if not stage_major:
        for i, chunk in enumerate(chunks):
            prepare(i)
            enter = cur[chunk[0]]
            cbs = [mm_cb(i, g) for g in range(M_GROUPS)]
            y_inters = [mm_inter(i, g, enter) for g in range(M_GROUPS)]
            yield
            cols, new = [], []
            for g in range(M_GROUPS):
                acc = jnp.zeros((q, GROUP_COLS), F32)
                for r in range(HEADS_PER_GROUP):
                    h = g * HEADS_PER_GROUP + r
                    y_h = mm_intra(i, g, r, cbs[g]) + y_inters[g] * pre[i]["e_cum"][:, h:h + 1]
                    acc = jnp.where(head_blk == r, y_h, acc)
                    new.append(pre[i]["e_last"][:, h:h + 1] * enter[h] + mm_state(i, g, r))
                    yield
                cols.append(acc)
            cur[chunk[0]] = new
            ys.append(jnp.concatenate(cols, axis=1))
        return ys, cur
    for i in range(n):
        prepare(i)
    cb = {(i, g): mm_cb(i, g) for i in range(n) for g in range(M_GROUPS)}
    st = {(i, g, r): mm_state(i, g, r)
          for i in range(n) for g in range(M_GROUPS) for r in range(HEADS_PER_GROUP)}
    part = {(i, g, r): mm_intra(i, g, r, cb[i, g])
            for i in range(n) for g in range(M_GROUPS) for r in range(HEADS_PER_GROUP)}
    enter = []
    for i, chunk in enumerate(chunks):
        seq = chunk[0]
        enter.append(list(cur[seq]))
        cur[seq] = [pre[i]["e_last"][:, h:h + 1] * cur[seq][h]
                    + st[i, h // HEADS_PER_GROUP, h % HEADS_PER_GROUP] for h in range(M_HEADS)]
    for i in range(n):
        cols = []
        for g in range(M_GROUPS):
            y_inter = mm_inter(i, g, enter[i])
            cols.append(combine(i, g, [part[i, g, r] for r in range(HEADS_PER_GROUP)], y_inter))
        ys.append(jnp.concatenate(cols, axis=1))
    return ys, cur


def _ssd_gated_norm(y_ssd, xm, z_act, dskip, mnw):
    g = (y_ssd + dskip * xm) * z_act
    outs = []
    for k in range(M_GROUPS):
        gk = g[:, k * GROUP_COLS:(k + 1) * GROUP_COLS]
        outs.append(gk * lax.rsqrt(jnp.mean(gk * gk, axis=-1, keepdims=True) + EPS))
    return jnp.concatenate(outs, axis=1) * mnw


def _state_blocks(h):
    return [h[k * M_HEAD_DIM:(k + 1) * M_HEAD_DIM, :] for k in range(M_HEADS)]


def _conv_silu(u, prev8, cw, cbias):
    tl = u.shape[0]
    ext = jnp.concatenate([prev8, u], axis=0)
    conv = cbias + cw[M_CONV - 1:M_CONV, :] * u
    for j in range(1, M_CONV):
        shifted = pltpu.roll(ext, j, 0)[SUBLANES:SUBLANES + tl, :]
        conv = conv + cw[M_CONV - 1 - j:M_CONV - j, :] * shifted
    return _silu(conv)


def _drain(gen):
    try:
        while True:
            next(gen)
    except StopIteration as stop:
        return stop.value


class _Stream:
    def __init__(self, gen):
        self.gen, self.done, self.value = gen, False, None

    def step(self, n):
        for _ in range(n):
            if self.done:
                return
            try:
                next(self.gen)
            except StopIteration as stop:
                self.done, self.value = True, stop.value


def _proj_mixer_prompt_kernel(x1_ref, wa_ref, wb_ref, wdt_ref, dtb_ref, lbp_ref, cw_ref, cb_ref, alog_ref,
                              dskip_ref, mnw_ref, wmo_ref, hnw_ref, who_ref,
                              mix_ref, conv_out_ref, ssm_out_ref, s_out_ref, prev_scr, h_scr, s_scr):
    t = pl.program_id(1)
    tl = x1_ref.shape[1]

    @pl.when(t == 0)
    def _():
        prev_scr[...] = jnp.zeros_like(prev_scr)
        h_scr[...] = jnp.zeros_like(h_scr)
        s_scr[...] = jnp.zeros_like(s_scr)

    xb = x1_ref[0].astype(BF16)
    piece = lambda c: _proj_piece(c, xb, wa_ref, wb_ref, lbp_ref)
    x_raw, bc_raw = piece(COL_X), piece(COL_BC)
    dt = _proj_dt(xb, wdt_ref, dtb_ref)
    tail = jnp.concatenate([x_raw[tl - SUBLANES:tl, :], bc_raw[tl - SUBLANES:tl, :]], axis=1)
    xm = _conv_silu(x_raw, prev_scr[:, :M_INNER], cw_ref[:, :M_INNER], cb_ref[:, :M_INNER])
    bc = _conv_silu(bc_raw, prev_scr[:, M_INNER:], cw_ref[:, M_INNER:], cb_ref[:, M_INNER:])
    prev_scr[...] = tail
    bm = bc[:, :M_GROUPS * M_STATE]
    cm = bc[:, M_GROUPS * M_STATE:]
    a_row = -jnp.exp(alog_ref[...])
    chunks = []
    for c in range(tl // SSD_CHUNK):
        rs = slice(c * SSD_CHUNK, (c + 1) * SSD_CHUNK)
        chunks.append((0, xm[rs], bm[rs], cm[rs], dt[rs]))
    got = {}

    def remaining_pieces():
        for c in (COL_LOGF, COL_Q, COL_I, COL_Z, COL_OG, COL_GM, COL_GH):
            got[c] = _proj_piece(c, xb, wa_ref, wb_ref, lbp_ref, activate=(c == COL_LOGF))
            yield

    proj = _Stream(remaining_pieces())

    def need(c):
        while c not in got:
            proj.step(1)
        return got[c] if c == COL_LOGF else _activate_piece(c, got[c], lbp_ref)

    def ssd_branch():
        ys, h_new = yield from _ssd_streams(chunks, {0: _state_blocks(h_scr[...])}, a_row,
                                            stage_major=False)
        gn = _ssd_gated_norm(jnp.concatenate(ys, axis=0), xm, need(COL_Z), dskip_ref[...], mnw_ref[...])
        ym = _dot(gn.astype(BF16), wmo_ref[...])
        yield
        return ym, h_new

    def hgrn_branch():
        (log_f, k), q, v = need(COL_LOGF), need(COL_Q), need(COL_I)
        blocks = []
        for b in range(tl // HGRN_BLOCK):
            rs = slice(b * HGRN_BLOCK, (b + 1) * HGRN_BLOCK)
            blocks.append((0, q[rs], log_f[rs], k[rs], v[rs]))
        outs, s_new = yield from _hgrn_streams(blocks, {0: [s_scr[h] for h in range(H_HEADS)]}, H_CHUNK)
        o = jnp.concatenate(outs, axis=0) * hnw_ref[...] * need(COL_OG)
        yh = _dot(o.astype(BF16), who_ref[...])
        yield
        return yh, s_new

    ssd, hgrn = _Stream(ssd_branch()), _Stream(hgrn_branch())
    n_ssd, n_hgrn = MIXER_INTERLEAVE
    while COL_I not in got:
        proj.step(1)
        ssd.step(n_ssd)
    while not (proj.done and ssd.done and hgrn.done):
        proj.step(1)
        ssd.step(n_ssd)
        hgrn.step(n_hgrn)
    (ym, h_new), (yh, s_new) = ssd.value, hgrn.value
    for h in range(M_HEADS):
        h_scr[h * M_HEAD_DIM:(h + 1) * M_HEAD_DIM, :] = h_new[0][h]
    for h in range(H_HEADS):
        s_scr[h] = s_new[0][h]
    mix_ref[0] = need(COL_GM) * ym + need(COL_GH) * yh

    @pl.when(t == pl.num_programs(1) - 1)
    def _():
        conv_out_ref[0] = tail
        ssm_out_ref[0] = h_scr[...]
        for h in range(H_HEADS):
            s_out_ref[0, h] = jnp.transpose(s_scr[h])


def _tile_rows(tokens, before):
    n, c = tokens.shape
    parts = [tokens, jnp.zeros((SAMPLE_ROWS - n - before, c), tokens.dtype)]
    if before:
        parts.insert(0, jnp.zeros((before, c), tokens.dtype))
    return jnp.concatenate(parts, axis=0)


def _ssd_sample_kernel(x_ref, bc_ref, z_ref, dt_ref, cs_ref, h0_ref, cw_ref, cb_ref, alog_ref,
                       dskip_ref, mnw_ref, wmo_ref, ym_ref, conv_out_ref, ssm_out_ref):
    bb = h0_ref.shape[0]
    n_tok = x_ref.shape[0] // bb
    a_row = -jnp.exp(alog_ref[...])
    chunks, zs, h_prev = [], [], {}
    for i in range(bb):
        rs = slice(i * n_tok, (i + 1) * n_tok)
        raw = jnp.concatenate([x_ref[rs, :], bc_ref[rs, :]], axis=1)
        u = jnp.concatenate([cs_ref[i], raw, jnp.zeros((1, M_CONV_DIM), F32)], axis=0)
        conv_out_ref[i] = u[n_tok:n_tok + M_CONV - 1, :]
        conv = cb_ref[...] + cw_ref[M_CONV - 1:M_CONV, :] * u
        for j in range(1, M_CONV):
            conv = conv + cw_ref[M_CONV - 1 - j:M_CONV - j, :] * pltpu.roll(u, j, 0)
        xbc = _silu(conv)
        xm = xbc[:, :M_INNER]
        bm = xbc[:, M_INNER:M_INNER + M_GROUPS * M_STATE]
        cm = xbc[:, M_INNER + M_GROUPS * M_STATE:]
        chunks.append((i, xm, bm, cm, _tile_rows(dt_ref[rs, :], SAMPLE_FIRST)))
        zs.append(_tile_rows(z_ref[rs, :], SAMPLE_FIRST))
        h_prev[i] = _state_blocks(h0_ref[i])
    ys, h_new = _drain(_ssd_streams(chunks, h_prev, a_row, stage_major=True))
    gns = []
    for i in range(bb):
        for h in range(M_HEADS):
            ssm_out_ref[i, h * M_HEAD_DIM:(h + 1) * M_HEAD_DIM, :] = h_new[i][h]
        gns.append(_ssd_gated_norm(ys[i], chunks[i][1], zs[i], dskip_ref[...], mnw_ref[...]))
    ym = _dot(jnp.concatenate(gns, axis=0).astype(BF16), wmo_ref[...])
    for i in range(bb):
        ym_ref[i * n_tok:(i + 1) * n_tok, :] = ym[i * SAMPLE_ROWS + SAMPLE_FIRST:
                                                  i * SAMPLE_ROWS + SAMPLE_FIRST + n_tok, :]


def _hgrn_gates(f_raw, lbp_ref, cols=slice(0, H_HEADS * H_KEY)):
    p0 = lbp_ref[0:1, cols]
    p1 = lbp_ref[1:2, cols]
    m = jnp.maximum(p0, p1)
    e0 = jnp.exp(p0 - m)
    e1 = jnp.exp(p1 - m)
    lb = e0 / (e0 + e1)
    log_f = jnp.log(lb + (1.0 - lb) * _sigmoid(f_raw))
    k = (1.0 - lb) * _sigmoid(-f_raw)
    return log_f, k


def _hgrn_streams(blocks, s_prev, ch):
    rows = blocks[0][1].shape[0]
    n = len(blocks)
    shift = ch.bit_length() - 1
    n_chunks = rows // ch
    row_i = lax.broadcasted_iota(jnp.int32, (rows, LANES), 0)
    col_i = lax.broadcasted_iota(jnp.int32, (rows, LANES), 1)
    causal = (jnp.right_shift(row_i, shift) == jnp.right_shift(col_i, shift)) & (col_i <= row_i)
    chunk_t = jnp.right_shift(lax.broadcasted_iota(jnp.int32, (LANES, LANES), 1), shift)
    heads = [slice(h * H_KEY, (h + 1) * H_KEY) for h in range(H_HEADS)]
    chunks = [slice(c * ch, (c + 1) * ch) for c in range(n_chunks)]
    items = [(i, h) for i in range(n) for h in range(H_HEADS)]

    pre = []
    for (_, q, log_f, k, v) in blocks:
        b_cum = _chunk_cumsum(log_f, ch)
        lasts = [b_cum[c * ch + ch - 1:c * ch + ch, :] for c in range(n_chunks)]
        b_last = jnp.concatenate([jnp.broadcast_to(l, (ch, l.shape[1])) for l in lasts], axis=0)
        pre.append(dict(qd=q * jnp.exp(b_cum), kd=k * jnp.exp(-b_cum),
                        k_end=k * jnp.exp(b_last - b_cum), lasts=lasts, v=v))
    qh = {(i, h): pre[i]["qd"][:, heads[h]].astype(BF16) for (i, h) in items}
    vh = {(i, h): _pad_rows(pre[i]["v"][:, heads[h]]) for (i, h) in items}
    sc, ds, y_intra, y_inter = {}, {}, {}, {}
    for (i, h) in items:
        sc[i, h] = _dot_nt(qh[i, h], _pad_rows(pre[i]["kd"][:, heads[h]]).astype(BF16))
        yield
    for (i, h) in items:
        v_t = jnp.transpose(vh[i, h])
        ke = _pad_rows(pre[i]["k_end"][:, heads[h]]).astype(BF16)
        ds[i, h] = []
        for c in range(n_chunks):
            v_tc = v_t if n_chunks == 1 else jnp.where(chunk_t == c, v_t, 0.0)
            ds[i, h].append(_dot(v_tc.astype(BF16), ke))
            yield
    for (i, h) in items:
        y_intra[i, h] = _dot(jnp.where(causal, sc[i, h], 0.0).astype(BF16), vh[i, h].astype(BF16))
        yield
    cur = {seq: list(states) for seq, states in s_prev.items()}
    enter = {}
    for i, blk in enumerate(blocks):
        seq = blk[0]
        for h in range(H_HEADS):
            s = cur[seq][h]
            per_chunk = []
            for c in range(n_chunks):
                per_chunk.append(s)
                s = jnp.exp(pre[i]["lasts"][c][:, heads[h]]) * s + ds[i, h][c]
            enter[i, h] = per_chunk
            cur[seq][h] = s
    for (i, h) in items:
        y_inter[i, h] = []
        for c in range(n_chunks):
            y_inter[i, h].append(_dot_nt(qh[i, h][chunks[c]], enter[i, h][c].astype(BF16)))
            yield
    outs = []
    for i in range(n):
        cols = []
        for h in range(H_HEADS):
            pieces = [y_intra[i, h][chunks[c]] + y_inter[i, h][c] for c in range(n_chunks)]
            o_h = pieces[0] if n_chunks == 1 else jnp.concatenate(pieces, axis=0)
            cols.append(o_h * lax.rsqrt(jnp.mean(o_h * o_h, axis=-1, keepdims=True) + EPS))
        outs.append(jnp.concatenate(cols, axis=1))
    return outs, cur


def _hgrn_sample_kernel(q_ref, logf_ref, k_ref, i_ref, og_ref, ym_ref, gm_ref, gh_ref, s0_ref,
                        hnw_ref, who_ref, mix_ref, s_out_ref):
    bb = s0_ref.shape[0]
    n_tok = q_ref.shape[0] // bb
    rows = [slice(i * n_tok, (i + 1) * n_tok) for i in range(bb)]
    blocks, s_prev = [], {}
    for i in range(bb):
        blocks.append((i, _tile_rows(q_ref[rows[i], :], 0), _tile_rows(logf_ref[rows[i], :], 0),
                       _tile_rows(k_ref[rows[i], :], 0), _tile_rows(i_ref[rows[i], :], 0)))
        s_prev[i] = [jnp.transpose(s0_ref[i, h]) for h in range(H_HEADS)]
    outs, s_new = _drain(_hgrn_streams(blocks, s_prev, SAMPLE_ROWS))
    os_ = []
    for i in range(bb):
        for h in range(H_HEADS):
            s_out_ref[i, h] = jnp.transpose(s_new[i][h])
        os_.append(outs[i] * hnw_ref[...] * _tile_rows(og_ref[rows[i], :], 0))
    yh = _dot(jnp.concatenate(os_, axis=0).astype(BF16), who_ref[...])
    for i in range(bb):
        mix_ref[rows[i], :] = (gm_ref[rows[i], :] * ym_ref[rows[i], :]
                               + gh_ref[rows[i], :] * yh[i * SAMPLE_ROWS:i * SAMPLE_ROWS + n_tok, :])


def _resident(shape):
    nd = len(shape)
    return pl.BlockSpec(shape, lambda *_: (0,) * nd, pipeline_mode=pl.Buffered(1))


def _params(semantics):
    return pltpu.CompilerParams(dimension_semantics=semantics, vmem_limit_bytes=VMEM_LIMIT)


def _row_tile(n_rows, want):
    tm = min(want, n_rows)
    assert n_rows % tm == 0
    return tm


def _ffn_ln(x, wg, wu, wd, g, b):
    n = x.shape[0]
    tm = _row_tile(n, FFN1_TM)
    row = pl.BlockSpec((tm, D_MODEL), lambda i: (i, 0))
    return pl.pallas_call(
        _ffn_ln_kernel,
        grid=(n // tm,),
        in_specs=[row, _resident(wg.shape), _resident(wu.shape), _resident(wd.shape),
                  _resident(g.shape), _resident(b.shape)],
        out_specs=row,
        out_shape=jax.ShapeDtypeStruct((n, D_MODEL), F32),
        compiler_params=_params(("parallel",)),
        name="ffn_ln",
    )(x, wg, wu, wd, g, b)


def _merge_ffn(x1, mix, *merge_w):
    n = x1.shape[0]
    tm = _row_tile(n, FFN2_TM)
    row = pl.BlockSpec((tm, D_MODEL), lambda i: (i, 0))
    return pl.pallas_call(
        _merge_ffn_kernel,
        grid=(n // tm,),
        in_specs=[row, row] + [_resident(w.shape) for w in merge_w],
        out_specs=row,
        out_shape=jax.ShapeDtypeStruct((n, D_MODEL), F32),
        compiler_params=_params(("parallel",)),
        name="merge_ffn",
    )(x1, mix, *merge_w)


def _in_proj(x1, proj_w):
    n = x1.shape[0]
    tm = _row_tile(n, 256)
    row = lambda width: pl.BlockSpec((tm, width), lambda i: (i, 0))
    return pl.pallas_call(
        _in_proj_kernel,
        grid=(n // tm,),
        in_specs=[row(D_MODEL)] + [_resident(w.shape) for w in proj_w],
        out_specs=[row(PROJ_MAIN), row(LANES), row(D_MODEL)],
        out_shape=[jax.ShapeDtypeStruct((n, PROJ_MAIN), F32),
                   jax.ShapeDtypeStruct((n, LANES), F32),
                   jax.ShapeDtypeStruct((n, D_MODEL), F32)],
        compiler_params=_params(("parallel",)),
        name="in_proj",
    )(x1, *proj_w)


def _proj_mixer_prompt(x1_3, proj_w, conv_wb, ssd_w, hgrn_w):
    bsz, length, _ = x1_3.shape
    tl = MIXER_TL
    rows = pl.BlockSpec((1, tl, D_MODEL), lambda b, t: (b, t, 0))
    once = lambda *shape: pl.BlockSpec((1,) + shape, lambda b, t: (b,) + (0,) * len(shape))
    weights = tuple(proj_w) + tuple(conv_wb) + tuple(ssd_w) + tuple(hgrn_w)
    return pl.pallas_call(
        _proj_mixer_prompt_kernel,
        grid=(bsz, length // tl),
        in_specs=[rows] + [_resident(w.shape) for w in weights],
        out_specs=[rows, once(SUBLANES, M_CONV_DIM), once(M_INNER, M_STATE),
                   once(H_HEADS, H_KEY, H_VAL)],
        out_shape=[jax.ShapeDtypeStruct((bsz, length, D_MODEL), F32),
                   jax.ShapeDtypeStruct((bsz, SUBLANES, M_CONV_DIM), F32),
                   jax.ShapeDtypeStruct((bsz, M_INNER, M_STATE), F32),
                   jax.ShapeDtypeStruct((bsz, H_HEADS, H_KEY, H_VAL), F32)],
        scratch_shapes=[pltpu.VMEM((SUBLANES, M_CONV_DIM), F32),
                        pltpu.VMEM((M_INNER, M_STATE), F32),
                        pltpu.VMEM((H_HEADS, H_VAL, H_KEY), F32)],
        compiler_params=_params(("parallel", "arbitrary")),
        name="proj_mixer_prompt",
    )(x1_3, *weights)


def _ssd_sample(proj, dt, conv0, ssm0, conv_wb, ssd_w):
    bsz = ssm0.shape[0]
    bb = SAMPLE_BATCH_BLOCK
    rows = bb * (proj.shape[0] // bsz)
    col = lambda width, c: pl.BlockSpec((rows, width), lambda b: (b, c))
    seq = lambda *shape: pl.BlockSpec((bb,) + shape, lambda b: (b,) + (0,) * len(shape))
    weights = tuple(conv_wb) + tuple(ssd_w)
    return pl.pallas_call(
        _ssd_sample_kernel,
        grid=(bsz // bb,),
        in_specs=[col(D_MODEL, COL_X), col(D_MODEL, COL_BC), col(D_MODEL, COL_Z), col(LANES, 0),
                  seq(M_CONV - 1, M_CONV_DIM), seq(M_INNER, M_STATE)]
                 + [_resident(w.shape) for w in weights],
        out_specs=[col(D_MODEL, 0), seq(M_CONV - 1, M_CONV_DIM), seq(M_INNER, M_STATE)],
        out_shape=[jax.ShapeDtypeStruct((proj.shape[0], D_MODEL), F32),
                   jax.ShapeDtypeStruct((bsz, M_CONV - 1, M_CONV_DIM), F32),
                   jax.ShapeDtypeStruct((bsz, M_INNER, M_STATE), F32)],
        compiler_params=_params(("parallel",)),
        name="ssd_sample",
    )(proj, proj, proj, dt, conv0, ssm0, *weights)


def _hgrn_sample(proj, kk, ym, s0, hgrn_w):
    bsz = s0.shape[0]
    bb = SAMPLE_BATCH_BLOCK
    rows = bb * (proj.shape[0] // bsz)
    col = lambda c: pl.BlockSpec((rows, D_MODEL), lambda b: (b, c))
    state = pl.BlockSpec((bb, H_HEADS, H_KEY, H_VAL), lambda b: (b, 0, 0, 0))
    return pl.pallas_call(
        _hgrn_sample_kernel,
        grid=(bsz // bb,),
        in_specs=[col(COL_Q), col(COL_LOGF), col(0), col(COL_I), col(COL_OG), col(0), col(COL_GM),
                  col(COL_GH), state] + [_resident(w.shape) for w in hgrn_w],
        out_specs=[col(0), state],
        out_shape=[jax.ShapeDtypeStruct((proj.shape[0], D_MODEL), F32),
                   jax.ShapeDtypeStruct((bsz, H_HEADS, H_KEY, H_VAL), F32)],
        compiler_params=_params(("parallel",)),
        name="hgrn_sample",
    )(proj, proj, kk, proj, proj, ym, proj, proj, s0, *hgrn_w)


def _pad_lanes(v):
    return jnp.pad(v, (0, LANES - v.shape[0])).reshape(1, LANES)


def kernel(x_prompt, x_sample, state_conv, state_ssm, state_hgrn, ffn1_w_gate, ffn1_w_up, ffn1_w_down, ln1_g, ln1_b, w_in, conv_w, conv_b, dt_bias, a_log, d_skip, m_norm_w, w_m_out, hgrn_lb_param, h_norm_w, w_h_out, w_o, ln2_g, ln2_b, ffn2_w_gate, ffn2_w_up, ffn2_w_down, ln3_g, ln3_b):
    assert w_in.shape[0] == 1, "single trunk layer"
    bp, lp, _ = x_prompt.shape
    bs, ls, _ = x_sample.shape
    assert ls == SAMPLE_ROWS - M_CONV

    row = lambda v: v[0].reshape(1, -1)
    f1 = (ffn1_w_gate[0], ffn1_w_up[0], ffn1_w_down[0])
    f2 = (ffn2_w_gate[0].astype(BF16), ffn2_w_up[0].astype(BF16), ffn2_w_down[0].astype(BF16))
    wt = jnp.transpose(w_in[0])
    dt_end = W_DT_AT + M_HEADS
    cut = lambda lo, hi: jnp.transpose(wt[lo:hi]).astype(BF16)
    w_dt = jnp.pad(cut(W_DT_AT, dt_end), ((0, 0), (0, LANES - M_HEADS)))
    proj_w = (cut(0, W_DT_AT), cut(dt_end, wt.shape[0]), w_dt, _pad_lanes(dt_bias[0]), hgrn_lb_param)
    conv_wb = (conv_w[0], row(conv_b))
    ssd_w = (_pad_lanes(a_log[0]), jnp.repeat(d_skip[0], M_HEAD_DIM).reshape(1, M_INNER),
             row(m_norm_w), w_m_out[0].astype(BF16))
    hgrn_w = (row(h_norm_w), w_h_out[0].astype(BF16))
    merge_w = (w_o[0], row(ln2_g), row(ln2_b)) + f2 + (row(ln3_g), row(ln3_b))

    x1 = _ffn_ln(x_prompt.reshape(bp * lp, D_MODEL), *f1, row(ln1_g), row(ln1_b))
    mix, conv_p, ssm_p, hg_p = _proj_mixer_prompt(x1.reshape(bp, lp, D_MODEL), proj_w, conv_wb,
                                                  ssd_w, hgrn_w)
    y_prompt = _merge_ffn(x1, mix.reshape(-1, D_MODEL), *merge_w).reshape(bp, lp, D_MODEL)
    new_conv_p = conv_p[:, SUBLANES - (M_CONV - 1):, :][None]
    new_ssm_p = ssm_p.reshape(1, bp, M_HEADS, M_HEAD_DIM, M_STATE)
    new_hg_p = hg_p[None]

    x1s = _ffn_ln(x_sample.reshape(bs * ls, D_MODEL), *f1, row(ln1_g), row(ln1_b))
    projs, dts, kks = _in_proj(x1s, proj_w)
    yms, conv_s, ssm_s = _ssd_sample(projs, dts, state_conv[0],
                                     state_ssm[0].reshape(bs, M_INNER, M_STATE), conv_wb, ssd_w)
    mixs, hg_s = _hgrn_sample(projs, kks, yms, state_hgrn[0], hgrn_w)
    y_sample = _merge_ffn(x1s, mixs, *merge_w).reshape(bs, ls, D_MODEL)
    new_conv_s = conv_s[None]
    new_ssm_s = ssm_s.reshape(1, bs, M_HEADS, M_HEAD_DIM, M_STATE)
    new_hg_s = hg_s[None]

    return (y_prompt, y_sample, new_conv_p, new_ssm_p, new_hg_p, new_conv_s, new_ssm_s, new_hg_s)
```

```python
import jax
import jax.numpy as jnp
from jax import lax
from jax.experimental import pallas as pl
from jax.experimental.pallas import tpu as pltpu

F32 = jnp.float32
BF16 = jnp.bfloat16

D_MODEL = 1024
D_FF = 2816
M_HEADS = 16
M_HEAD_DIM = 64
M_GROUPS = 4
M_STATE = 128
M_INNER = M_HEADS * M_HEAD_DIM
M_CONV = 4
M_CONV_DIM = M_INNER + 2 * M_GROUPS * M_STATE
HEADS_PER_GROUP = M_HEADS // M_GROUPS
GROUP_COLS = HEADS_PER_GROUP * M_HEAD_DIM
H_HEADS = 8
H_KEY = 128
H_VAL = 128
H_CHUNK = 32
ALPHA = 2.0 ** 0.25
EPS = 1e-5

LANES = 128
SUBLANES = 8
VMEM_LIMIT = 56 * 1024 * 1024

FF_CHUNK = D_FF // 2
FFN1_TM = 512
FFN2_TM = 512
PROJ_MAIN = 9 * D_MODEL
COL_X, COL_Q, COL_GM = 0, 1, 2
COL_BC, COL_I, COL_GH = 3, 4, 5
COL_Z, COL_LOGF, COL_OG = 6, 7, 8
W_DT_AT = M_INNER + M_CONV_DIM
PROJ_PIECES = ((COL_X, ("a", M_INNER)), (COL_Q, ("b", 0)), (COL_GM, ("b", 4 * D_MODEL)),
               (COL_BC, ("a", 2 * M_INNER)), (COL_I, ("b", 2 * D_MODEL)), (COL_GH, ("b", 5 * D_MODEL)),
               (COL_Z, ("a", 0)), (COL_LOGF, ("b", D_MODEL)), (COL_OG, ("b", 3 * D_MODEL)))

SSD_CHUNK = 128
HGRN_BLOCK = 128
MIXER_TL = 256
MIXER_INTERLEAVE = (5, 20)
SAMPLE_ROWS = 8
SAMPLE_FIRST = M_CONV - 1
SAMPLE_BATCH_BLOCK = 8


def _sigmoid(x):
    return 1.0 / (1.0 + jnp.exp(-x))


def _silu(x):
    return x * _sigmoid(x)


def _softplus(x):
    return jnp.maximum(x, 0.0) + jnp.log(1.0 + jnp.exp(-jnp.abs(x)))


def _dot(a, b):
    return jnp.dot(a, b, preferred_element_type=F32)


def _dot_nt(a, b):
    return lax.dot_general(a, b, (((1,), (1,)), ((), ())), preferred_element_type=F32)


def _pad_rows(a):
    q = a.shape[0]
    if q == LANES:
        return a
    return jnp.concatenate([a, jnp.zeros((LANES - q, a.shape[1]), a.dtype)], axis=0)


def _chunk_cumsum(x, ch):
    rows, cols = x.shape
    g = rows // SUBLANES
    y = x.reshape(g, SUBLANES, cols)
    sub = lax.broadcasted_iota(jnp.int32, (g, SUBLANES, cols), 1)
    s = 1
    while s < SUBLANES:
        y = y + jnp.where(sub >= s, pltpu.roll(y, s, 1), 0.0)
        s *= 2
    per = ch // SUBLANES
    if per > 1:
        y4 = y.reshape(g // per, per, SUBLANES, cols)
        carry, outs = None, []
        for j in range(per):
            yj = y4[:, j]
            outs.append(yj if carry is None else yj + carry)
            tot = jnp.broadcast_to(yj[:, SUBLANES - 1:SUBLANES, :], yj.shape)
            carry = tot if carry is None else carry + tot
        y = jnp.stack(outs, axis=1).reshape(g, SUBLANES, cols)
    return y.reshape(rows, cols)


def _layer_norm(y, g, b):
    mu = jnp.mean(y, axis=-1, keepdims=True)
    yc = y - mu
    var = jnp.mean(yc * yc, axis=-1, keepdims=True)
    return yc * lax.rsqrt(var + EPS) * g + b


def _swiglu(x, wg_ref, wu_ref, wd_ref):
    xb = x.astype(wg_ref.dtype)
    acc = None
    for c in range(D_FF // FF_CHUNK):
        sl = slice(c * FF_CHUNK, (c + 1) * FF_CHUNK)
        hg = _dot(xb, wg_ref[:, sl])
        hu = _dot(xb, wu_ref[:, sl])
        act = (_silu(hg) * hu).astype(wd_ref.dtype)
        part = _dot(act, wd_ref[sl, :])
        acc = part if acc is None else acc + part
    return acc


def _ffn_ln_kernel(x_ref, wg_ref, wu_ref, wd_ref, g_ref, b_ref, o_ref):
    x = x_ref[...]
    y = ALPHA * x + 0.5 * _swiglu(x, wg_ref, wu_ref, wd_ref)
    o_ref[...] = _layer_norm(y, g_ref[...], b_ref[...])


def _activate_piece(c, p, lbp_ref, sub=slice(0, D_MODEL)):
    if c in (COL_GM, COL_GH):
        return _sigmoid(p)
    if c in (COL_Z, COL_OG):
        return _silu(p)
    if c == COL_LOGF:
        return _hgrn_gates(p, lbp_ref, sub)
    return p


def _proj_piece(c, xb, wa_ref, wb_ref, lbp_ref, sub=slice(0, D_MODEL), activate=True):
    part, at = dict(PROJ_PIECES)[c]
    w_ref = wa_ref if part == "a" else wb_ref
    p = _dot(xb, w_ref[:, at + sub.start:at + sub.stop])
    return _activate_piece(c, p, lbp_ref, sub) if activate else p


def _proj_dt(xb, wdt_ref, dtb_ref):
    return _softplus(_dot(xb, wdt_ref[...]) + dtb_ref[...])


def _in_proj_kernel(x_ref, wa_ref, wb_ref, wdt_ref, dtb_ref, lbp_ref, o_ref, odt_ref, k_ref):
    xb = x_ref[...].astype(BF16)
    for c, _ in PROJ_PIECES:
        cols = slice(c * D_MODEL, (c + 1) * D_MODEL)
        p = _proj_piece(c, xb, wa_ref, wb_ref, lbp_ref)
        if c == COL_LOGF:
            o_ref[:, cols], k_ref[...] = p
        else:
            o_ref[:, cols] = p
    odt_ref[...] = _proj_dt(xb, wdt_ref, dtb_ref)


def _merge_ffn_kernel(x1_ref, mix_ref, wo_ref, g2_ref, b2_ref,
                      wg_ref, wu_ref, wd_ref, g3_ref, b3_ref, o_ref):
    x2 = _layer_norm(ALPHA * x1_ref[...] + _dot(mix_ref[...], wo_ref[...]),
                     g2_ref[...], b2_ref[...])
    y = ALPHA * x2 + 0.5 * _swiglu(x2, wg_ref, wu_ref, wd_ref)
    o_ref[...] = _layer_norm(y, g3_ref[...], b3_ref[...])


def _ssd_streams(chunks, h_prev, a_row, stage_major):
    q = chunks[0][1].shape[0]
    n = len(chunks)
    row_i = lax.broadcasted_iota(jnp.int32, (q, LANES), 0)
    col_i = lax.broadcasted_iota(jnp.int32, (q, LANES), 1)
    causal = col_i <= row_i
    head_blk = jnp.right_shift(lax.broadcasted_iota(jnp.int32, (q, GROUP_COLS), 1),
                               M_HEAD_DIM.bit_length() - 1)
    groups = [slice(g * M_STATE, (g + 1) * M_STATE) for g in range(M_GROUPS)]
    rows_r = [slice(r * M_HEAD_DIM, (r + 1) * M_HEAD_DIM) for r in range(HEADS_PER_GROUP)]

    pre, ops = {}, {}

    def prepare(i):
        _, xm, bm, cm, dt = chunks[i]
        a_cum = _chunk_cumsum(dt * a_row, q)
        a_last = a_cum[q - 1:q, :]
        to_end = jnp.exp(a_last - a_cum) * dt
        pre[i] = dict(
            a_cum=a_cum, e_last=jnp.exp(a_last), e_cum=jnp.exp(a_cum),
            a_cum_t=jnp.transpose(_pad_rows(a_cum)),
            dt_t=jnp.transpose(_pad_rows(dt)),
            to_end_t=jnp.transpose(_pad_rows(to_end)))
        for g in range(M_GROUPS):
            xg = _pad_rows(xm[:, g * GROUP_COLS:(g + 1) * GROUP_COLS])
            ops[i, g] = dict(bg=_pad_rows(bm[:, groups[g]]).astype(BF16),
                             cg=cm[:, groups[g]].astype(BF16),
                             xg_b=xg.astype(BF16), xg_t=jnp.transpose(xg))

    def mm_cb(i, g):
        return _dot_nt(ops[i, g]["cg"], ops[i, g]["bg"])

    def state_lhs(i, g, r):
        h = g * HEADS_PER_GROUP + r
        return (ops[i, g]["xg_t"][rows_r[r], :] * pre[i]["to_end_t"][h:h + 1, :]).astype(BF16)

    def mm_state(i, g, r):
        return _dot(state_lhs(i, g, r), ops[i, g]["bg"])

    def intra_lhs(i, g, r, cb):
        h = g * HEADS_PER_GROUP + r
        seg = pre[i]["a_cum"][:, h:h + 1] - pre[i]["a_cum_t"][h:h + 1, :]
        decay = jnp.where(causal, jnp.exp(jnp.where(causal, seg, 0.0)), 0.0)
        return (cb * decay * pre[i]["dt_t"][h:h + 1, :]).astype(BF16)

    def mm_intra(i, g, r, cb):
        return _dot(intra_lhs(i, g, r, cb), ops[i, g]["xg_b"])

    def mm_inter(i, g, blocks):
        hg = jnp.concatenate(blocks[g * HEADS_PER_GROUP:(g + 1) * HEADS_PER_GROUP], axis=0)
        return _dot_nt(ops[i, g]["cg"], hg.astype(BF16))

    def combine(i, g, parts, y_inter):
        acc = jnp.zeros((q, GROUP_COLS), F32)
        for r in range(HEADS_PER_GROUP):
            h = g * HEADS_PER_GROUP + r
            acc = jnp.where(head_blk == r, parts[r] + y_inter * pre[i]["e_cum"][:, h:h + 1], acc)
        return acc

    cur = {seq: list(blocks) for seq, blocks in h_prev.items()}
    ys = []
    if not stage_major:
        for i, chunk in enumerate(chunks):
            prepare(i)
            enter = cur[chunk[0]]
            cbs = [mm_cb(i, g) for g in range(M_GROUPS)]
            y_inters = [mm_inter(i, g, enter) for g in range(M_GROUPS)]
            yield
            cols, new = [], []
            for g in range(M_GROUPS):
                acc = jnp.zeros((q, GROUP_COLS), F32)
                for r in range(HEADS_PER_GROUP):
                    h = g * HEADS_PER_GROUP + r
                    y_h = mm_intra(i, g, r, cbs[g]) + y_inters[g] * pre[i]["e_cum"][:, h:h + 1]
                    acc = jnp.where(head_blk == r, y_h, acc)
                    new.append(pre[i]["e_last"][:, h:h + 1] * enter[h] + mm_state(i, g, r))
                    yield
                cols.append(acc)
            cur[chunk[0]] = new
            ys.append(jnp.concatenate(cols, axis=1))
        return ys, cur
    for i in range(n):
        prepare(i)
    cb = {(i, g): mm_cb(i, g) for i in range(n) for g in range(M_GROUPS)}
    st = {(i, g, r): mm_state(i, g, r)
          for i in range(n) for g in range(M_GROUPS) for r in range(HEADS_PER_GROUP)}
    part = {(i, g, r): mm_intra(i, g, r, cb[i, g])
            for i in range(n) for g in range(M_GROUPS) for r in range(HEADS_PER_GROUP)}
    enter = []
    for i, chunk in enumerate(chunks):
        seq = chunk[0]
        enter.append(list(cur[seq]))
        cur[seq] = [pre[i]["e_last"][:, h:h + 1] * cur[seq][h]
                    + st[i, h // HEADS_PER_GROUP, h % HEADS_PER_GROUP] for h in range(M_HEADS)]
    for i in range(n):
        cols = []
        for g in range(M_GROUPS):
            y_inter = mm_inter(i, g, enter[i])
            cols.append(combine(i, g, [part[i, g, r] for r in range(HEADS_PER_GROUP)], y_inter))
        ys.append(jnp.concatenate(cols, axis=1))
    return ys, cur


def _ssd_gated_norm(y_ssd, xm, z_act, dskip, mnw):
    g = (y_ssd + dskip * xm) * z_act
    outs = []
    for k in range(M_GROUPS):
        gk = g[:, k * GROUP_COLS:(k + 1) * GROUP_COLS]
        outs.append(gk * lax.rsqrt(jnp.mean(gk * gk, axis=-1, keepdims=True) + EPS))
    return jnp.concatenate(outs, axis=1) * mnw


def _state_blocks(h):
    return [h[k * M_HEAD_DIM:(k + 1) * M_HEAD_DIM, :] for k in range(M_HEADS)]


def _conv_silu(u, prev8, cw, cbias):
    tl = u.shape[0]
    ext = jnp.concatenate([prev8, u], axis=0)
    conv = cbias + cw[M_CONV - 1:M_CONV, :] * u
    for j in range(1, M_CONV):
        shifted = pltpu.roll(ext, j, 0)[SUBLANES:SUBLANES + tl, :]
        conv = conv + cw[M_CONV - 1 - j:M_CONV - j, :] * shifted
    return _silu(conv)


def _drain(gen):
    try:
        while True:
            next(gen)
    except StopIteration as stop:
        return stop.value


class _Stream:
    def __init__(self, gen):
        self.gen, self.done, self.value = gen, False, None

    def step(self, n):
        for _ in range(n):
            if self.done:
                return
            try:
                next(self.gen)
            except StopIteration as stop:
                self.done, self.value = True, stop.value


def _proj_mixer_prompt_kernel(x1_ref, wa_ref, wb_ref, wdt_ref, dtb_ref, lbp_ref, cw_ref, cb_ref, alog_ref,
                              dskip_ref, mnw_ref, wmo_ref, hnw_ref, who_ref,
                              mix_ref, conv_out_ref, ssm_out_ref, s_out_ref, prev_scr, h_scr, s_scr):
    t = pl.program_id(1)
    tl = x1_ref.shape[1]

    @pl.when(t == 0)
    def _():
        prev_scr[...] = jnp.zeros_like(prev_scr)
        h_scr[...] = jnp.zeros_like(h_scr)
        s_scr[...] = jnp.zeros_like(s_scr)

    xb = x1_ref[0].astype(BF16)
    piece = lambda c: _proj_piece(c, xb, wa_ref, wb_ref, lbp_ref)
    x_raw, bc_raw = piece(COL_X), piece(COL_BC)
    dt = _proj_dt(xb, wdt_ref, dtb_ref)
    tail = jnp.concatenate([x_raw[tl - SUBLANES:tl, :], bc_raw[tl - SUBLANES:tl, :]], axis=1)
    xm = _conv_silu(x_raw, prev_scr[:, :M_INNER], cw_ref[:, :M_INNER], cb_ref[:, :M_INNER])
    bc = _conv_silu(bc_raw, prev_scr[:, M_INNER:], cw_ref[:, M_INNER:], cb_ref[:, M_INNER:])
    prev_scr[...] = tail
    bm = bc[:, :M_GROUPS * M_STATE]
    cm = bc[:, M_GROUPS * M_STATE:]
    a_row = -jnp.exp(alog_ref[...])
    chunks = []
    for c in range(tl // SSD_CHUNK):
        rs = slice(c * SSD_CHUNK, (c + 1) * SSD_CHUNK)
        chunks.append((0, xm[rs], bm[rs], cm[rs], dt[rs]))
    got = {}

    def remaining_pieces():
        for c in (COL_LOGF, COL_Q, COL_I, COL_Z, COL_OG, COL_GM, COL_GH):
            got[c] = _proj_piece(c, xb, wa_ref, wb_ref, lbp_ref, activate=(c == COL_LOGF))
            yield

    proj = _Stream(remaining_pieces())

    def need(c):
        while c not in got:
            proj.step(1)
        return got[c] if c == COL_LOGF else _activate_piece(c, got[c], lbp_ref)

    def ssd_branch():
        ys, h_new = yield from _ssd_streams(chunks, {0: _state_blocks(h_scr[...])}, a_row,
                                            stage_major=False)
        gn = _ssd_gated_norm(jnp.concatenate(ys, axis=0), xm, need(COL_Z), dskip_ref[...], mnw_ref[...])
        ym = _dot(gn.astype(BF16), wmo_ref[...])
        yield
        return ym, h_new

    def hgrn_branch():
        (log_f, k), q, v = need(COL_LOGF), need(COL_Q), need(COL_I)
        blocks = []
        for b in range(tl // HGRN_BLOCK):
            rs = slice(b * HGRN_BLOCK, (b + 1) * HGRN_BLOCK)
            blocks.append((0, q[rs], log_f[rs], k[rs], v[rs]))
        outs, s_new = yield from _hgrn_streams(blocks, {0: [s_scr[h] for h in range(H_HEADS)]}, H_CHUNK)
        o = jnp.concatenate(outs, axis=0) * hnw_ref[...] * need(COL_OG)
        yh = _dot(o.astype(BF16), who_ref[...])
        yield
        return yh, s_new

    ssd, hgrn = _Stream(ssd_branch()), _Stream(hgrn_branch())
    n_ssd, n_hgrn = MIXER_INTERLEAVE
    while COL_I not in got:
        proj.step(1)
        ssd.step(n_ssd)
    while not (proj.done and ssd.done and hgrn.done):
        proj.step(1)
        ssd.step(n_ssd)
        hgrn.step(n_hgrn)
    (ym, h_new), (yh, s_new) = ssd.value, hgrn.value
    for h in range(M_HEADS):
        h_scr[h * M_HEAD_DIM:(h + 1) * M_HEAD_DIM, :] = h_new[0][h]
    for h in range(H_HEADS):
        s_scr[h] = s_new[0][h]
    mix_ref[0] = need(COL_GM) * ym + need(COL_GH) * yh

    @pl.when(t == pl.num_programs(1) - 1)
    def _():
        conv_out_ref[0] = tail
        ssm_out_ref[0] = h_scr[...]
        for h in range(H_HEADS):
            s_out_ref[0, h] = jnp.transpose(s_scr[h])


def _tile_rows(tokens, before):
    n, c = tokens.shape
    parts = [tokens, jnp.zeros((SAMPLE_ROWS - n - before, c), tokens.dtype)]
    if before:
        parts.insert(0, jnp.zeros((before, c), tokens.dtype))
    return jnp.concatenate(parts, axis=0)


def _ssd_sample_kernel(x_ref, bc_ref, z_ref, dt_ref, cs_ref, h0_ref, cw_ref, cb_ref, alog_ref,
                       dskip_ref, mnw_ref, wmo_ref, ym_ref, conv_out_ref, ssm_out_ref):
    bb = h0_ref.shape[0]
    n_tok = x_ref.shape[0] // bb
    a_row = -jnp.exp(alog_ref[...])
    chunks, zs, h_prev = [], [], {}
    for i in range(bb):
        rs = slice(i * n_tok, (i + 1) * n_tok)
        raw = jnp.concatenate([x_ref[rs, :], bc_ref[rs, :]], axis=1)
        u = jnp.concatenate([cs_ref[i], raw, jnp.zeros((1, M_CONV_DIM), F32)], axis=0)
        conv_out_ref[i] = u[n_tok:n_tok + M_CONV - 1, :]
        conv = cb_ref[...] + cw_ref[M_CONV - 1:M_CONV, :] * u
        for j in range(1, M_CONV):
            conv = conv + cw_ref[M_CONV - 1 - j:M_CONV - j, :] * pltpu.roll(u, j, 0)
        xbc = _silu(conv)
        xm = xbc[:, :M_INNER]
        bm = xbc[:, M_INNER:M_INNER + M_GROUPS * M_STATE]
        cm = xbc[:, M_INNER + M_GROUPS * M_STATE:]
        chunks.append((i, xm, bm, cm, _tile_rows(dt_ref[rs, :], SAMPLE_FIRST)))
        zs.append(_tile_rows(z_ref[rs, :], SAMPLE_FIRST))
        h_prev[i] = _state_blocks(h0_ref[i])
    ys, h_new = _drain(_ssd_streams(chunks, h_prev, a_row, stage_major=True))
    gns = []
    for i in range(bb):
        for h in range(M_HEADS):
            ssm_out_ref[i, h * M_HEAD_DIM:(h + 1) * M_HEAD_DIM, :] = h_new[i][h]
        gns.append(_ssd_gated_norm(ys[i], chunks[i][1], zs[i], dskip_ref[...], mnw_ref[...]))
    ym = _dot(jnp.concatenate(gns, axis=0).astype(BF16), wmo_ref[...])
    for i in range(bb):
        ym_ref[i * n_tok:(i + 1) * n_tok, :] = ym[i * SAMPLE_ROWS + SAMPLE_FIRST:
                                                  i * SAMPLE_ROWS + SAMPLE_FIRST + n_tok, :]


def _hgrn_gates(f_raw, lbp_ref, cols=slice(0, H_HEADS * H_KEY)):
    p0 = lbp_ref[0:1, cols]
    p1 = lbp_ref[1:2, cols]
    m = jnp.maximum(p0, p1)
    e0 = jnp.exp(p0 - m)
    e1 = jnp.exp(p1 - m)
    lb = e0 / (e0 + e1)
    log_f = jnp.log(lb + (1.0 - lb) * _sigmoid(f_raw))
    k = (1.0 - lb) * _sigmoid(-f_raw)
    return log_f, k


def _hgrn_streams(blocks, s_prev, ch):
    rows = blocks[0][1].shape[0]
    n = len(blocks)
    shift = ch.bit_length() - 1
    n_chunks = rows // ch
    row_i = lax.broadcasted_iota(jnp.int32, (rows, LANES), 0)
    col_i = lax.broadcasted_iota(jnp.int32, (rows, LANES), 1)
    causal = (jnp.right_shift(row_i, shift) == jnp.right_shift(col_i, shift)) & (col_i <= row_i)
    chunk_t = jnp.right_shift(lax.broadcasted_iota(jnp.int32, (LANES, LANES), 1), shift)
    heads = [slice(h * H_KEY, (h + 1) * H_KEY) for h in range(H_HEADS)]
    chunks = [slice(c * ch, (c + 1) * ch) for c in range(n_chunks)]
    items = [(i, h) for i in range(n) for h in range(H_HEADS)]

    pre = []
    for (_, q, log_f, k, v) in blocks:
        b_cum = _chunk_cumsum(log_f, ch)
        lasts = [b_cum[c * ch + ch - 1:c * ch + ch, :] for c in range(n_chunks)]
        b_last = jnp.concatenate([jnp.broadcast_to(l, (ch, l.shape[1])) for l in lasts], axis=0)
        pre.append(dict(qd=q * jnp.exp(b_cum), kd=k * jnp.exp(-b_cum),
                        k_end=k * jnp.exp(b_last - b_cum), lasts=lasts, v=v))
    qh = {(i, h): pre[i]["qd"][:, heads[h]].astype(BF16) for (i, h) in items}
    vh = {(i, h): _pad_rows(pre[i]["v"][:, heads[h]]) for (i, h) in items}
    sc, ds, y_intra, y_inter = {}, {}, {}, {}
    for (i, h) in items:
        sc[i, h] = _dot_nt(qh[i, h], _pad_rows(pre[i]["kd"][:, heads[h]]).astype(BF16))
        yield
    for (i, h) in items:
        v_t = jnp.transpose(vh[i, h])
        ke = _pad_rows(pre[i]["k_end"][:, heads[h]]).astype(BF16)
        ds[i, h] = []
        for c in range(n_chunks):
            v_tc = v_t if n_chunks == 1 else jnp.where(chunk_t == c, v_t, 0.0)
            ds[i, h].append(_dot(v_tc.astype(BF16), ke))
            yield
    for (i, h) in items:
        y_intra[i, h] = _dot(jnp.where(causal, sc[i, h], 0.0).astype(BF16), vh[i, h].astype(BF16))
        yield
    cur = {seq: list(states) for seq, states in s_prev.items()}
    enter = {}
    for i, blk in enumerate(blocks):
        seq = blk[0]
        for h in range(H_HEADS):
            s = cur[seq][h]
            per_chunk = []
            for c in range(n_chunks):
                per_chunk.append(s)
                s = jnp.exp(pre[i]["lasts"][c][:, heads[h]]) * s + ds[i, h][c]
            enter[i, h] = per_chunk
            cur[seq][h] = s
    for (i, h) in items:
        y_inter[i, h] = []
        for c in range(n_chunks):
            y_inter[i, h].append(_dot_nt(qh[i, h][chunks[c]], enter[i, h][c].astype(BF16)))
            yield
    outs = []
    for i in range(n):
        cols = []
        for h in range(H_HEADS):
            pieces = [y_intra[i, h][chunks[c]] + y_inter[i, h][c] for c in range(n_chunks)]
            o_h = pieces[0] if n_chunks == 1 else jnp.concatenate(pieces, axis=0)
            cols.append(o_h * lax.rsqrt(jnp.mean(o_h * o_h, axis=-1, keepdims=True) + EPS))
        outs.append(jnp.concatenate(cols, axis=1))
    return outs, cur


def _hgrn_sample_kernel(q_ref, logf_ref, k_ref, i_ref, og_ref, ym_ref, gm_ref, gh_ref, s0_ref,
                        hnw_ref, who_ref, mix_ref, s_out_ref):
    bb = s0_ref.shape[0]
    n_tok = q_ref.shape[0] // bb
    rows = [slice(i * n_tok, (i + 1) * n_tok) for i in range(bb)]
    blocks, s_prev = [], {}
    for i in range(bb):
        blocks.append((i, _tile_rows(q_ref[rows[i], :], 0), _tile_rows(logf_ref[rows[i], :], 0),
                       _tile_rows(k_ref[rows[i], :], 0), _tile_rows(i_ref[rows[i], :], 0)))
        s_prev[i] = [jnp.transpose(s0_ref[i, h]) for h in range(H_HEADS)]
    outs, s_new = _drain(_hgrn_streams(blocks, s_prev, SAMPLE_ROWS))
    os_ = []
    for i in range(bb):
        for h in range(H_HEADS):
            s_out_ref[i, h] = jnp.transpose(s_new[i][h])
        os_.append(outs[i] * hnw_ref[...] * _tile_rows(og_ref[rows[i], :], 0))
    yh = _dot(jnp.concatenate(os_, axis=0).astype(BF16), who_ref[...])
    for i in range(bb):
        mix_ref[rows[i], :] = (gm_ref[rows[i], :] * ym_ref[rows[i], :]
                               + gh_ref[rows[i], :] * yh[i * SAMPLE_ROWS:i * SAMPLE_ROWS + n_tok, :])


def _resident(shape):
    nd = len(shape)
    return pl.BlockSpec(shape, lambda *_: (0,) * nd, pipeline_mode=pl.Buffered(1))


def _params(semantics):
    return pltpu.CompilerParams(dimension_semantics=semantics, vmem_limit_bytes=VMEM_LIMIT)


def _row_tile(n_rows, want):
    tm = min(want, n_rows)
    assert n_rows % tm == 0
    return tm


def _ffn_ln(x, wg, wu, wd, g, b):
    n = x.shape[0]
    tm = _row_tile(n, FFN1_TM)
    row = pl.BlockSpec((tm, D_MODEL), lambda i: (i, 0))
    return pl.pallas_call(
        _ffn_ln_kernel,
        grid=(n // tm,),
        in_specs=[row, _resident(wg.shape), _resident(wu.shape), _resident(wd.shape),
                  _resident(g.shape), _resident(b.shape)],
        out_specs=row,
        out_shape=jax.ShapeDtypeStruct((n, D_MODEL), F32),
        compiler_params=_params(("parallel",)),
        name="ffn_ln",
    )(x, wg, wu, wd, g, b)


def _merge_ffn(x1, mix, *merge_w):
    n = x1.shape[0]
    tm = _row_tile(n, FFN2_TM)
    row = pl.BlockSpec((tm, D_MODEL), lambda i: (i, 0))
    return pl.pallas_call(
        _merge_ffn_kernel,
        grid=(n // tm,),
        in_specs=[row, row] + [_resident(w.shape) for w in merge_w],
        out_specs=row,
        out_shape=jax.ShapeDtypeStruct((n, D_MODEL), F32),
        compiler_params=_params(("parallel",)),
        name="merge_ffn",
    )(x1, mix, *merge_w)


def _in_proj(x1, proj_w):
    n = x1.shape[0]
    tm = _row_tile(n, 256)
    row = lambda width: pl.BlockSpec((tm, width), lambda i: (i, 0))
    return pl.pallas_call(
        _in_proj_kernel,
        grid=(n // tm,),
        in_specs=[row(D_MODEL)] + [_resident(w.shape) for w in proj_w],
        out_specs=[row(PROJ_MAIN), row(LANES), row(D_MODEL)],
        out_shape=[jax.ShapeDtypeStruct((n, PROJ_MAIN), F32),
                   jax.ShapeDtypeStruct((n, LANES), F32),
                   jax.ShapeDtypeStruct((n, D_MODEL), F32)],
        compiler_params=_params(("parallel",)),
        name="in_proj",
    )(x1, *proj_w)


def _proj_mixer_prompt(x1_3, proj_w, conv_wb, ssd_w, hgrn_w):
    bsz, length, _ = x1_3.shape
    tl = MIXER_TL
    rows = pl.BlockSpec((1, tl, D_MODEL), lambda b, t: (b, t, 0))
    once = lambda *shape: pl.BlockSpec((1,) + shape, lambda b, t: (b,) + (0,) * len(shape))
    weights = tuple(proj_w) + tuple(conv_wb) + tuple(ssd_w) + tuple(hgrn_w)
    return pl.pallas_call(
        _proj_mixer_prompt_kernel,
        grid=(bsz, length // tl),
        in_specs=[rows] + [_resident(w.shape) for w in weights],
        out_specs=[rows, once(SUBLANES, M_CONV_DIM), once(M_INNER, M_STATE),
                   once(H_HEADS, H_KEY, H_VAL)],
        out_shape=[jax.ShapeDtypeStruct((bsz, length, D_MODEL), F32),
                   jax.ShapeDtypeStruct((bsz, SUBLANES, M_CONV_DIM), F32),
                   jax.ShapeDtypeStruct((bsz, M_INNER, M_STATE), F32),
                   jax.ShapeDtypeStruct((bsz, H_HEADS, H_KEY, H_VAL), F32)],
        scratch_shapes=[pltpu.VMEM((SUBLANES, M_CONV_DIM), F32),
                        pltpu.VMEM((M_INNER, M_STATE), F32),
                        pltpu.VMEM((H_HEADS, H_VAL, H_KEY), F32)],
        compiler_params=_params(("parallel", "arbitrary")),
        name="proj_mixer_prompt",
    )(x1_3, *weights)


def _ssd_sample(proj, dt, conv0, ssm0, conv_wb, ssd_w):
    bsz = ssm0.shape[0]
    bb = SAMPLE_BATCH_BLOCK
    rows = bb * (proj.shape[0] // bsz)
    col = lambda width, c: pl.BlockSpec((rows, width), lambda b: (b, c))
    seq = lambda *shape: pl.BlockSpec((bb,) + shape, lambda b: (b,) + (0,) * len(shape))
    weights = tuple(conv_wb) + tuple(ssd_w)
    return pl.pallas_call(
        _ssd_sample_kernel,
        grid=(bsz // bb,),
        in_specs=[col(D_MODEL, COL_X), col(D_MODEL, COL_BC), col(D_MODEL, COL_Z), col(LANES, 0),
                  seq(M_CONV - 1, M_CONV_DIM), seq(M_INNER, M_STATE)]
                 + [_resident(w.shape) for w in weights],
        out_specs=[col(D_MODEL, 0), seq(M_CONV - 1, M_CONV_DIM), seq(M_INNER, M_STATE)],
        out_shape=[jax.ShapeDtypeStruct((proj.shape[0], D_MODEL), F32),
                   jax.ShapeDtypeStruct((bsz, M_CONV - 1, M_CONV_DIM), F32),
                   jax.ShapeDtypeStruct((bsz, M_INNER, M_STATE), F32)],
        compiler_params=_params(("parallel",)),
        name="ssd_sample",
    )(proj, proj, proj, dt, conv0, ssm0, *weights)


def _hgrn_sample(proj, kk, ym, s0, hgrn_w):
    bsz = s0.shape[0]
    bb = SAMPLE_BATCH_BLOCK
    rows = bb * (proj.shape[0] // bsz)
    col = lambda c: pl.BlockSpec((rows, D_MODEL), lambda b: (b, c))
    state = pl.BlockSpec((bb, H_HEADS, H_KEY, H_VAL), lambda b: (b, 0, 0, 0))
    return pl.pallas_call(
        _hgrn_sample_kernel,
        grid=(bsz // bb,),
        in_specs=[col(COL_Q), col(COL_LOGF), col(0), col(COL_I), col(COL_OG), col(0), col(COL_GM),
                  col(COL_GH), state] + [_resident(w.shape) for w in hgrn_w],
        out_specs=[col(0), state],
        out_shape=[jax.ShapeDtypeStruct((proj.shape[0], D_MODEL), F32),
                   jax.ShapeDtypeStruct((bsz, H_HEADS, H_KEY, H_VAL), F32)],
        compiler_params=_params(("parallel",)),
        name="hgrn_sample",
    )(proj, proj, kk, proj, proj, ym, proj, proj, s0, *hgrn_w)


def _pad_lanes(v):
    return jnp.pad(v, (0, LANES - v.shape[0])).reshape(1, LANES)


def kernel(x_prompt, x_sample, state_conv, state_ssm, state_hgrn, ffn1_w_gate, ffn1_w_up, ffn1_w_down, ln1_g, ln1_b, w_in, conv_w, conv_b, dt_bias, a_log, d_skip, m_norm_w, w_m_out, hgrn_lb_param, h_norm_w, w_h_out, w_o, ln2_g, ln2_b, ffn2_w_gate, ffn2_w_up, ffn2_w_down, ln3_g, ln3_b):
    assert w_in.shape[0] == 1, "single trunk layer"
    bp, lp, _ = x_prompt.shape
    bs, ls, _ = x_sample.shape
    assert ls == SAMPLE_ROWS - M_CONV

    row = lambda v: v[0].reshape(1, -1)
    f1 = (ffn1_w_gate[0], ffn1_w_up[0], ffn1_w_down[0])
    f2 = (ffn2_w_gate[0].astype(BF16), ffn2_w_up[0].astype(BF16), ffn2_w_down[0].astype(BF16))
    wi = w_in[0]
    dt_end = W_DT_AT + M_HEADS
    w_dt = jnp.pad(wi[:, W_DT_AT:dt_end], ((0, 0), (0, LANES - M_HEADS))).astype(BF16)
    proj_w = (wi[:, :W_DT_AT].astype(BF16), wi[:, dt_end:].astype(BF16), w_dt,
              _pad_lanes(dt_bias[0]), hgrn_lb_param)
    conv_wb = (conv_w[0], row(conv_b))
    ssd_w = (_pad_lanes(a_log[0]), jnp.repeat(d_skip[0], M_HEAD_DIM).reshape(1, M_INNER),
             row(m_norm_w), w_m_out[0].astype(BF16))
    hgrn_w = (row(h_norm_w), w_h_out[0].astype(BF16))
    merge_w = (w_o[0], row(ln2_g), row(ln2_b)) + f2 + (row(ln3_g), row(ln3_b))

    x1 = _ffn_ln(x_prompt.reshape(bp * lp, D_MODEL), *f1, row(ln1_g), row(ln1_b))
    mix, conv_p, ssm_p, hg_p = _proj_mixer_prompt(x1.reshape(bp, lp, D_MODEL), proj_w, conv_wb,
                                                  ssd_w, hgrn_w)
    y_prompt = _merge_ffn(x1, mix.reshape(-1, D_MODEL), *merge_w).reshape(bp, lp, D_MODEL)
    new_conv_p = conv_p[:, SUBLANES - (M_CONV - 1):, :][None]
    new_ssm_p = ssm_p.reshape(1, bp, M_HEADS, M_HEAD_DIM, M_STATE)
    new_hg_p = hg_p[None]

    x1s = _ffn_ln(x_sample.reshape(bs * ls, D_MODEL), *f1, row(ln1_g), row(ln1_b))
    projs, dts, kks = _in_proj(x1s, proj_w)
    yms, conv_s, ssm_s = _ssd_sample(projs, dts, state_conv[0],
                                     state_ssm[0].reshape(bs, M_INNER, M_STATE), conv_wb, ssd_w)
    mixs, hg_s = _hgrn_sample(projs, kks, yms, state_hgrn[0], hgrn_w)
    y_sample = _merge_ffn(x1s, mixs, *merge_w).reshape(bs, ls, D_MODEL)
    new_conv_s = conv_s[None]
    new_ssm_s = ssm_s.reshape(1, bs, M_HEADS, M_HEAD_DIM, M_STATE)
    new_hg_s = hg_s[None]

    return (y_prompt, y_sample, new_conv_p, new_ssm_p, new_hg_p, new_conv_s, new_ssm_s, new_hg_s)
```

```python
import jax
import jax.numpy as jnp
from jax import lax
from jax.experimental import pallas as pl
from jax.experimental.pallas import tpu as pltpu

F32 = jnp.float32
BF16 = jnp.bfloat16

D_MODEL = 1024
D_FF = 2816
M_HEADS = 16
M_HEAD_DIM = 64
M_GROUPS = 4
M_STATE = 128
M_INNER = M_HEADS * M_HEAD_DIM
M_CONV = 4
M_CONV_DIM = M_INNER + 2 * M_GROUPS * M_STATE
HEADS_PER_GROUP = M_HEADS // M_GROUPS
GROUP_COLS = HEADS_PER_GROUP * M_HEAD_DIM
H_HEADS = 8
H_KEY = 128
H_VAL = 128
H_CHUNK = 32
ALPHA = 2.0 ** 0.25
EPS = 1e-5

LANES = 128
SUBLANES = 8
VMEM_LIMIT = 56 * 1024 * 1024

FF_CHUNK = D_FF // 2
FFN1_TM = 512
FFN2_TM = 512
PROJ_MAIN = 9 * D_MODEL
COL_X, COL_Q, COL_GM = 0, 1, 2
COL_BC, COL_I, COL_GH = 3, 4, 5
COL_Z, COL_LOGF, COL_OG = 6, 7, 8
W_DT_AT = M_INNER + M_CONV_DIM
PROJ_PIECES = ((COL_X, ("a", M_INNER)), (COL_Q, ("b", 0)), (COL_GM, ("b", 4 * D_MODEL)),
               (COL_BC, ("a", 2 * M_INNER)), (COL_I, ("b", 2 * D_MODEL)), (COL_GH, ("b", 5 * D_MODEL)),
               (COL_Z, ("a", 0)), (COL_LOGF, ("b", D_MODEL)), (COL_OG, ("b", 3 * D_MODEL)))

SSD_CHUNK = 128
HGRN_BLOCK = 128
MIXER_TL = 256
MIXER_INTERLEAVE = (5, 20)
SAMPLE_ROWS = 8
SAMPLE_FIRST = M_CONV - 1
SAMPLE_BATCH_BLOCK = 8


def _sigmoid(x):
    return 1.0 / (1.0 + jnp.exp(-x))


def _silu(x):
    return x * _sigmoid(x)


def _softplus(x):
    return jnp.maximum(x, 0.0) + jnp.log(1.0 + jnp.exp(-jnp.abs(x)))


def _dot(a, b):
    return jnp.dot(a, b, preferred_element_type=F32)


def _dot_nt(a, b):
    return lax.dot_general(a, b, (((1,), (1,)), ((), ())), preferred_element_type=F32)


def _pad_rows(a):
    q = a.shape[0]
    if q == LANES:
        return a
    return jnp.concatenate([a, jnp.zeros((LANES - q, a.shape[1]), a.dtype)], axis=0)


def _chunk_cumsum(x, ch):
    rows, cols = x.shape
    g = rows // SUBLANES
    y = x.reshape(g, SUBLANES, cols)
    sub = lax.broadcasted_iota(jnp.int32, (g, SUBLANES, cols), 1)
    s = 1
    while s < SUBLANES:
        y = y + jnp.where(sub >= s, pltpu.roll(y, s, 1), 0.0)
        s *= 2
    per = ch // SUBLANES
    if per > 1:
        y4 = y.reshape(g // per, per, SUBLANES, cols)
        carry, outs = None, []
        for j in range(per):
            yj = y4[:, j]
            outs.append(yj if carry is None else yj + carry)
            tot = jnp.broadcast_to(yj[:, SUBLANES - 1:SUBLANES, :], yj.shape)
            carry = tot if carry is None else carry + tot
        y = jnp.stack(outs, axis=1).reshape(g, SUBLANES, cols)
    return y.reshape(rows, cols)


def _layer_norm(y, g, b):
    mu = jnp.mean(y, axis=-1, keepdims=True)
    yc = y - mu
    var = jnp.mean(yc * yc, axis=-1, keepdims=True)
    return yc * lax.rsqrt(var + EPS) * g + b


def _swiglu(x, wg_ref, wu_ref, wd_ref):
    xb = x.astype(wg_ref.dtype)
    acc = None
    for c in range(D_FF // FF_CHUNK):
        sl = slice(c * FF_CHUNK, (c + 1) * FF_CHUNK)
        hg = _dot(xb, wg_ref[:, sl])
        hu = _dot(xb, wu_ref[:, sl])
        act = (_silu(hg) * hu).astype(wd_ref.dtype)
        part = _dot(act, wd_ref[sl, :])
        acc = part if acc is None else acc + part
    return acc


def _ffn_ln_kernel(x_ref, wg_ref, wu_ref, wd_ref, g_ref, b_ref, o_ref):
    x = x_ref[...]
    y = ALPHA * x + 0.5 * _swiglu(x, wg_ref, wu_ref, wd_ref)
    o_ref[...] = _layer_norm(y, g_ref[...], b_ref[...])


def _activate_piece(c, p, lbp_ref, sub=slice(0, D_MODEL)):
    if c in (COL_GM, COL_GH):
        return _sigmoid(p)
    if c in (COL_Z, COL_OG):
        return _silu(p)
    if c == COL_LOGF:
        return _hgrn_gates(p, lbp_ref, sub)
    return p


def _proj_piece(c, xb, wa_ref, wb_ref, lbp_ref, sub=slice(0, D_MODEL), activate=True):
    part, at = dict(PROJ_PIECES)[c]
    w_ref = wa_ref if part == "a" else wb_ref
    p = _dot(xb, w_ref[:, at + sub.start:at + sub.stop])
    return _activate_piece(c, p, lbp_ref, sub) if activate else p


def _proj_dt(xb, wdt_ref, dtb_ref):
    return _softplus(_dot(xb, wdt_ref[...]) + dtb_ref[...])


def _in_proj_kernel(x_ref, wa_ref, wb_ref, wdt_ref, dtb_ref, lbp_ref, o_ref, odt_ref, k_ref):
    xb = x_ref[...].astype(BF16)
    for c, _ in PROJ_PIECES:
        cols = slice(c * D_MODEL, (c + 1) * D_MODEL)
        p = _proj_piece(c, xb, wa_ref, wb_ref, lbp_ref)
        if c == COL_LOGF:
            o_ref[:, cols], k_ref[...] = p
        else:
            o_ref[:, cols] = p
    odt_ref[...] = _proj_dt(xb, wdt_ref, dtb_ref)


def _merge_ffn_kernel(x1_ref, mix_ref, wo_ref, g2_ref, b2_ref,
                      wg_ref, wu_ref, wd_ref, g3_ref, b3_ref, o_ref):
    x2 = _layer_norm(ALPHA * x1_ref[...] + _dot(mix_ref[...], wo_ref[...]),
                     g2_ref[...], b2_ref[...])
    y = ALPHA * x2 + 0.5 * _swiglu(x2, wg_ref, wu_ref, wd_ref)
    o_ref[...] = _layer_norm(y, g3_ref[...], b3_ref[...])


def _ssd_streams(chunks, h_prev, a_row, stage_major):
    q = chunks[0][1].shape[0]
    n = len(chunks)
    row_i = lax.broadcasted_iota(jnp.int32, (q, LANES), 0)
    col_i = lax.broadcasted_iota(jnp.int32, (q, LANES), 1)
    causal = col_i <= row_i
    head_blk = jnp.right_shift(lax.broadcasted_iota(jnp.int32, (q, GROUP_COLS), 1),
                               M_HEAD_DIM.bit_length() - 1)
    groups = [slice(g * M_STATE, (g + 1) * M_STATE) for g in range(M_GROUPS)]
    rows_r = [slice(r * M_HEAD_DIM, (r + 1) * M_HEAD_DIM) for r in range(HEADS_PER_GROUP)]

    pre, ops = {}, {}

    def prepare(i):
        _, xm, bm, cm, dt = chunks[i]
        a_cum = _chunk_cumsum(dt * a_row, q)
        a_last = a_cum[q - 1:q, :]
        to_end = jnp.exp(a_last - a_cum) * dt
        pre[i] = dict(
            a_cum=a_cum, e_last=jnp.exp(a_last), e_cum=jnp.exp(a_cum),
            a_cum_t=jnp.transpose(_pad_rows(a_cum)),
            dt_t=jnp.transpose(_pad_rows(dt)),
            to_end_t=jnp.transpose(_pad_rows(to_end)))
        for g in range(M_GROUPS):
            xg = _pad_rows(xm[:, g * GROUP_COLS:(g + 1) * GROUP_COLS])
            ops[i, g] = dict(bg=_pad_rows(bm[:, groups[g]]).astype(BF16),
                             cg=cm[:, groups[g]].astype(BF16),
                             xg_b=xg.astype(BF16), xg_t=jnp.transpose(xg))

    def mm_cb(i, g):
        return _dot_nt(ops[i, g]["cg"], ops[i, g]["bg"])

    def state_lhs(i, g, r):
        h = g * HEADS_PER_GROUP + r
        return (ops[i, g]["xg_t"][rows_r[r], :] * pre[i]["to_end_t"][h:h + 1, :]).astype(BF16)

    def mm_state(i, g, r):
        return _dot(state_lhs(i, g, r), ops[i, g]["bg"])

    def intra_lhs(i, g, r, cb):
        h = g * HEADS_PER_GROUP + r
        seg = pre[i]["a_cum"][:, h:h + 1] - pre[i]["a_cum_t"][h:h + 1, :]
        decay = jnp.where(causal, jnp.exp(jnp.where(causal, seg, 0.0)), 0.0)
        return (cb * decay * pre[i]["dt_t"][h:h + 1, :]).astype(BF16)

    def mm_intra(i, g, r, cb):
        return _dot(intra_lhs(i, g, r, cb), ops[i, g]["xg_b"])

    def mm_inter(i, g, blocks):
        hg = jnp.concatenate(blocks[g * HEADS_PER_GROUP:(g + 1) * HEADS_PER_GROUP], axis=0)
        return _dot_nt(ops[i, g]["cg"], hg.astype(BF16))

    def combine(i, g, parts, y_inter):
        acc = jnp.zeros((q, GROUP_COLS), F32)
        for r in range(HEADS_PER_GROUP):
            h = g * HEADS_PER_GROUP + r
            acc = jnp.where(head_blk == r, parts[r] + y_inter * pre[i]["e_cum"][:, h:h + 1], acc)
        return acc

    cur = {seq: list(blocks) for seq, blocks in h_prev.items()}
    ys = []
    if not stage_major:
        for i, chunk in enumerate(chunks):
            prepare(i)
            enter = cur[chunk[0]]
            cbs = [mm_cb(i, g) for g in range(M_GROUPS)]
            y_inters = [mm_inter(i, g, enter) for g in range(M_GROUPS)]
            yield
            cols, new = [], []
            for g in range(M_GROUPS):
                acc = jnp.zeros((q, GROUP_COLS), F32)
                for r in range(HEADS_PER_GROUP):
                    h = g * HEADS_PER_GROUP + r
                    y_h = mm_intra(i, g, r, cbs[g]) + y_inters[g] * pre[i]["e_cum"][:, h:h + 1]
                    acc = jnp.where(head_blk == r, y_h, acc)
                    new.append(pre[i]["e_last"][:, h:h + 1] * enter[h] + mm_state(i, g, r))
                    yield
                cols.append(acc)
            cur[chunk[0]] = new
            ys.append(jnp.concatenate(cols, axis=1))
        return ys, cur
    for i in range(n):
        prepare(i)
    cb = {(i, g): mm_cb(i, g) for i in range(n) for g in range(M_GROUPS)}
    st = {(i, g, r): mm_state(i, g, r)
          for i in range(n) for g in range(M_GROUPS) for r in range(HEADS_PER_GROUP)}
    part = {(i, g, r): mm_intra(i, g, r, cb[i, g])
            for i in range(n) for g in range(M_GROUPS) for r in range(HEADS_PER_GROUP)}
    enter = []
    for i, chunk in enumerate(chunks):
        seq = chunk[0]
        enter.append(list(cur[seq]))
        cur[seq] = [pre[i]["e_last"][:, h:h + 1] * cur[seq][h]
                    + st[i, h // HEADS_PER_GROUP, h % HEADS_PER_GROUP] for h in range(M_HEADS)]
    for i in range(n):
        cols = []
        for g in range(M_GROUPS):
            y_inter = mm_inter(i, g, enter[i])
            cols.append(combine(i, g, [part[i, g, r] for r in range(HEADS_PER_GROUP)], y_inter))
        ys.append(jnp.concatenate(cols, axis=1))
    return ys, cur


def _ssd_gated_norm(y_ssd, xm, z_act, dskip, mnw):
    g = (y_ssd + dskip * xm) * z_act
    outs = []
    for k in range(M_GROUPS):
        gk = g[:, k * GROUP_COLS:(k + 1) * GROUP_COLS]
        outs.append(gk * lax.rsqrt(jnp.mean(gk * gk, axis=-1, keepdims=True) + EPS))
    return jnp.concatenate(outs, axis=1) * mnw


def _state_blocks(h):
    return [h[k * M_HEAD_DIM:(k + 1) * M_HEAD_DIM, :] for k in range(M_HEADS)]


def _conv_silu(u, prev8, cw, cbias):
    tl = u.shape[0]
    ext = jnp.concatenate([prev8, u], axis=0)
    conv = cbias + cw[M_CONV - 1:M_CONV, :] * u
    for j in range(1, M_CONV):
        shifted = pltpu.roll(ext, j, 0)[SUBLANES:SUBLANES + tl, :]
        conv = conv + cw[M_CONV - 1 - j:M_CONV - j, :] * shifted
    return _silu(conv)


def _drain(gen):
    try:
        while True:
            next(gen)
    except StopIteration as stop:
        return stop.value


class _Stream:
    def __init__(self, gen):
        self.gen, self.done, self.value = gen, False, None

    def step(self, n):
        for _ in range(n):
            if self.done:
                return
            try:
                next(self.gen)
            except StopIteration as stop:
                self.done, self.value = True, stop.value


def _proj_mixer_prompt_kernel(x1_ref, wa_ref, wb_ref, wdt_ref, dtb_ref, lbp_ref, cw_ref, cb_ref, alog_ref,
                              dskip_ref, mnw_ref, wmo_ref, hnw_ref, who_ref,
                              mix_ref, conv_out_ref, ssm_out_ref, s_out_ref, prev_scr, h_scr, s_scr):
    t = pl.program_id(1)
    tl = x1_ref.shape[1]

    @pl.when(t == 0)
    def _():
        prev_scr[...] = jnp.zeros_like(prev_scr)
        h_scr[...] = jnp.zeros_like(h_scr)
        s_scr[...] = jnp.zeros_like(s_scr)

    xb = x1_ref[0].astype(BF16)
    piece = lambda c: _proj_piece(c, xb, wa_ref, wb_ref, lbp_ref)
    x_raw, bc_raw = piece(COL_X), piece(COL_BC)
    dt = _proj_dt(xb, wdt_ref, dtb_ref)
    tail = jnp.concatenate([x_raw[tl - SUBLANES:tl, :], bc_raw[tl - SUBLANES:tl, :]], axis=1)
    xm = _conv_silu(x_raw, prev_scr[:, :M_INNER], cw_ref[:, :M_INNER], cb_ref[:, :M_INNER])
    bc = _conv_silu(bc_raw, prev_scr[:, M_INNER:], cw_ref[:, M_INNER:], cb_ref[:, M_INNER:])
    prev_scr[...] = tail
    bm = bc[:, :M_GROUPS * M_STATE]
    cm = bc[:, M_GROUPS * M_STATE:]
    a_row = -jnp.exp(alog_ref[...])
    chunks = []
    for c in range(tl // SSD_CHUNK):
        rs = slice(c * SSD_CHUNK, (c + 1) * SSD_CHUNK)
        chunks.append((0, xm[rs], bm[rs], cm[rs], dt[rs]))
    got = {}

    def remaining_pieces():
        for c in (COL_LOGF, COL_Q, COL_I, COL_Z, COL_OG, COL_GM, COL_GH):
            got[c] = _proj_piece(c, xb, wa_ref, wb_ref, lbp_ref, activate=(c == COL_LOGF))
            yield

    proj = _Stream(remaining_pieces())

    def need(c):
        while c not in got:
            proj.step(1)
        return got[c] if c == COL_LOGF else _activate_piece(c, got[c], lbp_ref)

    def ssd_branch():
        ys, h_new = yield from _ssd_streams(chunks, {0: _state_blocks(h_scr[...])}, a_row,
                                            stage_major=False)
        gn = _ssd_gated_norm(jnp.concatenate(ys, axis=0), xm, need(COL_Z), dskip_ref[...], mnw_ref[...])
        ym = _dot(gn.astype(BF16), wmo_ref[...])
        yield
        return ym, h_new

    def hgrn_branch():
        (log_f, k), q, v = need(COL_LOGF), need(COL_Q), need(COL_I)
        blocks = []
        for b in range(tl // HGRN_BLOCK):
            rs = slice(b * HGRN_BLOCK, (b + 1) * HGRN_BLOCK)
            blocks.append((0, q[rs], log_f[rs], k[rs], v[rs]))
        outs, s_new = yield from _hgrn_streams(blocks, {0: [s_scr[h] for h in range(H_HEADS)]}, H_CHUNK)
        o = jnp.concatenate(outs, axis=0) * hnw_ref[...] * need(COL_OG)
        yh = _dot(o.astype(BF16), who_ref[...])
        yield
        return yh, s_new

    ssd, hgrn = _Stream(ssd_branch()), _Stream(hgrn_branch())
    n_ssd, n_hgrn = MIXER_INTERLEAVE
    while COL_I not in got:
        proj.step(1)
        ssd.step(n_ssd)
    while not (proj.done and ssd.done and hgrn.done):
        proj.step(1)
        ssd.step(n_ssd)
        hgrn.step(n_hgrn)
    (ym, h_new), (yh, s_new) = ssd.value, hgrn.value
    for h in range(M_HEADS):
        h_scr[h * M_HEAD_DIM:(h + 1) * M_HEAD_DIM, :] = h_new[0][h]
    for h in range(H_HEADS):
        s_scr[h] = s_new[0][h]
    mix_ref[0] = need(COL_GM) * ym + need(COL_GH) * yh

    @pl.when(t == pl.num_programs(1) - 1)
    def _():
        conv_out_ref[0] = tail
        ssm_out_ref[0] = h_scr[...]
        for h in range(H_HEADS):
            s_out_ref[0, h] = jnp.transpose(s_scr[h])


def _tile_rows(tokens, before):
    n, c = tokens.shape
    parts = [tokens, jnp.zeros((SAMPLE_ROWS - n - before, c), tokens.dtype)]
    if before:
        parts.insert(0, jnp.zeros((before, c), tokens.dtype))
    return jnp.concatenate(parts, axis=0)


def _ssd_sample_kernel(x_ref, bc_ref, z_ref, dt_ref, cs_ref, h0_ref, cw_ref, cb_ref, alog_ref,
                       dskip_ref, mnw_ref, wmo_ref, ym_ref, conv_out_ref, ssm_out_ref):
    bb = h0_ref.shape[0]
    n_tok = x_ref.shape[0] // bb
    a_row = -jnp.exp(alog_ref[...])
    chunks, zs, h_prev = [], [], {}
    for i in range(bb):
        rs = slice(i * n_tok, (i + 1) * n_tok)
        raw = jnp.concatenate([x_ref[rs, :], bc_ref[rs, :]], axis=1)
        u = jnp.concatenate([cs_ref[i], raw, jnp.zeros((1, M_CONV_DIM), F32)], axis=0)
        conv_out_ref[i] = u[n_tok:n_tok + M_CONV - 1, :]
        conv = cb_ref[...] + cw_ref[M_CONV - 1:M_CONV, :] * u
        for j in range(1, M_CONV):
            conv = conv + cw_ref[M_CONV - 1 - j:M_CONV - j, :] * pltpu.roll(u, j, 0)
        xbc = _silu(conv)
        xm = xbc[:, :M_INNER]
        bm = xbc[:, M_INNER:M_INNER + M_GROUPS * M_STATE]
        cm = xbc[:, M_INNER + M_GROUPS * M_STATE:]
        chunks.append((i, xm, bm, cm, _tile_rows(dt_ref[rs, :], SAMPLE_FIRST)))
        zs.append(_tile_rows(z_ref[rs, :], SAMPLE_FIRST))
        h_prev[i] = _state_blocks(h0_ref[i])
    ys, h_new = _drain(_ssd_streams(chunks, h_prev, a_row, stage_major=True))
    gns = []
    for i in range(bb):
        for h in range(M_HEADS):
            ssm_out_ref[i, h * M_HEAD_DIM:(h + 1) * M_HEAD_DIM, :] = h_new[i][h]
        gns.append(_ssd_gated_norm(ys[i], chunks[i][1], zs[i], dskip_ref[...], mnw_ref[...]))
    ym = _dot(jnp.concatenate(gns, axis=0).astype(BF16), wmo_ref[...])
    for i in range(bb):
        ym_ref[i * n_tok:(i + 1) * n_tok, :] = ym[i * SAMPLE_ROWS + SAMPLE_FIRST:
                                                  i * SAMPLE_ROWS + SAMPLE_FIRST + n_tok, :]


def _hgrn_gates(f_raw, lbp_ref, cols=slice(0, H_HEADS * H_KEY)):
    p0 = lbp_ref[0:1, cols]
    p1 = lbp_ref[1:2, cols]
    m = jnp.maximum(p0, p1)
    e0 = jnp.exp(p0 - m)
    e1 = jnp.exp(p1 - m)
    lb = e0 / (e0 + e1)
    log_f = jnp.log(lb + (1.0 - lb) * _sigmoid(f_raw))
    k = (1.0 - lb) * _sigmoid(-f_raw)
    return log_f, k


def _hgrn_streams(blocks, s_prev, ch):
    rows = blocks[0][1].shape[0]
    n = len(blocks)
    shift = ch.bit_length() - 1
    n_chunks = rows // ch
    row_i = lax.broadcasted_iota(jnp.int32, (rows, LANES), 0)
    col_i = lax.broadcasted_iota(jnp.int32, (rows, LANES), 1)
    causal = (jnp.right_shift(row_i, shift) == jnp.right_shift(col_i, shift)) & (col_i <= row_i)
    chunk_t = jnp.right_shift(lax.broadcasted_iota(jnp.int32, (LANES, LANES), 1), shift)
    heads = [slice(h * H_KEY, (h + 1) * H_KEY) for h in range(H_HEADS)]
    chunks = [slice(c * ch, (c + 1) * ch) for c in range(n_chunks)]
    items = [(i, h) for i in range(n) for h in range(H_HEADS)]

    pre = []
    for (_, q, log_f, k, v) in blocks:
        b_cum = _chunk_cumsum(log_f, ch)
        lasts = [b_cum[c * ch + ch - 1:c * ch + ch, :] for c in range(n_chunks)]
        b_last = jnp.concatenate([jnp.broadcast_to(l, (ch, l.shape[1])) for l in lasts], axis=0)
        pre.append(dict(qd=q * jnp.exp(b_cum), kd=k * jnp.exp(-b_cum),
                        k_end=k * jnp.exp(b_last - b_cum), lasts=lasts, v=v))
    qh = {(i, h): pre[i]["qd"][:, heads[h]].astype(BF16) for (i, h) in items}
    vh = {(i, h): _pad_rows(pre[i]["v"][:, heads[h]]) for (i, h) in items}
    sc, ds, y_intra, y_inter = {}, {}, {}, {}
    for (i, h) in items:
        sc[i, h] = _dot_nt(qh[i, h], _pad_rows(pre[i]["kd"][:, heads[h]]).astype(BF16))
        yield
    for (i, h) in items:
        v_t = jnp.transpose(vh[i, h])
        ke = _pad_rows(pre[i]["k_end"][:, heads[h]]).astype(BF16)
        ds[i, h] = []
        for c in range(n_chunks):
            v_tc = v_t if n_chunks == 1 else jnp.where(chunk_t == c, v_t, 0.0)
            ds[i, h].append(_dot(v_tc.astype(BF16), ke))
            yield
    for (i, h) in items:
        y_intra[i, h] = _dot(jnp.where(causal, sc[i, h], 0.0).astype(BF16), vh[i, h].astype(BF16))
        yield
    cur = {seq: list(states) for seq, states in s_prev.items()}
    enter = {}
    for i, blk in enumerate(blocks):
        seq = blk[0]
        for h in range(H_HEADS):
            s = cur[seq][h]
            per_chunk = []
            for c in range(n_chunks):
                per_chunk.append(s)
                s = jnp.exp(pre[i]["lasts"][c][:, heads[h]]) * s + ds[i, h][c]
            enter[i, h] = per_chunk
            cur[seq][h] = s
    for (i, h) in items:
        y_inter[i, h] = []
        for c in range(n_chunks):
            y_inter[i, h].append(_dot_nt(qh[i, h][chunks[c]], enter[i, h][c].astype(BF16)))
            yield
    outs = []
    for i in range(n):
        cols = []
        for h in range(H_HEADS):
            pieces = [y_intra[i, h][chunks[c]] + y_inter[i, h][c] for c in range(n_chunks)]
            o_h = pieces[0] if n_chunks == 1 else jnp.concatenate(pieces, axis=0)
            cols.append(o_h * lax.rsqrt(jnp.mean(o_h * o_h, axis=-1, keepdims=True) + EPS))
        outs.append(jnp.concatenate(cols, axis=1))
    return outs, cur


def _hgrn_sample_kernel(q_ref, logf_ref, k_ref, i_ref, og_ref, ym_ref, gm_ref, gh_ref, s0_ref,
                        hnw_ref, who_ref, mix_ref, s_out_ref):
    bb = s0_ref.shape[0]
    n_tok = q_ref.shape[0] // bb
    rows = [slice(i * n_tok, (i + 1) * n_tok) for i in range(bb)]
    blocks, s_prev = [], {}
    for i in range(bb):
        blocks.append((i, _tile_rows(q_ref[rows[i], :], 0), _tile_rows(logf_ref[rows[i], :], 0),
                       _tile_rows(k_ref[rows[i], :], 0), _tile_rows(i_ref[rows[i], :], 0)))
        s_prev[i] = [jnp.transpose(s0_ref[i, h]) for h in range(H_HEADS)]
    outs, s_new = _drain(_hgrn_streams(blocks, s_prev, SAMPLE_ROWS))
    os_ = []
    for i in range(bb):
        for h in range(H_HEADS):
            s_out_ref[i, h] = jnp.transpose(s_new[i][h])
        os_.append(outs[i] * hnw_ref[...] * _tile_rows(og_ref[rows[i], :], 0))
    yh = _dot(jnp.concatenate(os_, axis=0).astype(BF16), who_ref[...])
    for i in range(bb):
        mix_ref[rows[i], :] = (gm_ref[rows[i], :] * ym_ref[rows[i], :]
                               + gh_ref[rows[i], :] * yh[i * SAMPLE_ROWS:i * SAMPLE_ROWS + n_tok, :])


def _mixer_sample_kernel(x_ref, bc_ref, z_ref, dt_ref, q_ref, logf_ref, k_ref, i_ref, og_ref, gm_ref,
                         gh_ref, cs_ref, h0_ref, s0_ref, cw_ref, cb_ref, alog_ref, dskip_ref, mnw_ref,
                         wmo_ref, hnw_ref, who_ref, mix_ref, conv_out_ref, ssm_out_ref, s_out_ref):
    bb = h0_ref.shape[0]
    n_tok = x_ref.shape[0] // bb
    rows = [slice(i * n_tok, (i + 1) * n_tok) for i in range(bb)]
    a_row = -jnp.exp(alog_ref[...])
    chunks, zs, h_prev, blocks, s_prev = [], [], {}, [], {}
    for i in range(bb):
        raw = jnp.concatenate([x_ref[rows[i], :], bc_ref[rows[i], :]], axis=1)
        u = jnp.concatenate([cs_ref[i], raw, jnp.zeros((1, M_CONV_DIM), F32)], axis=0)
        conv_out_ref[i] = u[n_tok:n_tok + M_CONV - 1, :]
        conv = cb_ref[...] + cw_ref[M_CONV - 1:M_CONV, :] * u
        for j in range(1, M_CONV):
            conv = conv + cw_ref[M_CONV - 1 - j:M_CONV - j, :] * pltpu.roll(u, j, 0)
        xbc = _silu(conv)
        chunks.append((i, xbc[:, :M_INNER], xbc[:, M_INNER:M_INNER + M_GROUPS * M_STATE],
                       xbc[:, M_INNER + M_GROUPS * M_STATE:], _tile_rows(dt_ref[rows[i], :], SAMPLE_FIRST)))
        zs.append(_tile_rows(z_ref[rows[i], :], SAMPLE_FIRST))
        h_prev[i] = _state_blocks(h0_ref[i])
        blocks.append((i, _tile_rows(q_ref[rows[i], :], 0), _tile_rows(logf_ref[rows[i], :], 0),
                       _tile_rows(k_ref[rows[i], :], 0), _tile_rows(i_ref[rows[i], :], 0)))
        s_prev[i] = [jnp.transpose(s0_ref[i, h]) for h in range(H_HEADS)]
    hgrn = _Stream(_hgrn_streams(blocks, s_prev, SAMPLE_ROWS))
    ssd = _Stream(_ssd_streams(chunks, h_prev, a_row, stage_major=True))
    hgrn.step(bb * H_HEADS)
    while not (ssd.done and hgrn.done):
        ssd.step(1)
        hgrn.step(bb * H_HEADS)
    (ys, h_new), (outs, s_new) = ssd.value, hgrn.value
    gns, os_ = [], []
    for i in range(bb):
        for h in range(M_HEADS):
            ssm_out_ref[i, h * M_HEAD_DIM:(h + 1) * M_HEAD_DIM, :] = h_new[i][h]
        for h in range(H_HEADS):
            s_out_ref[i, h] = jnp.transpose(s_new[i][h])
        gns.append(_ssd_gated_norm(ys[i], chunks[i][1], zs[i], dskip_ref[...], mnw_ref[...]))
        os_.append(outs[i] * hnw_ref[...] * _tile_rows(og_ref[rows[i], :], 0))
    ym = _dot(jnp.concatenate(gns, axis=0).astype(BF16), wmo_ref[...])
    yh = _dot(jnp.concatenate(os_, axis=0).astype(BF16), who_ref[...])
    for i in range(bb):
        ym_i = ym[i * SAMPLE_ROWS + SAMPLE_FIRST:i * SAMPLE_ROWS + SAMPLE_FIRST + n_tok, :]
        yh_i = yh[i * SAMPLE_ROWS:i * SAMPLE_ROWS + n_tok, :]
        mix_ref[rows[i], :] = gm_ref[rows[i], :] * ym_i + gh_ref[rows[i], :] * yh_i


def _resident(shape):
    nd = len(shape)
    return pl.BlockSpec(shape, lambda *_: (0,) * nd, pipeline_mode=pl.Buffered(1))


def _params(semantics):
    return pltpu.CompilerParams(dimension_semantics=semantics, vmem_limit_bytes=VMEM_LIMIT)


def _row_tile(n_rows, want):
    tm = min(want, n_rows)
    assert n_rows % tm == 0
    return tm


def _ffn_ln(x, wg, wu, wd, g, b):
    n = x.shape[0]
    tm = _row_tile(n, FFN1_TM)
    row = pl.BlockSpec((tm, D_MODEL), lambda i: (i, 0))
    return pl.pallas_call(
        _ffn_ln_kernel,
        grid=(n // tm,),
        in_specs=[row, _resident(wg.shape), _resident(wu.shape), _resident(wd.shape),
                  _resident(g.shape), _resident(b.shape)],
        out_specs=row,
        out_shape=jax.ShapeDtypeStruct((n, D_MODEL), F32),
        compiler_params=_params(("parallel",)),
        name="ffn_ln",
    )(x, wg, wu, wd, g, b)


def _merge_ffn(x1, mix, *merge_w):
    n = x1.shape[0]
    tm = _row_tile(n, FFN2_TM)
    row = pl.BlockSpec((tm, D_MODEL), lambda i: (i, 0))
    return pl.pallas_call(
        _merge_ffn_kernel,
        grid=(n // tm,),
        in_specs=[row, row] + [_resident(w.shape) for w in merge_w],
        out_specs=row,
        out_shape=jax.ShapeDtypeStruct((n, D_MODEL), F32),
        compiler_params=_params(("parallel",)),
        name="merge_ffn",
    )(x1, mix, *merge_w)


def _in_proj(x1, proj_w):
    n = x1.shape[0]
    tm = _row_tile(n, 256)
    row = lambda width: pl.BlockSpec((tm, width), lambda i: (i, 0))
    return pl.pallas_call(
        _in_proj_kernel,
        grid=(n // tm,),
        in_specs=[row(D_MODEL)] + [_resident(w.shape) for w in proj_w],
        out_specs=[row(PROJ_MAIN), row(LANES), row(D_MODEL)],
        out_shape=[jax.ShapeDtypeStruct((n, PROJ_MAIN), F32),
                   jax.ShapeDtypeStruct((n, LANES), F32),
                   jax.ShapeDtypeStruct((n, D_MODEL), F32)],
        compiler_params=_params(("parallel",)),
        name="in_proj",
    )(x1, *proj_w)


def _proj_mixer_prompt(x1_3, proj_w, conv_wb, ssd_w, hgrn_w):
    bsz, length, _ = x1_3.shape
    tl = MIXER_TL
    rows = pl.BlockSpec((1, tl, D_MODEL), lambda b, t: (b, t, 0))
    once = lambda *shape: pl.BlockSpec((1,) + shape, lambda b, t: (b,) + (0,) * len(shape))
    weights = tuple(proj_w) + tuple(conv_wb) + tuple(ssd_w) + tuple(hgrn_w)
    return pl.pallas_call(
        _proj_mixer_prompt_kernel,
        grid=(bsz, length // tl),
        in_specs=[rows] + [_resident(w.shape) for w in weights],
        out_specs=[rows, once(SUBLANES, M_CONV_DIM), once(M_INNER, M_STATE),
                   once(H_HEADS, H_KEY, H_VAL)],
        out_shape=[jax.ShapeDtypeStruct((bsz, length, D_MODEL), F32),
                   jax.ShapeDtypeStruct((bsz, SUBLANES, M_CONV_DIM), F32),
                   jax.ShapeDtypeStruct((bsz, M_INNER, M_STATE), F32),
                   jax.ShapeDtypeStruct((bsz, H_HEADS, H_KEY, H_VAL), F32)],
        scratch_shapes=[pltpu.VMEM((SUBLANES, M_CONV_DIM), F32),
                        pltpu.VMEM((M_INNER, M_STATE), F32),
                        pltpu.VMEM((H_HEADS, H_VAL, H_KEY), F32)],
        compiler_params=_params(("parallel", "arbitrary")),
        name="proj_mixer_prompt",
    )(x1_3, *weights)


def _ssd_sample(proj, dt, conv0, ssm0, conv_wb, ssd_w):
    bsz = ssm0.shape[0]
    bb = SAMPLE_BATCH_BLOCK
    rows = bb * (proj.shape[0] // bsz)
    col = lambda width, c: pl.BlockSpec((rows, width), lambda b: (b, c))
    seq = lambda *shape: pl.BlockSpec((bb,) + shape, lambda b: (b,) + (0,) * len(shape))
    weights = tuple(conv_wb) + tuple(ssd_w)
    return pl.pallas_call(
        _ssd_sample_kernel,
        grid=(bsz // bb,),
        in_specs=[col(D_MODEL, COL_X), col(D_MODEL, COL_BC), col(D_MODEL, COL_Z), col(LANES, 0),
                  seq(M_CONV - 1, M_CONV_DIM), seq(M_INNER, M_STATE)]
                 + [_resident(w.shape) for w in weights],
        out_specs=[col(D_MODEL, 0), seq(M_CONV - 1, M_CONV_DIM), seq(M_INNER, M_STATE)],
        out_shape=[jax.ShapeDtypeStruct((proj.shape[0], D_MODEL), F32),
                   jax.ShapeDtypeStruct((bsz, M_CONV - 1, M_CONV_DIM), F32),
                   jax.ShapeDtypeStruct((bsz, M_INNER, M_STATE), F32)],
        compiler_params=_params(("parallel",)),
        name="ssd_sample",
    )(proj, proj, proj, dt, conv0, ssm0, *weights)


def _hgrn_sample(proj, kk, ym, s0, hgrn_w):
    bsz = s0.shape[0]
    bb = SAMPLE_BATCH_BLOCK
    rows = bb * (proj.shape[0] // bsz)
    col = lambda c: pl.BlockSpec((rows, D_MODEL), lambda b: (b, c))
    state = pl.BlockSpec((bb, H_HEADS, H_KEY, H_VAL), lambda b: (b, 0, 0, 0))
    return pl.pallas_call(
        _hgrn_sample_kernel,
        grid=(bsz // bb,),
        in_specs=[col(COL_Q), col(COL_LOGF), col(0), col(COL_I), col(COL_OG), col(0), col(COL_GM),
                  col(COL_GH), state] + [_resident(w.shape) for w in hgrn_w],
        out_specs=[col(0), state],
        out_shape=[jax.ShapeDtypeStruct((proj.shape[0], D_MODEL), F32),
                   jax.ShapeDtypeStruct((bsz, H_HEADS, H_KEY, H_VAL), F32)],
        compiler_params=_params(("parallel",)),
        name="hgrn_sample",
    )(proj, proj, kk, proj, proj, ym, proj, proj, s0, *hgrn_w)


def _mixer_sample(proj, dt, kk, conv0, ssm0, s0, conv_wb, ssd_w, hgrn_w):
    bsz = ssm0.shape[0]
    bb = SAMPLE_BATCH_BLOCK
    rows = bb * (proj.shape[0] // bsz)
    col = lambda width, c: pl.BlockSpec((rows, width), lambda b: (b, c))
    seq = lambda *shape: pl.BlockSpec((bb,) + shape, lambda b: (b,) + (0,) * len(shape))
    pc = lambda c: (proj, col(D_MODEL, c))
    ins = [pc(COL_X), pc(COL_BC), pc(COL_Z), (dt, col(LANES, 0)), pc(COL_Q), pc(COL_LOGF),
           (kk, col(D_MODEL, 0)), pc(COL_I), pc(COL_OG), pc(COL_GM), pc(COL_GH),
           (conv0, seq(M_CONV - 1, M_CONV_DIM)), (ssm0, seq(M_INNER, M_STATE)),
           (s0, seq(H_HEADS, H_KEY, H_VAL))]
    weights = tuple(conv_wb) + tuple(ssd_w) + tuple(hgrn_w)
    return pl.pallas_call(
        _mixer_sample_kernel,
        grid=(bsz // bb,),
        in_specs=[s for _, s in ins] + [_resident(w.shape) for w in weights],
        out_specs=[col(D_MODEL, 0), seq(M_CONV - 1, M_CONV_DIM), seq(M_INNER, M_STATE),
                   seq(H_HEADS, H_KEY, H_VAL)],
        out_shape=[jax.ShapeDtypeStruct((proj.shape[0], D_MODEL), F32),
                   jax.ShapeDtypeStruct((bsz, M_CONV - 1, M_CONV_DIM), F32),
                   jax.ShapeDtypeStruct((bsz, M_INNER, M_STATE), F32),
                   jax.ShapeDtypeStruct((bsz, H_HEADS, H_KEY, H_VAL), F32)],
        compiler_params=_params(("parallel",)),
        name="mixer_sample",
    )(*[a for a, _ in ins], *weights)


def _pad_lanes(v):
    return jnp.pad(v, (0, LANES - v.shape[0])).reshape(1, LANES)


def kernel(x_prompt, x_sample, state_conv, state_ssm, state_hgrn, ffn1_w_gate, ffn1_w_up, ffn1_w_down, ln1_g, ln1_b, w_in, conv_w, conv_b, dt_bias, a_log, d_skip, m_norm_w, w_m_out, hgrn_lb_param, h_norm_w, w_h_out, w_o, ln2_g, ln2_b, ffn2_w_gate, ffn2_w_up, ffn2_w_down, ln3_g, ln3_b):
    assert w_in.shape[0] == 1, "single trunk layer"
    bp, lp, _ = x_prompt.shape
    bs, ls, _ = x_sample.shape
    assert ls == SAMPLE_ROWS - M_CONV

    row = lambda v: v[0].reshape(1, -1)
    f1 = (ffn1_w_gate[0], ffn1_w_up[0], ffn1_w_down[0])
    f2 = (ffn2_w_gate[0].astype(BF16), ffn2_w_up[0].astype(BF16), ffn2_w_down[0].astype(BF16))
    wi = w_in[0]
    dt_end = W_DT_AT + M_HEADS
    w_dt = jnp.pad(wi[:, W_DT_AT:dt_end], ((0, 0), (0, LANES - M_HEADS))).astype(BF16)
    proj_w = (wi[:, :W_DT_AT].astype(BF16), wi[:, dt_end:].astype(BF16), w_dt,
              _pad_lanes(dt_bias[0]), hgrn_lb_param)
    conv_wb = (conv_w[0], row(conv_b))
    ssd_w = (_pad_lanes(a_log[0]), jnp.repeat(d_skip[0], M_HEAD_DIM).reshape(1, M_INNER),
             row(m_norm_w), w_m_out[0].astype(BF16))
    hgrn_w = (row(h_norm_w), w_h_out[0].astype(BF16))
    merge_w = (w_o[0], row(ln2_g), row(ln2_b)) + f2 + (row(ln3_g), row(ln3_b))

    x1 = _ffn_ln(x_prompt.reshape(bp * lp, D_MODEL), *f1, row(ln1_g), row(ln1_b))
    mix, conv_p, ssm_p, hg_p = _proj_mixer_prompt(x1.reshape(bp, lp, D_MODEL), proj_w, conv_wb,
                                                  ssd_w, hgrn_w)
    y_prompt = _merge_ffn(x1, mix.reshape(-1, D_MODEL), *merge_w).reshape(bp, lp, D_MODEL)
    new_conv_p = conv_p[:, SUBLANES - (M_CONV - 1):, :][None]
    new_ssm_p = ssm_p.reshape(1, bp, M_HEADS, M_HEAD_DIM, M_STATE)
    new_hg_p = hg_p[None]

    x1s = _ffn_ln(x_sample.reshape(bs * ls, D_MODEL), *f1, row(ln1_g), row(ln1_b))
    projs, dts, kks = _in_proj(x1s, proj_w)
    mixs, conv_s, ssm_s, hg_s = _mixer_sample(projs, dts, kks, state_conv[0],
                                              state_ssm[0].reshape(bs, M_INNER, M_STATE),
                                              state_hgrn[0], conv_wb, ssd_w, hgrn_w)
    y_sample = _merge_ffn(x1s, mixs, *merge_w).reshape(bs, ls, D_MODEL)
    new_conv_s = conv_s[None]
    new_ssm_s = ssm_s.reshape(1, bs, M_HEADS, M_HEAD_DIM, M_STATE)
    new_hg_s = hg_s[None]

    return (y_prompt, y_sample, new_conv_p, new_ssm_p, new_hg_p, new_conv_s, new_ssm_s, new_hg_s)
```
